```python
import jax, jax.numpy as jnp
from jax import lax
import numpy as np

D_MODEL = 1024
BATCH = 8
SEQ = 8192
DEPTH = 4

N_MIXERS = 3
GRID_W = 64
EPS = 1e-6
NEG_INF = -1e30
D_RNN = 1024
RG_BLOCKS = 8
RG_BLOCK_W = D_RNN // RG_BLOCKS
CONV_W = 4
CONV_LEFT = 2
RG_C = 8.0
HEAD_DIM = 128
GQA_HEADS = 8
GQA_KV_HEADS = 2
GQA_GROUP = GQA_HEADS // GQA_KV_HEADS
Q_BLOCK = 128
AXIAL_THETA = 10000.0
DIL_PATTERNS = ((128, 1), (512, 4), (2048, 16))
DIL_GROUPS = len(DIL_PATTERNS)
DIL_HEADS = 8
ROPE_THETA = 500000.0
ROT_DIMS = HEAD_DIM // 4
D_FF = 3584
N_EXPERTS = 8
TOP_K = 2
D_FF_EXPERT = 3584

kernel_name = 'hybrid_bidir_rglru_gqa_dilated_moe'


def rms_norm(x, g):
    xf = x.astype(jnp.float32)
    y = xf * lax.rsqrt(jnp.mean(xf * xf, axis=-1, keepdims=True) + EPS)
    return (y * g.astype(jnp.float32)).astype(x.dtype)


def inv_freq(dim, theta):
    return theta ** (-jnp.arange(0, dim, 2, dtype=jnp.float32) / dim)


def rope_half(x, ang):
    xf = x.astype(jnp.float32)
    n = ang.shape[-1]
    x1, x2 = xf[..., :n], xf[..., n:]
    c = jnp.cos(ang)[:, None, :]
    s = jnp.sin(ang)[:, None, :]
    return jnp.concatenate([x1 * c - x2 * s, x2 * c + x1 * s], axis=-1).astype(x.dtype)


def axial_rope(x, rows, cols):
    half = HEAD_DIM // 2
    xr = rope_half(x[..., :half], rows[:, None] * inv_freq(half, AXIAL_THETA)[None, :])
    xc = rope_half(x[..., half:], cols[:, None] * inv_freq(half, AXIAL_THETA)[None, :])
    return jnp.concatenate([xr, xc], axis=-1)


def partial_rope(x, pos):
    xr = rope_half(x[..., :ROT_DIMS], pos[:, None] * inv_freq(ROT_DIMS, ROPE_THETA)[None, :])
    return jnp.concatenate([xr, x[..., ROT_DIMS:]], axis=-1)


def swiglu(h, w_gate_up, w_down):
    g, u = jnp.split(h @ w_gate_up, 2, axis=-1)
    return (jax.nn.silu(g) * u) @ w_down


def dense_ffn(h, w_gate_up, w_down):
    return swiglu(h, w_gate_up, w_down)


def moe_ffn(h, router, router_b, w_gate_up, w_down):
    logits = (h @ router).astype(jnp.float32) + router_b.astype(jnp.float32)
    top_v, top_i = lax.top_k(logits, TOP_K)
    gates = jax.nn.softmax(top_v, axis=-1)
    combine = jnp.sum(jax.nn.one_hot(top_i, N_EXPERTS, dtype=jnp.float32) * gates[..., None], axis=-2)
    y = jnp.zeros(h.shape, jnp.float32)
    for e in range(N_EXPERTS):
        y = y + combine[..., e:e + 1] * swiglu(h, w_gate_up[e], w_down[e]).astype(jnp.float32)
    return y.astype(h.dtype)


def _linear_recurrence_combine(c1, c2):
    a1, b1 = c1
    a2, b2 = c2
    return (a1 * a2, a2 * b1 + b2)


def _rglru_scan(xs, w_a, b_a, w_x, b_x, lam):
    B, S, _ = xs.shape
    xb = xs.reshape(B, S, RG_BLOCKS, RG_BLOCK_W)

    def block_diag(w, b):
        return jnp.einsum('bsnc,ncd->bsnd', xb, w.astype(jnp.float32)).reshape(B, S, D_RNN) + b.astype(jnp.float32)

    r = jax.nn.sigmoid(block_diag(w_a, b_a))
    i = jax.nn.sigmoid(block_diag(w_x, b_x))
    log_a = -RG_C * r * jax.nn.softplus(-lam.astype(jnp.float32))
    a = jnp.exp(log_a)
    u = jnp.sqrt(-jnp.expm1(2.0 * log_a)) * (i * xs)
    _, hs = lax.associative_scan(_linear_recurrence_combine, (a, u), axis=1)
    return hs


def rglru_mixer(h, w_in, conv_w, conv_b, w_a, b_a, w_x, b_x, lam, w_out):
    S = h.shape[1]
    y_br, x_br = jnp.split(h @ w_in, 2, axis=-1)
    xp = jnp.pad(x_br, ((0, 0), (CONV_LEFT, CONV_W - 1 - CONV_LEFT), (0, 0)))
    xc = conv_b + xp[:, 0:S] * conv_w[0]
    for j in range(1, CONV_W):
        xc = xc + xp[:, j:j + S] * conv_w[j]
    xf = xc.astype(jnp.float32)
    h_fwd = _rglru_scan(xf, w_a[0], b_a[0], w_x[0], b_x[0], lam[0])
    h_bwd = jnp.flip(_rglru_scan(jnp.flip(xf, 1), w_a[1], b_a[1], w_x[1], b_x[1], lam[1]), 1)
    out = jax.nn.gelu(y_br.astype(jnp.float32)) * (h_fwd + h_bwd)
    return out.astype(h.dtype) @ w_out


def gqa_axial_mixer(h, w_qkv, q_gain, k_gain, w_out):
    B, S, _ = h.shape
    rows_n = S // GRID_W
    qkv = h @ w_qkv
    nq = GQA_HEADS * HEAD_DIM
    nkv = GQA_KV_HEADS * HEAD_DIM
    q = qkv[..., :nq].reshape(B, S, GQA_HEADS, HEAD_DIM)
    k = qkv[..., nq:nq + nkv].reshape(B, S, GQA_KV_HEADS, HEAD_DIM)
    v = qkv[..., nq + nkv:].reshape(B, S, GQA_KV_HEADS, HEAD_DIM)
    rows = jnp.repeat(jnp.arange(rows_n), GRID_W).astype(jnp.float32)
    cols = jnp.tile(jnp.arange(GRID_W), rows_n).astype(jnp.float32)
    q = axial_rope(rms_norm(q, q_gain), rows, cols)
    k = axial_rope(rms_norm(k, k_gain), rows, cols)
    scale = HEAD_DIM ** -0.5
    qb = q.reshape(B, S // Q_BLOCK, Q_BLOCK, GQA_KV_HEADS, GQA_GROUP, HEAD_DIM).transpose(1, 0, 2, 3, 4, 5)

    def attend_block(qi):
        s = jnp.einsum('bqngd,bsnd->bngqs', qi, k).astype(jnp.float32) * scale
        p = jax.nn.softmax(s, axis=-1).astype(v.dtype)
        return jnp.einsum('bngqs,bsnd->bqngd', p, v)

    o = lax.map(attend_block, qb)
    o = o.transpose(1, 0, 2, 3, 4, 5).reshape(B, S, GQA_HEADS * HEAD_DIM)
    return o @ w_out


def dilated_band_attention(q, k, v, dil, span):
    B, S, H, Dh = q.shape
    unit = dil * span
    Sp = -(-S // unit) * unit
    M = Sp // dil
    nb = M // span
    pad = ((0, 0), (0, Sp - S), (0, 0), (0, 0))

    def to_res(t):
        return jnp.pad(t, pad).reshape(B, M, dil, H, Dh).transpose(0, 2, 1, 3, 4).reshape(B, dil, nb, span, H, Dh)

    def band(t):
        tp = jnp.pad(t, [(0, 0), (0, 0), (1, 1)] + [(0, 0)] * (t.ndim - 3))
        return jnp.concatenate([tp[:, :, :-2], tp[:, :, 1:-1], tp[:, :, 2:]], axis=3)

    def from_res(t):
        rest = t.shape[4:]
        return t.reshape((B, dil, M) + rest).swapaxes(1, 2).reshape((B, Sp) + rest)[:, :S]

    qr = to_res(q)
    kr = band(to_res(k))
    vr = band(to_res(v)).astype(jnp.float32)
    key_ok = (jnp.arange(Sp) < S).reshape(M, dil).T.reshape(1, dil, nb, span)
    key_ok = band(key_ok)[0]
    rel = jnp.arange(3 * span)[None, :] - span - jnp.arange(span)[:, None]
    rel_ok = jnp.abs(rel) <= span
    mask = rel_ok[None, None, None] & key_ok[:, :, None, None, :]
    s = jnp.einsum('brnqhd,brnkhd->brnhqk', qr, kr).astype(jnp.float32) * (Dh ** -0.5)
    s = jnp.where(mask, s, NEG_INF)
    m = jnp.max(s, axis=-1, keepdims=True)
    p = jnp.exp(s - m)
    l = jnp.sum(p, axis=-1, keepdims=True)
    o = jnp.einsum('brnhqk,brnkhd->brnqhd', p / l, vr)
    lse = (m + jnp.log(l))[..., 0].transpose(0, 1, 2, 4, 3)
    return from_res(o), from_res(lse)


def dilated_mixer(h, w_qkv, q_gain, k_gain, w_out):
    B, S, _ = h.shape
    qkv = (h @ w_qkv).reshape(B, S, DIL_GROUPS, 3, DIL_HEADS, HEAD_DIM)
    pos = jnp.arange(S, dtype=jnp.float32)
    outs = []
    lses = []
    for g, (window, dil) in enumerate(DIL_PATTERNS):
        span = window // (2 * dil)
        q = partial_rope(rms_norm(qkv[:, :, g, 0], q_gain), pos)
        k = partial_rope(rms_norm(qkv[:, :, g, 1], k_gain), pos)
        o, lse = dilated_band_attention(q, k, qkv[:, :, g, 2], dil, span)
        outs.append(o)
        lses.append(lse)
    w = jax.nn.softmax(jnp.stack(lses, axis=0), axis=0)[..., None]
    o = jnp.sum(jnp.stack(outs, axis=0) * w, axis=0)
    return o.reshape(B, S, DIL_HEADS * HEAD_DIM).astype(h.dtype) @ w_out


def setup_inputs(seed: int = 0) -> dict:
    key = jax.random.key(seed)
    ks = iter(jax.random.split(key, 64))

    def w(shape, fan_in):
        return jax.random.normal(next(ks), shape, jnp.float32) * fan_in ** -0.5

    def gain(n):
        return 1.0 + 0.05 * jax.random.normal(next(ks), (n,), jnp.float32)

    def bias(shape, scale=0.02):
        return scale * jax.random.normal(next(ks), shape, jnp.float32)

    p = {'x': jax.random.normal(next(ks), (BATCH, SEQ, D_MODEL), jnp.float32)}

    def add_rglru(pre):
        p[pre + 'attn_norm'] = gain(D_MODEL)
        p[pre + 'rg_w_in'] = w((D_MODEL, 2 * D_RNN), D_MODEL)
        p[pre + 'rg_conv_w'] = w((CONV_W, D_RNN), CONV_W)
        p[pre + 'rg_conv_b'] = bias((D_RNN,))
        p[pre + 'rg_w_a'] = w((2, RG_BLOCKS, RG_BLOCK_W, RG_BLOCK_W), RG_BLOCK_W)
        p[pre + 'rg_b_a'] = bias((2, D_RNN))
        p[pre + 'rg_w_x'] = w((2, RG_BLOCKS, RG_BLOCK_W, RG_BLOCK_W), RG_BLOCK_W)
        p[pre + 'rg_b_x'] = bias((2, D_RNN))
        a0 = jax.random.uniform(next(ks), (2, D_RNN), jnp.float32, 0.9, 0.999)
        sa = a0 ** (1.0 / RG_C)
        p[pre + 'rg_lambda'] = jnp.log(sa) - jnp.log1p(-sa)
        p[pre + 'rg_w_out'] = w((D_RNN, D_MODEL), D_RNN)

    def add_gqa(pre):
        p[pre + 'attn_norm'] = gain(D_MODEL)
        p[pre + 'gqa_w_qkv'] = w((D_MODEL, (GQA_HEADS + 2 * GQA_KV_HEADS) * HEAD_DIM), D_MODEL)
        p[pre + 'gqa_q_norm'] = gain(HEAD_DIM)
        p[pre + 'gqa_k_norm'] = gain(HEAD_DIM)
        p[pre + 'gqa_w_out'] = w((GQA_HEADS * HEAD_DIM, D_MODEL), GQA_HEADS * HEAD_DIM)

    def add_dil(pre):
        p[pre + 'attn_norm'] = gain(D_MODEL)
        p[pre + 'dil_w_qkv'] = w((D_MODEL, DIL_GROUPS * 3 * DIL_HEADS * HEAD_DIM), D_MODEL)
        p[pre + 'dil_q_norm'] = gain(HEAD_DIM)
        p[pre + 'dil_k_norm'] = gain(HEAD_DIM)
        p[pre + 'dil_w_out'] = w((DIL_HEADS * HEAD_DIM, D_MODEL), DIL_HEADS * HEAD_DIM)

    def add_dense(pre):
        p[pre + 'ffn_norm'] = gain(D_MODEL)
        p[pre + 'ffn_w_gate_up'] = w((D_MODEL, 2 * D_FF), D_MODEL)
        p[pre + 'ffn_w_down'] = w((D_FF, D_MODEL), D_FF)

    def add_moe(pre):
        p[pre + 'ffn_norm'] = gain(D_MODEL)
        p[pre + 'moe_router'] = w((D_MODEL, N_EXPERTS), D_MODEL)
        p[pre + 'moe_router_b'] = bias((N_EXPERTS,), 0.01)
        p[pre + 'moe_w_gate_up'] = w((N_EXPERTS, D_MODEL, 2 * D_FF_EXPERT), D_MODEL)
        p[pre + 'moe_w_down'] = w((N_EXPERTS, D_FF_EXPERT, D_MODEL), D_FF_EXPERT)

    add_rglru('l0_')
    add_dense('l0_')
    add_gqa('l1_')
    add_moe('l1_')
    add_dil('l2_')
    add_dense('l2_')
    add_rglru('l3_')
    add_moe('l3_')
    return p


def reference(x, l0_attn_norm, l0_rg_w_in, l0_rg_conv_w, l0_rg_conv_b, l0_rg_w_a, l0_rg_b_a, l0_rg_w_x, l0_rg_b_x, l0_rg_lambda, l0_rg_w_out, l0_ffn_norm, l0_ffn_w_gate_up, l0_ffn_w_down, l1_attn_norm, l1_gqa_w_qkv, l1_gqa_q_norm, l1_gqa_k_norm, l1_gqa_w_out, l1_ffn_norm, l1_moe_router, l1_moe_router_b, l1_moe_w_gate_up, l1_moe_w_down, l2_attn_norm, l2_dil_w_qkv, l2_dil_q_norm, l2_dil_k_norm, l2_dil_w_out, l2_ffn_norm, l2_ffn_w_gate_up, l2_ffn_w_down, l3_attn_norm, l3_rg_w_in, l3_rg_conv_w, l3_rg_conv_b, l3_rg_w_a, l3_rg_b_a, l3_rg_w_x, l3_rg_b_x, l3_rg_lambda, l3_rg_w_out, l3_ffn_norm, l3_moe_router, l3_moe_router_b, l3_moe_w_gate_up, l3_moe_w_down):
    mixers = (rglru_mixer, gqa_axial_mixer, dilated_mixer)
    attn_norms = [l0_attn_norm, l1_attn_norm, l2_attn_norm, l3_attn_norm]
    ffn_norms = [l0_ffn_norm, l1_ffn_norm, l2_ffn_norm, l3_ffn_norm]
    mixer_params = [
        (l0_rg_w_in, l0_rg_conv_w, l0_rg_conv_b, l0_rg_w_a, l0_rg_b_a, l0_rg_w_x, l0_rg_b_x, l0_rg_lambda, l0_rg_w_out),
        (l1_gqa_w_qkv, l1_gqa_q_norm, l1_gqa_k_norm, l1_gqa_w_out),
        (l2_dil_w_qkv, l2_dil_q_norm, l2_dil_k_norm, l2_dil_w_out),
        (l3_rg_w_in, l3_rg_conv_w, l3_rg_conv_b, l3_rg_w_a, l3_rg_b_a, l3_rg_w_x, l3_rg_b_x, l3_rg_lambda, l3_rg_w_out),
    ]
    ffn_params = [
        (l0_ffn_w_gate_up, l0_ffn_w_down),
        (l1_moe_router, l1_moe_router_b, l1_moe_w_gate_up, l1_moe_w_down),
        (l2_ffn_w_gate_up, l2_ffn_w_down),
        (l3_moe_router, l3_moe_router_b, l3_moe_w_gate_up, l3_moe_w_down),
    ]
    for i in range(DEPTH):
        x = x + mixers[i % N_MIXERS](rms_norm(x, attn_norms[i]), *mixer_params[i])
        ffn = dense_ffn if i % 2 == 0 else moe_ffn
        x = x + ffn(rms_norm(x, ffn_norms[i]), *ffn_params[i])
    return x
```

```python
import functools

import jax
import jax.numpy as jnp
from jax import lax
from jax.experimental import pallas as pl
from jax.experimental.pallas import tpu as pltpu

F32 = jnp.float32
BF16 = jnp.bfloat16

EPS = 1e-6
NEG_INF = -1e30
GRID_W = 64
HEAD_DIM = 128
LANES = 128
RG_BLOCK_W = 128
CONV_W = 4
CONV_LEFT = 2
RG_C = 8.0
GQA_HEADS = 8
GQA_KV_HEADS = 2
GQA_GROUP = GQA_HEADS // GQA_KV_HEADS
AXIAL_THETA = 10000.0
DIL_PATTERNS = ((128, 1), (512, 4), (2048, 16))
DIL_HEADS = 8
ROPE_THETA = 500000.0
ROT_DIMS = HEAD_DIM // 4
N_EXPERTS = 8
VMEM_LIMIT = 56 * 1024 * 1024

NT_DIMS = (((1,), (1,)), ((), ()))


def _params(sem, vmem=VMEM_LIMIT):
    return pltpu.CompilerParams(dimension_semantics=sem, vmem_limit_bytes=vmem)


def _rms(x, g):
    return x * lax.rsqrt(jnp.mean(x * x, axis=-1, keepdims=True) + EPS) * g


def _norm_matmul_kernel(x_ref, g_ref, w_ref, o_ref, h_scr):
    @pl.when(pl.program_id(2) == 0)
    def _():
        h_scr[...] = _rms(x_ref[...], g_ref[...]).astype(BF16)

    o_ref[...] = jnp.dot(h_scr[...], w_ref[...], preferred_element_type=F32).astype(o_ref.dtype)


def norm_matmul_time_major(x2, gain, w, batch, seq, *, tm, tn, out_dtype):
    d = x2.shape[1]
    n = w.shape[1]
    ns, nj = seq // tm, n // tn
    return pl.pallas_call(
        _norm_matmul_kernel,
        grid=(batch, ns, nj),
        in_specs=[
            pl.BlockSpec((tm, d), lambda b, s, j: (b * ns + s, 0)),
            pl.BlockSpec((1, d), lambda b, s, j: (0, 0)),
            pl.BlockSpec((d, tn), lambda b, s, j: (0, j)),
        ],
        out_specs=pl.BlockSpec((tm, tn), lambda b, s, j: (s, b * nj + j)),
        out_shape=jax.ShapeDtypeStruct((seq, batch * n), out_dtype),
        scratch_shapes=[pltpu.VMEM((tm, d), BF16)],
        compiler_params=_params(("parallel", "parallel", "arbitrary")),
        name="norm_matmul_tm",
    )(x2, gain, w)


def _rope_head(a, gain, cos, sin, lo_mask, half, scale):
    y = _rms(a, gain)
    swapped = jnp.where(lo_mask, pltpu.roll(y, LANES - half, 1), pltpu.roll(y, half, 1))
    out = y * cos + swapped * sin
    if scale != 1.0:
        out = out * scale
    return out


def _qkv_kernel(modes_ref, x_ref, g_ref, w_ref, cos_ref, sin_ref, qg_ref, kg_ref, o_ref, h_scr,
                *, half, scale):
    j = pl.program_id(1)

    @pl.when(j == 0)
    def _():
        h_scr[...] = _rms(x_ref[...], g_ref[...]).astype(BF16)

    acc = jnp.dot(h_scr[...], w_ref[...], preferred_element_type=F32)
    mode = modes_ref[j]
    heads = acc.shape[1] // LANES

    @pl.when(mode == 0)
    def _():
        o_ref[...] = acc.astype(o_ref.dtype)

    def roped(gain, sc):
        cos = cos_ref[...]
        sin = sin_ref[...]
        lane = lax.broadcasted_iota(jnp.int32, cos.shape, 1)
        lo_mask = (lane % (2 * half)) < half
        for c in range(heads):
            sl = slice(c * LANES, (c + 1) * LANES)
            o_ref[:, sl] = _rope_head(acc[:, sl], gain, cos, sin, lo_mask, half, sc).astype(o_ref.dtype)

    @pl.when(mode == 1)
    def _():
        roped(qg_ref[...], scale)

    @pl.when(mode == 2)
    def _():
        roped(kg_ref[...], 1.0)


def qkv_project(x2, gain, w, modes, cos, sin, q_gain, k_gain, seq, *, tm, tn, half, scale):
    t, d = x2.shape
    n = w.shape[1]
    ns = seq // tm
    grid_spec = pltpu.PrefetchScalarGridSpec(
        num_scalar_prefetch=1,
        grid=(t // tm, n // tn),
        in_specs=[
            pl.BlockSpec((tm, d), lambda i, j, m: (i, 0)),
            pl.BlockSpec((1, d), lambda i, j, m: (0, 0)),
            pl.BlockSpec((d, tn), lambda i, j, m: (0, j)),
            pl.BlockSpec((tm, LANES), lambda i, j, m: (i % ns, 0)),
            pl.BlockSpec((tm, LANES), lambda i, j, m: (i % ns, 0)),
            pl.BlockSpec((1, LANES), lambda i, j, m: (0, 0)),
            pl.BlockSpec((1, LANES), lambda i, j, m: (0, 0)),
        ],
        out_specs=pl.BlockSpec((tm, tn), lambda i, j, m: (i, j)),
        scratch_shapes=[pltpu.VMEM((tm, d), BF16)],
    )
    return pl.pallas_call(
        functools.partial(_qkv_kernel, half=half, scale=scale),
        grid_spec=grid_spec,
        out_shape=jax.ShapeDtypeStruct((t, n), BF16),
        compiler_params=_params(("parallel", "arbitrary")),
        name="qkv_project",
    )(modes, x2, gain, w, cos, sin, q_gain, k_gain)


def _matmul_residual_kernel(a_ref, w_ref, x_ref, o_ref):
    o_ref[...] = x_ref[...] + jnp.dot(a_ref[...], w_ref[...], preferred_element_type=F32)


def matmul_residual(a, w, x2, *, tm):
    t, k = a.shape
    n = w.shape[1]
    return pl.pallas_call(
        _matmul_residual_kernel,
        grid=(t // tm,),
        in_specs=[
            pl.BlockSpec((tm, k), lambda i: (i, 0)),
            pl.BlockSpec((k, n), lambda i: (0, 0)),
            pl.BlockSpec((tm, n), lambda i: (i, 0)),
        ],
        out_specs=pl.BlockSpec((tm, n), lambda i: (i, 0)),
        out_shape=jax.ShapeDtypeStruct((t, n), F32),
        compiler_params=_params(("parallel",)),
        name="matmul_residual",
    )(a, w, x2)


def _gqa_attn_kernel(q_ref, k_ref, v_ref, o_ref, m_scr, l_scr, acc_scr, *, tk):
    tq = q_ref.shape[1]
    seq = k_ref.shape[1]
    q = jnp.concatenate([q_ref[0, :, g * LANES:(g + 1) * LANES] for g in range(GQA_GROUP)], axis=0)
    m_scr[...] = jnp.full(m_scr.shape, NEG_INF, F32)
    l_scr[...] = jnp.zeros(l_scr.shape, F32)
    acc_scr[...] = jnp.zeros(acc_scr.shape, F32)

    def body(c, carry):
        start = pl.multiple_of(c * tk, tk)
        k = k_ref[0, pl.ds(start, tk), :]
        v = v_ref[0, pl.ds(start, tk), :]
        s = lax.dot_general(q, k, NT_DIMS, preferred_element_type=F32)
        m_prev = m_scr[...]
        m_new = jnp.maximum(m_prev, jnp.max(s, axis=-1, keepdims=True))
        alpha = jnp.exp(m_prev - m_new)
        p = jnp.exp(s - m_new)
        l_scr[...] = alpha * l_scr[...] + jnp.sum(p, axis=-1, keepdims=True)
        acc_scr[...] = alpha * acc_scr[...] + jnp.dot(p.astype(BF16), v, preferred_element_type=F32)
        m_scr[...] = m_new
        return carry

    lax.fori_loop(0, seq // tk, body, 0)
    o = acc_scr[...] / l_scr[...]
    for g in range(GQA_GROUP):
        o_ref[0, :, g * LANES:(g + 1) * LANES] = o[g * tq:(g + 1) * tq].astype(o_ref.dtype)


def gqa_attention(qkv, *, tq, tk):
    b, s, _ = qkv.shape
    gw = GQA_GROUP * HEAD_DIM
    k_off = GQA_HEADS
    v_off = GQA_HEADS + GQA_KV_HEADS
    return pl.pallas_call(
        functools.partial(_gqa_attn_kernel, tk=tk),
        grid=(b, GQA_KV_HEADS, s // tq),
        in_specs=[
            pl.BlockSpec((1, tq, gw), lambda bb, n, i: (bb, i, n)),
            pl.BlockSpec((1, s, HEAD_DIM), lambda bb, n, i: (bb, 0, k_off + n)),
            pl.BlockSpec((1, s, HEAD_DIM), lambda bb, n, i: (bb, 0, v_off + n)),
        ],
        out_specs=pl.BlockSpec((1, tq, gw), lambda bb, n, i: (bb, i, n)),
        out_shape=jax.ShapeDtypeStruct((b, s, GQA_HEADS * HEAD_DIM), BF16),
        scratch_shapes=[
            pltpu.VMEM((GQA_GROUP * tq, 1), F32),
            pltpu.VMEM((GQA_GROUP * tq, 1), F32),
            pltpu.VMEM((GQA_GROUP * tq, HEAD_DIM), F32),
        ],
        compiler_params=_params(("parallel", "parallel", "arbitrary")),
        name="gqa_attention",
    )(qkv, qkv, qkv)


def _dil_attn_kernel(q_ref, kp_ref, kc_ref, kn_ref, vp_ref, vc_ref, vn_ref, o_ref, lse_ref,
                     *, span, m_total):
    i = pl.program_id(2)
    tq = q_ref.shape[1]
    row = lax.broadcasted_iota(jnp.int32, (tq, 3 * tq), 0)
    col = lax.broadcasted_iota(jnp.int32, (tq, 3 * tq), 1)
    key_idx = (i - 1) * tq + col
    ok = (jnp.abs(col - tq - row) <= span) & (key_idx >= 0) & (key_idx < m_total)
    lane = lax.broadcasted_iota(jnp.int32, (tq, LANES), 1)
    lse_tile = jnp.zeros((tq, LANES), F32)
    for h in range(DIL_HEADS):
        sl = slice(h * LANES, (h + 1) * LANES)
        q = q_ref[0, :, sl]
        k = jnp.concatenate([kp_ref[0, :, sl], kc_ref[0, :, sl], kn_ref[0, :, sl]], axis=0)
        v = jnp.concatenate([vp_ref[0, :, sl], vc_ref[0, :, sl], vn_ref[0, :, sl]], axis=0)
        s = lax.dot_general(q, k, NT_DIMS, preferred_element_type=F32)
        s = jnp.where(ok, s, NEG_INF)
        m = jnp.max(s, axis=-1, keepdims=True)
        p = jnp.exp(s - m)
        l = jnp.sum(p, axis=-1, keepdims=True)
        o = jnp.dot(p.astype(BF16), v, preferred_element_type=F32) / l
        o_ref[0, :, sl] = o.astype(o_ref.dtype)
        lse_tile = jnp.where(lane == h, m + jnp.log(l), lse_tile)
    lse_ref[0] = lse_tile


def dilated_attention(qkv, group, dil, span, *, tq):
    b, s, n = qkv.shape
    hw = DIL_HEADS * HEAD_DIM
    ncol = n // hw
    m_total = s // dil
    nb = m_total // tq
    view = qkv.reshape(b, m_total, dil * n)
    base = group * 3

    def spec(off, shift):
        def imap(bb, r, i):
            return (bb, jnp.clip(i + shift, 0, nb - 1), r * ncol + base + off)
        return pl.BlockSpec((1, tq, hw), imap)

    o, lse = pl.pallas_call(
        functools.partial(_dil_attn_kernel, span=span, m_total=m_total),
        grid=(b, dil, nb),
        in_specs=[spec(0, 0), spec(1, -1), spec(1, 0), spec(1, 1), spec(2, -1), spec(2, 0), spec(2, 1)],
        out_specs=[
            pl.BlockSpec((1, tq, hw), lambda bb, r, i: (bb, i, r)),
            pl.BlockSpec((1, tq, LANES), lambda bb, r, i: (bb, i, r)),
        ],
        out_shape=[
            jax.ShapeDtypeStruct((b, m_total, dil * hw), BF16),
            jax.ShapeDtypeStruct((b, m_total, dil * LANES), F32),
        ],
        compiler_params=_params(("parallel", "parallel", "arbitrary")),
        name=f"dilated_attention_d{dil}",
    )(view, view, view, view, view, view, view)
    return o.reshape(b * s, hw), lse.reshape(b * s, LANES)


def _dil_merge_kernel(o0_ref, o1_ref, o2_ref, l0_ref, l1_ref, l2_ref, w_ref, x_ref, out_ref):
    lses = [l0_ref[...], l1_ref[...], l2_ref[...]]
    mx = jnp.maximum(jnp.maximum(lses[0], lses[1]), lses[2])
    es = [jnp.exp(l - mx) for l in lses]
    den = es[0] + es[1] + es[2]
    ws = [e / den for e in es]
    outs = [o0_ref, o1_ref, o2_ref]
    parts = []
    for h in range(DIL_HEADS):
        sl = slice(h * LANES, (h + 1) * LANES)
        acc = None
        for g in range(3):
            term = outs[g][:, sl].astype(F32) * ws[g][:, h:h + 1]
            acc = term if acc is None else acc + term
        parts.append(acc.astype(BF16))
    merged = jnp.concatenate(parts, axis=1)
    out_ref[...] = x_ref[...] + jnp.dot(merged, w_ref[...], preferred_element_type=F32)


def dilated_merge_project(outs, lses, w_out, x2, *, tm):
    t, d = x2.shape
    hw = outs[0].shape[1]
    row = lambda width: pl.BlockSpec((tm, width), lambda i: (i, 0))
    return pl.pallas_call(
        _dil_merge_kernel,
        grid=(t // tm,),
        in_specs=[row(hw)] * 3 + [row(LANES)] * 3 + [pl.BlockSpec((hw, d), lambda i: (0, 0)), row(d)],
        out_specs=row(d),
        out_shape=jax.ShapeDtypeStruct((t, d), F32),
        compiler_params=_params(("parallel",)),
        name="dilated_merge_project",
    )(*outs, *lses, w_out, x2)


def _rglru_scan_kernel(xc_ref, xp_ref, xn_ref, cw_ref, cb_ref, wa_ref, ba_ref, wx_ref, bx_ref, lam_ref,
                       o_ref, a_scr, u_scr, h_scr, *, reverse, n_steps):
    s = pl.program_id(1)
    ts, nb, tc = xc_ref.shape
    chunk = (n_steps - 1 - s) if reverse else s

    @pl.when(s == 0)
    def _():
        h_scr[...] = jnp.zeros(h_scr.shape, F32)

    prev = jnp.where(chunk == 0, 0.0, xp_ref[...])
    nxt = jnp.where(chunk == n_steps - 1, 0.0, xn_ref[...])
    xe = jnp.concatenate([prev, xc_ref[...], nxt], axis=0)
    cw = cw_ref[...]
    xc = cb_ref[...].reshape(1, 1, tc) + xe[0:ts] * cw[0:1].reshape(1, 1, tc)
    for j in range(1, CONV_W):
        xc = xc + xe[j:j + ts] * cw[j:j + 1].reshape(1, 1, tc)
    xc2 = xc.reshape(ts * nb, tc)
    lam = lam_ref[...]
    softplus_neg_lam = jnp.maximum(-lam, 0.0) + jnp.log1p(jnp.exp(-jnp.abs(lam)))
    for kb in range(tc // RG_BLOCK_W):
        sl = slice(kb * RG_BLOCK_W, (kb + 1) * RG_BLOCK_W)
        xb = xc2[:, sl]
        xb16 = xb.astype(BF16)
        r = jax.nn.sigmoid(jnp.dot(xb16, wa_ref[kb], preferred_element_type=F32) + ba_ref[:, sl])
        gate = jax.nn.sigmoid(jnp.dot(xb16, wx_ref[kb], preferred_element_type=F32) + bx_ref[:, sl])
        log_a = (-RG_C) * r * softplus_neg_lam[:, sl]
        a = jnp.exp(log_a)
        u = jnp.sqrt(1.0 - a * a) * (gate * xb)
        a_scr[:, :, sl] = a.reshape(ts, nb, RG_BLOCK_W)
        u_scr[:, :, sl] = u.reshape(ts, nb, RG_BLOCK_W)

    def step(t, h):
        tt = (ts - 1 - t) if reverse else t
        h = a_scr[tt] * h + u_scr[tt]
        o_ref[tt] = h
        return h

    h_scr[...] = lax.fori_loop(0, ts, step, h_scr[...], unroll=8)


def rglru_scan(yx3, conv_w, conv_b, w_a, b_a, w_x, b_x, lam, *, reverse, ts, tc):
    seq, nb, two_c = yx3.shape
    c = two_c // 2
    nc, n_steps = c // tc, seq // ts
    kb = tc // RG_BLOCK_W

    def chunk_of(s):
        return (n_steps - 1 - s) if reverse else s

    vec = lambda rows: pl.BlockSpec((rows, tc), lambda ci, s: (0, ci))
    mat = pl.BlockSpec((kb, RG_BLOCK_W, RG_BLOCK_W), lambda ci, s: (ci, 0, 0))
    return pl.pallas_call(
        functools.partial(_rglru_scan_kernel, reverse=reverse, n_steps=n_steps),
        grid=(nc, n_steps),
        in_specs=[
            pl.BlockSpec((ts, nb, tc), lambda ci, s: (chunk_of(s), 0, nc + ci)),
            pl.BlockSpec((CONV_LEFT, nb, tc),
                         lambda ci, s: (jnp.maximum(chunk_of(s) * (ts // CONV_LEFT) - 1, 0), 0, nc + ci)),
            pl.BlockSpec((1, nb, tc),
                         lambda ci, s: (jnp.minimum((chunk_of(s) + 1) * ts, seq - 1), 0, nc + ci)),
            vec(CONV_W), vec(1), mat, vec(1), mat, vec(1), vec(1),
        ],
        out_specs=pl.BlockSpec((ts, nb, tc), lambda ci, s: (chunk_of(s), 0, ci)),
        out_shape=jax.ShapeDtypeStruct((seq, nb, c), F32),
        scratch_shapes=[
            pltpu.VMEM((ts, nb, tc), F32),
            pltpu.VMEM((ts, nb, tc), F32),
            pltpu.VMEM((nb, tc), F32),
        ],
        compiler_params=_params(("parallel", "arbitrary")),
        name="rglru_scan_bwd" if reverse else "rglru_scan_fwd",
    )(yx3, yx3, yx3, conv_w, conv_b, w_a, b_a, w_x, b_x, lam)


def _gelu_tanh(y):
    return y * (0.5 * (1.0 + jnp.tanh(0.7978845608028654 * (y + 0.044715 * (y * y * y)))))


def _rglru_out_kernel(y_ref, hf_ref, hb_ref, w_ref, x_ref, o_ref):
    gated = _gelu_tanh(y_ref[...]) * (hf_ref[...] + hb_ref[...])
    o_ref[...] = x_ref[...] + jnp.dot(gated.astype(BF16), w_ref[...], preferred_element_type=F32)


def rglru_out_project(yx, hf, hb, w_out, x2, batch, seq, *, tm):
    c, d = w_out.shape
    ns = seq // tm
    return pl.pallas_call(
        _rglru_out_kernel,
        grid=(batch, ns),
        in_specs=[
            pl.BlockSpec((tm, c), lambda b, s: (s, 2 * b)),
            pl.BlockSpec((tm, c), lambda b, s: (s, b)),
            pl.BlockSpec((tm, c), lambda b, s: (s, b)),
            pl.BlockSpec((c, d), lambda b, s: (0, 0)),
            pl.BlockSpec((tm, d), lambda b, s: (b * ns + s, 0)),
        ],
        out_specs=pl.BlockSpec((tm, d), lambda b, s: (b * ns + s, 0)),
        out_shape=jax.ShapeDtypeStruct(x2.shape, F32),
        compiler_params=_params(("parallel", "parallel")),
        name="rglru_out_project",
    )(yx, hf, hb, w_out, x2)


def _ffn_kernel(x_ref, g_ref, wg_ref, wu_ref, wd_ref, c_ref, o_ref, h_scr, acc_scr, *, weighted):
    e = pl.program_id(1)
    f = pl.program_id(2)

    @pl.when((e == 0) & (f == 0))
    def _():
        h_scr[...] = _rms(x_ref[...], g_ref[...]).astype(BF16)
        acc_scr[...] = jnp.zeros(acc_scr.shape, F32)

    h = h_scr[...]
    gate = jnp.dot(h, wg_ref[...], preferred_element_type=F32)
    up = jnp.dot(h, wu_ref[...], preferred_element_type=F32)
    act = gate * jax.nn.sigmoid(gate) * up
    if weighted:
        c = c_ref[...]
        lane = lax.broadcasted_iota(jnp.int32, c.shape, 1)
        act = act * jnp.sum(jnp.where(lane == e, c, 0.0), axis=-1, keepdims=True)
    acc_scr[...] += jnp.dot(act.astype(BF16), wd_ref[...], preferred_element_type=F32)

    @pl.when((e == pl.num_programs(1) - 1) & (f == pl.num_programs(2) - 1))
    def _():
        o_ref[...] = x_ref[...] + acc_scr[...]


def swiglu_ffn(x2, gain, w_gate_up, w_down, combine, *, tm, tf):
    t, d = x2.shape
    n_e, ff, _ = w_down.shape
    nf = ff // tf
    weighted = combine is not None
    if combine is None:
        combine = jnp.ones((8, LANES), F32)
        c_spec = pl.BlockSpec((8, LANES), lambda i, e, f: (0, 0))
    else:
        c_spec = pl.BlockSpec((tm, LANES), lambda i, e, f: (i, 0))
    return pl.pallas_call(
        functools.partial(_ffn_kernel, weighted=weighted),
        grid=(t // tm, n_e, nf),
        in_specs=[
            pl.BlockSpec((tm, d), lambda i, e, f: (i, 0)),
            pl.BlockSpec((1, d), lambda i, e, f: (0, 0)),
            pl.BlockSpec((None, d, tf), lambda i, e, f: (e, 0, f)),
            pl.BlockSpec((None, d, tf), lambda i, e, f: (e, 0, nf + f)),
            pl.BlockSpec((None, tf, d), lambda i, e, f: (e, f, 0)),
            c_spec,
        ],
        out_specs=pl.BlockSpec((tm, d), lambda i, e, f: (i, 0)),
        out_shape=jax.ShapeDtypeStruct((t, d), F32),
        scratch_shapes=[pltpu.VMEM((tm, d), BF16), pltpu.VMEM((tm, d), F32)],
        compiler_params=_params(("parallel", "arbitrary", "arbitrary")),
        name="moe_ffn" if weighted else "dense_ffn",
    )(x2, gain, w_gate_up, w_gate_up, w_down, combine)


def _router_kernel(x_ref, g_ref, r_ref, b_ref, c_ref):
    h = _rms(x_ref[...], g_ref[...])
    logits = jnp.dot(h, r_ref[...], preferred_element_type=F32, precision=lax.Precision.HIGHEST)
    logits = logits + b_ref[...]
    lane = lax.broadcasted_iota(jnp.int32, logits.shape, 1)
    logits = jnp.where(lane < N_EXPERTS, logits, NEG_INF)
    v1 = jnp.max(logits, axis=-1, keepdims=True)
    i1 = jnp.min(jnp.where(logits == v1, lane, LANES), axis=-1, keepdims=True)
    rest = jnp.where(lane == i1, NEG_INF, logits)
    v2 = jnp.max(rest, axis=-1, keepdims=True)
    i2 = jnp.min(jnp.where(rest == v2, lane, LANES), axis=-1, keepdims=True)
    e2 = jnp.exp(v2 - v1)
    g1 = 1.0 / (1.0 + e2)
    g2 = e2 / (1.0 + e2)
    c_ref[...] = jnp.where(lane == i1, g1, 0.0) + jnp.where(lane == i2, g2, 0.0)


def moe_router(x2, gain, router, router_b, *, tm):
    t, d = x2.shape
    r_pad = jnp.zeros((d, LANES), F32).at[:, :N_EXPERTS].set(router)
    b_pad = jnp.zeros((1, LANES), F32).at[0, :N_EXPERTS].set(router_b)
    return pl.pallas_call(
        _router_kernel,
        grid=(t // tm,),
        in_specs=[
            pl.BlockSpec((tm, d), lambda i: (i, 0)),
            pl.BlockSpec((1, d), lambda i: (0, 0)),
            pl.BlockSpec((d, LANES), lambda i: (0, 0)),
            pl.BlockSpec((1, LANES), lambda i: (0, 0)),
        ],
        out_specs=pl.BlockSpec((tm, LANES), lambda i: (i, 0)),
        out_shape=jax.ShapeDtypeStruct((t, LANES), F32),
        compiler_params=_params(("parallel",)),
        name="moe_router",
    )(x2, gain, r_pad, b_pad)


def _inv_freq(dim, theta):
    return theta ** (-jnp.arange(0, dim, 2, dtype=F32) / dim)


def _axial_tables(seq):
    half = HEAD_DIM // 2
    t = jnp.arange(seq)
    rows = (t // GRID_W).astype(F32)
    cols = (t % GRID_W).astype(F32)
    freq = _inv_freq(half, AXIAL_THETA)
    ang_r = rows[:, None] * freq[None, :]
    ang_c = cols[:, None] * freq[None, :]
    ang = jnp.concatenate([ang_r, ang_r, ang_c, ang_c], axis=-1)
    sign = jnp.tile(jnp.concatenate([-jnp.ones(half // 2, F32), jnp.ones(half // 2, F32)]), 2)
    return jnp.cos(ang), jnp.sin(ang) * sign[None, :]


def _partial_tables(seq):
    pos = jnp.arange(seq, dtype=F32)
    freq = _inv_freq(ROT_DIMS, ROPE_THETA)
    ang = pos[:, None] * freq[None, :]
    cos = jnp.concatenate([jnp.cos(ang), jnp.cos(ang), jnp.ones((seq, HEAD_DIM - ROT_DIMS), F32)], axis=-1)
    sin = jnp.concatenate([-jnp.sin(ang), jnp.sin(ang), jnp.zeros((seq, HEAD_DIM - ROT_DIMS), F32)], axis=-1)
    return cos, sin


def _row(v):
    return v.reshape(1, -1).astype(F32)


def _rglru_layer(x2, batch, seq, norm, w_in, conv_w, conv_b, w_a, b_a, w_x, b_x, lam, w_out):
    c = w_out.shape[0]
    yx = norm_matmul_time_major(x2, _row(norm), w_in.astype(BF16), batch, seq,
                                tm=512, tn=1024, out_dtype=F32)
    yx3 = yx.reshape(seq, batch, 2 * c)
    hs = []
    for direction in range(2):
        hs.append(rglru_scan(
            yx3, conv_w, _row(conv_b), w_a[direction].astype(BF16), _row(b_a[direction]),
            w_x[direction].astype(BF16), _row(b_x[direction]), _row(lam[direction]),
            reverse=direction == 1, ts=256, tc=512).reshape(seq, batch * c))
    return rglru_out_project(yx, hs[0], hs[1], w_out.astype(BF16), x2, batch, seq, tm=512)


def _gqa_layer(x2, batch, seq, norm, w_qkv, q_gain, k_gain, w_out):
    nq = GQA_HEADS * HEAD_DIM
    nkv = GQA_KV_HEADS * HEAD_DIM
    tn = nkv
    modes = jnp.array([1] * (nq // tn) + [2] * (nkv // tn) + [0] * (nkv // tn), jnp.int32)
    cos, sin = _axial_tables(seq)
    qkv = qkv_project(x2, _row(norm), w_qkv.astype(BF16), modes, cos, sin, _row(q_gain), _row(k_gain), seq,
                      tm=512, tn=tn, half=HEAD_DIM // 4, scale=HEAD_DIM ** -0.5)
    qkv = qkv.reshape(batch, seq, nq + 2 * nkv)
    o = gqa_attention(qkv, tq=256, tk=512)
    return matmul_residual(o.reshape(batch * seq, nq), w_out.astype(BF16), x2, tm=512)


def _dilated_layer(x2, batch, seq, norm, w_qkv, q_gain, k_gain, w_out):
    hw = DIL_HEADS * HEAD_DIM
    n = w_qkv.shape[1]
    modes = jnp.array([1, 2, 0] * (n // (3 * hw)), jnp.int32)
    cos, sin = _partial_tables(seq)
    qkv = qkv_project(x2, _row(norm), w_qkv.astype(BF16), modes, cos, sin, _row(q_gain), _row(k_gain), seq,
                      tm=512, tn=hw, half=ROT_DIMS // 2, scale=HEAD_DIM ** -0.5)
    qkv = qkv.reshape(batch, seq, n)
    outs, lses = [], []
    for g, (window, dil) in enumerate(DIL_PATTERNS):
        span = window // (2 * dil)
        assert seq % (dil * span) == 0
        o, lse = dilated_attention(qkv, g, dil, span, tq=128)
        outs.append(o)
        lses.append(lse)
    return dilated_merge_project(outs, lses, w_out.astype(BF16), x2, tm=512)


def _dense_ffn_layer(x2, norm, w_gate_up, w_down):
    return swiglu_ffn(x2, _row(norm), w_gate_up.astype(BF16)[None], w_down.astype(BF16)[None], None,
                      tm=1024, tf=512)


def _moe_layer(x2, norm, router, router_b, w_gate_up, w_down):
    combine = moe_router(x2, _row(norm), router, router_b, tm=512)
    return swiglu_ffn(x2, _row(norm), w_gate_up.astype(BF16), w_down.astype(BF16), combine, tm=1024, tf=512)


def kernel(x, l0_attn_norm, l0_rg_w_in, l0_rg_conv_w, l0_rg_conv_b, l0_rg_w_a, l0_rg_b_a, l0_rg_w_x, l0_rg_b_x, l0_rg_lambda, l0_rg_w_out, l0_ffn_norm, l0_ffn_w_gate_up, l0_ffn_w_down, l1_attn_norm, l1_gqa_w_qkv, l1_gqa_q_norm, l1_gqa_k_norm, l1_gqa_w_out, l1_ffn_norm, l1_moe_router, l1_moe_router_b, l1_moe_w_gate_up, l1_moe_w_down, l2_attn_norm, l2_dil_w_qkv, l2_dil_q_norm, l2_dil_k_norm, l2_dil_w_out, l2_ffn_norm, l2_ffn_w_gate_up, l2_ffn_w_down, l3_attn_norm, l3_rg_w_in, l3_rg_conv_w, l3_rg_conv_b, l3_rg_w_a, l3_rg_b_a, l3_rg_w_x, l3_rg_b_x, l3_rg_lambda, l3_rg_w_out, l3_ffn_norm, l3_moe_router, l3_moe_router_b, l3_moe_w_gate_up, l3_moe_w_down):
    batch, seq, d = x.shape
    x2 = x.reshape(batch * seq, d)
    x2 = _rglru_layer(x2, batch, seq, l0_attn_norm, l0_rg_w_in, l0_rg_conv_w, l0_rg_conv_b, l0_rg_w_a,
                      l0_rg_b_a, l0_rg_w_x, l0_rg_b_x, l0_rg_lambda, l0_rg_w_out)
    x2 = _dense_ffn_layer(x2, l0_ffn_norm, l0_ffn_w_gate_up, l0_ffn_w_down)
    x2 = _gqa_layer(x2, batch, seq, l1_attn_norm, l1_gqa_w_qkv, l1_gqa_q_norm, l1_gqa_k_norm, l1_gqa_w_out)
    x2 = _moe_layer(x2, l1_ffn_norm, l1_moe_router, l1_moe_router_b, l1_moe_w_gate_up, l1_moe_w_down)
    x2 = _dilated_layer(x2, batch, seq, l2_attn_norm, l2_dil_w_qkv, l2_dil_q_norm, l2_dil_k_norm, l2_dil_w_out)
    x2 = _dense_ffn_layer(x2, l2_ffn_norm, l2_ffn_w_gate_up, l2_ffn_w_down)
    x2 = _rglru_layer(x2, batch, seq, l3_attn_norm, l3_rg_w_in, l3_rg_conv_w, l3_rg_conv_b, l3_rg_w_a,
                      l3_rg_b_a, l3_rg_w_x, l3_rg_b_x, l3_rg_lambda, l3_rg_w_out)
    x2 = _moe_layer(x2, l3_ffn_norm, l3_moe_router, l3_moe_router_b, l3_moe_w_gate_up, l3_moe_w_down)
    return x2.reshape(batch, seq, d)
```

```python
import functools

import jax
import jax.numpy as jnp
from jax import lax
from jax.experimental import pallas as pl
from jax.experimental.pallas import tpu as pltpu

F32 = jnp.float32
BF16 = jnp.bfloat16

EPS = 1e-6
NEG_INF = -1e30
GRID_W = 64
HEAD_DIM = 128
LANES = 128
RG_BLOCK_W = 128
CONV_W = 4
CONV_LEFT = 2
RG_C = 8.0
GQA_HEADS = 8
GQA_KV_HEADS = 2
GQA_GROUP = GQA_HEADS // GQA_KV_HEADS
AXIAL_THETA = 10000.0
DIL_PATTERNS = ((128, 1), (512, 4), (2048, 16))
DIL_HEADS = 8
ROPE_THETA = 500000.0
ROT_DIMS = HEAD_DIM // 4
N_EXPERTS = 8
VMEM_LIMIT = 56 * 1024 * 1024

NT_DIMS = (((1,), (1,)), ((), ()))


def _params(sem, vmem=VMEM_LIMIT):
    return pltpu.CompilerParams(dimension_semantics=sem, vmem_limit_bytes=vmem)


def _rms(x, g):
    return x * lax.rsqrt(jnp.mean(x * x, axis=-1, keepdims=True) + EPS) * g


def _norm_matmul_kernel(x_ref, g_ref, w_ref, o_ref, h_scr):
    @pl.when(pl.program_id(2) == 0)
    def _():
        h_scr[...] = _rms(x_ref[...], g_ref[...]).astype(BF16)

    o_ref[...] = jnp.dot(h_scr[...], w_ref[...], preferred_element_type=F32).astype(o_ref.dtype)


def norm_matmul_time_major(x2, gain, w, batch, seq, *, tm, tn, out_dtype):
    d = x2.shape[1]
    n = w.shape[1]
    ns, nj = seq // tm, n // tn
    return pl.pallas_call(
        _norm_matmul_kernel,
        grid=(batch, ns, nj),
        in_specs=[
            pl.BlockSpec((tm, d), lambda b, s, j: (b * ns + s, 0)),
            pl.BlockSpec((1, d), lambda b, s, j: (0, 0)),
            pl.BlockSpec((d, tn), lambda b, s, j: (0, j)),
        ],
        out_specs=pl.BlockSpec((tm, tn), lambda b, s, j: (s, b * nj + j)),
        out_shape=jax.ShapeDtypeStruct((seq, batch * n), out_dtype),
        scratch_shapes=[pltpu.VMEM((tm, d), BF16)],
        compiler_params=_params(("parallel", "parallel", "arbitrary")),
        name="norm_matmul_tm",
    )(x2, gain, w)


def _rope_head(a, gain, cos, sin, lo_mask, half, scale):
    y = _rms(a, gain)
    swapped = jnp.where(lo_mask, pltpu.roll(y, LANES - half, 1), pltpu.roll(y, half, 1))
    out = y * cos + swapped * sin
    if scale != 1.0:
        out = out * scale
    return out


def _qkv_kernel(modes_ref, x_ref, g_ref, w_ref, cos_ref, sin_ref, qg_ref, kg_ref, o_ref, h_scr, *stage,
                half, scale, dil):
    j = pl.program_id(1)
    tm = x_ref.shape[0]
    per = tm // dil

    @pl.when(j == 0)
    def _():
        hn = _rms(x_ref[...], g_ref[...])
        if dil == 1:
            h_scr[...] = hn.astype(BF16)
        else:
            stage_scr, = stage
            for c in range(hn.shape[1] // LANES):
                sl = slice(c * LANES, (c + 1) * LANES)
                stage_scr[c] = hn[:, sl]
                for r in range(dil):
                    h_scr[r * per:(r + 1) * per, sl] = stage_scr[c, pl.ds(r, per, stride=dil), :].astype(BF16)

    mode = modes_ref[j]
    heads = w_ref.shape[1] // LANES

    def emit(fn):
        h = h_scr[...]
        for c in range(heads):
            sl = slice(c * LANES, (c + 1) * LANES)
            a = jnp.dot(h, w_ref[:, sl], preferred_element_type=F32)
            o_ref[:, :, sl] = fn(a).reshape(dil, per, LANES).astype(o_ref.dtype)

    def roped(gain, sc):
        cos = cos_ref[...].reshape(tm, LANES)
        sin = sin_ref[...].reshape(tm, LANES)
        lane = lax.broadcasted_iota(jnp.int32, cos.shape, 1)
        lo_mask = (lane % (2 * half)) < half
        emit(lambda a: _rope_head(a, gain, cos, sin, lo_mask, half, sc))

    @pl.when(mode == 0)
    def _():
        emit(lambda a: a)

    @pl.when(mode == 1)
    def _():
        roped(qg_ref[...], scale)

    @pl.when(mode == 2)
    def _():
        roped(kg_ref[...], 1.0)


def _residue_major(table, dil, n_tiles, per):
    s, w = table.shape
    return table.reshape(s // dil, dil, w).transpose(1, 0, 2).reshape(dil, n_tiles, per, w)


def qkv_project(x2, gain, w, modes, cos, sin, q_gain, k_gain, batch, seq, *, tm, tn, half, scale, dil):
    t, d = x2.shape
    n = w.shape[1]
    ns = seq // tm
    per = tm // dil
    cos_r = _residue_major(cos, dil, ns, per)
    sin_r = _residue_major(sin, dil, ns, per)
    table_spec = pl.BlockSpec((dil, None, per, LANES), lambda i, j, m: (0, i % ns, 0, 0))
    grid_spec = pltpu.PrefetchScalarGridSpec(
        num_scalar_prefetch=1,
        grid=(t // tm, n // tn),
        in_specs=[
            pl.BlockSpec((tm, d), lambda i, j, m: (i, 0)),
            pl.BlockSpec((1, d), lambda i, j, m: (0, 0)),
            pl.BlockSpec((d, tn), lambda i, j, m: (0, j)),
            table_spec,
            table_spec,
            pl.BlockSpec((1, LANES), lambda i, j, m: (0, 0)),
            pl.BlockSpec((1, LANES), lambda i, j, m: (0, 0)),
        ],
        out_specs=pl.BlockSpec((None, dil, None, per, tn), lambda i, j, m: (i // ns, 0, i % ns, 0, j)),
        scratch_shapes=[pltpu.VMEM((tm, d), BF16)]
        + ([pltpu.VMEM((d // LANES, tm, LANES), F32)] if dil > 1 else []),
    )
    out = pl.pallas_call(
        functools.partial(_qkv_kernel, half=half, scale=scale, dil=dil),
        grid_spec=grid_spec,
        out_shape=jax.ShapeDtypeStruct((batch, dil, ns, per, n), BF16),
        compiler_params=_params(("parallel", "arbitrary")),
        name=f"qkv_project_d{dil}",
    )(modes, x2, gain, w, cos_r, sin_r, q_gain, k_gain)
    return out.reshape(batch, dil, seq // dil, n)


def _matmul_residual_kernel(a_ref, w_ref, x_ref, o_ref):
    o_ref[...] = x_ref[...] + jnp.dot(a_ref[...], w_ref[...], preferred_element_type=F32)


def matmul_residual(a, w, x2, *, tm):
    t, k = a.shape
    n = w.shape[1]
    return pl.pallas_call(
        _matmul_residual_kernel,
        grid=(t // tm,),
        in_specs=[
            pl.BlockSpec((tm, k), lambda i: (i, 0)),
            pl.BlockSpec((k, n), lambda i: (0, 0)),
            pl.BlockSpec((tm, n), lambda i: (i, 0)),
        ],
        out_specs=pl.BlockSpec((tm, n), lambda i: (i, 0)),
        out_shape=jax.ShapeDtypeStruct((t, n), F32),
        compiler_params=_params(("parallel",)),
        name="matmul_residual",
    )(a, w, x2)


def _gqa_attn_kernel(q_ref, k_ref, v_ref, o_ref, *, tk, streams):
    tq = q_ref.shape[1]
    seq = k_ref.shape[1]
    per = GQA_GROUP // streams
    qs = [jnp.concatenate([q_ref[0, :, g * LANES:(g + 1) * LANES] for g in range(st * per, (st + 1) * per)],
                          axis=0) for st in range(streams)]
    rows = per * tq
    ones = jnp.ones((tk, LANES), BF16)

    def body(c, carry):
        start = pl.multiple_of(c * tk, tk)
        k = k_ref[0, pl.ds(start, tk), :]
        v1 = jnp.concatenate([v_ref[0, pl.ds(start, tk), :], ones], axis=1)
        new = []
        for st in range(streams):
            m_prev, acc = carry[st]
            s = lax.dot_general(qs[st], k, NT_DIMS, preferred_element_type=F32)
            m_new = jnp.maximum(m_prev, jnp.max(s, axis=-1, keepdims=True))
            alpha = jnp.exp(m_prev - m_new)
            p = jnp.exp(s - m_new).astype(BF16)
            acc = alpha * acc + jnp.dot(p, v1, preferred_element_type=F32)
            new.append((m_new, acc))
        return tuple(new)

    init = tuple((jnp.full((rows, 1), NEG_INF, F32), jnp.zeros((rows, 2 * LANES), F32)) for _ in range(streams))
    final = lax.fori_loop(0, seq // tk, body, init)
    for st in range(streams):
        acc = final[st][1]
        o = acc[:, :LANES] / acc[:, LANES:]
        for gi in range(per):
            g = st * per + gi
            o_ref[0, :, g * LANES:(g + 1) * LANES] = o[gi * tq:(gi + 1) * tq].astype(o_ref.dtype)


def gqa_attention(qkv, *, tq, tk, streams):
    b, s, _ = qkv.shape
    gw = GQA_GROUP * HEAD_DIM
    k_off = GQA_HEADS
    v_off = GQA_HEADS + GQA_KV_HEADS
    return pl.pallas_call(
        functools.partial(_gqa_attn_kernel, tk=tk, streams=streams),
        grid=(b, GQA_KV_HEADS, s // tq),
        in_specs=[
            pl.BlockSpec((1, tq, gw), lambda bb, n, i: (bb, i, n)),
            pl.BlockSpec((1, s, HEAD_DIM), lambda bb, n, i: (bb, 0, k_off + n)),
            pl.BlockSpec((1, s, HEAD_DIM), lambda bb, n, i: (bb, 0, v_off + n)),
        ],
        out_specs=pl.BlockSpec((1, tq, gw), lambda bb, n, i: (bb, i, n)),
        out_shape=jax.ShapeDtypeStruct((b, s, GQA_HEADS * HEAD_DIM), BF16),
        compiler_params=_params(("parallel", "parallel", "arbitrary")),
        name="gqa_attention",
    )(qkv, qkv, qkv)


def _dil_attn_kernel(q_ref, kp_ref, kc_ref, kn_ref, vp_ref, vc_ref, vn_ref, o_ref, lse_ref,
                     *, span, m_total):
    i = pl.program_id(2)
    tq = q_ref.shape[0]
    nk = tq + 2 * span
    row = lax.broadcasted_iota(jnp.int32, (tq, nk), 0)
    col = lax.broadcasted_iota(jnp.int32, (tq, nk), 1)
    key_idx = i * tq - span + col
    ok = (jnp.abs(col - span - row) <= span) & (key_idx >= 0) & (key_idx < m_total)
    lane = lax.broadcasted_iota(jnp.int32, (tq, LANES), 1)
    lse_tile = jnp.zeros((tq, LANES), F32)

    def window(p_ref, c_ref, n_ref, sl):
        return jnp.concatenate([p_ref[tq - span:, sl], c_ref[:, sl], n_ref[:span, sl]], axis=0)

    for h in range(DIL_HEADS):
        sl = slice(h * LANES, (h + 1) * LANES)
        k = window(kp_ref, kc_ref, kn_ref, sl)
        v = window(vp_ref, vc_ref, vn_ref, sl)
        s = lax.dot_general(q_ref[:, sl], k, NT_DIMS, preferred_element_type=F32)
        s = jnp.where(ok, s, NEG_INF)
        m = jnp.max(s, axis=-1, keepdims=True)
        p = jnp.exp(s - m)
        l = jnp.sum(p, axis=-1, keepdims=True)
        o = jnp.dot(p.astype(BF16), v, preferred_element_type=F32) / l
        o_ref[:, sl] = o.astype(o_ref.dtype)
        lse_tile = jnp.where(lane == h, m + jnp.log(l), lse_tile)
    lse_ref[...] = lse_tile


def dilated_attention(qkv, span, *, tq):
    b, dil, m_total, n = qkv.shape
    hw = DIL_HEADS * HEAD_DIM
    nb = m_total // tq

    def spec(col, shift):
        return pl.BlockSpec((None, None, tq, hw),
                            lambda bb, r, i: (bb, r, jnp.clip(i + shift, 0, nb - 1), col))

    return pl.pallas_call(
        functools.partial(_dil_attn_kernel, span=span, m_total=m_total),
        grid=(b, dil, nb),
        in_specs=[spec(0, 0), spec(1, -1), spec(1, 0), spec(1, 1), spec(2, -1), spec(2, 0), spec(2, 1)],
        out_specs=[
            pl.BlockSpec((None, None, tq, hw), lambda bb, r, i: (bb, r, i, 0)),
            pl.BlockSpec((None, None, tq, LANES), lambda bb, r, i: (bb, r, i, 0)),
        ],
        out_shape=[
            jax.ShapeDtypeStruct((b, dil, m_total, hw), BF16),
            jax.ShapeDtypeStruct((b, dil, m_total, LANES), F32),
        ],
        compiler_params=_params(("parallel", "parallel", "arbitrary")),
        name=f"dilated_attention_d{dil}",
    )(qkv, qkv, qkv, qkv, qkv, qkv, qkv)


def _dil_merge_kernel(o0_ref, o1_ref, o2_ref, l0_ref, l1_ref, l2_ref, w_ref, x_ref, out_ref, o_scr, l_scr):
    def to_tokens(ref, scr, g):
        dil, per, width = ref.shape
        for c in range(width // LANES):
            sl = slice(c * LANES, (c + 1) * LANES)
            if dil == 1:
                scr[g, c] = ref[0, :, sl].astype(F32)
            else:
                for r in range(dil):
                    scr[g, c, pl.ds(r, per, stride=dil), :] = ref[r, :, sl].astype(F32)

    for g, (o_ref, l_ref) in enumerate(((o0_ref, l0_ref), (o1_ref, l1_ref), (o2_ref, l2_ref))):
        to_tokens(o_ref, o_scr, g)
        to_tokens(l_ref, l_scr, g)
    lses = [l_scr[g, 0] for g in range(3)]
    mx = jnp.maximum(jnp.maximum(lses[0], lses[1]), lses[2])
    es = [jnp.exp(l - mx) for l in lses]
    den = es[0] + es[1] + es[2]
    ws = [e / den for e in es]
    parts = []
    for h in range(DIL_HEADS):
        acc = None
        for g in range(3):
            term = o_scr[g, h] * ws[g][:, h:h + 1]
            acc = term if acc is None else acc + term
        parts.append(acc.astype(BF16))
    merged = jnp.concatenate(parts, axis=1)
    out_ref[...] = x_ref[...] + jnp.dot(merged, w_ref[...], preferred_element_type=F32)


def dilated_merge_project(outs, lses, w_out, x2, batch, seq, *, tm):
    t, d = x2.shape
    hw = w_out.shape[0]
    ns = seq // tm

    def grouped(arr):
        _, dil, m_total, w = arr.shape
        per = tm // dil
        view = arr.reshape(batch, dil, ns, per, w)
        return view, pl.BlockSpec((None, dil, None, per, w), lambda i: (i // ns, 0, i % ns, 0, 0))

    views, specs = zip(*[grouped(a) for a in (*outs, *lses)])
    row = pl.BlockSpec((tm, d), lambda i: (i, 0))
    return pl.pallas_call(
        _dil_merge_kernel,
        grid=(t // tm,),
        in_specs=list(specs) + [pl.BlockSpec((hw, d), lambda i: (0, 0)), row],
        out_specs=row,
        out_shape=jax.ShapeDtypeStruct((t, d), F32),
        scratch_shapes=[pltpu.VMEM((3, hw // LANES, tm, LANES), F32), pltpu.VMEM((3, 1, tm, LANES), F32)],
        compiler_params=_params(("parallel",)),
        name="dilated_merge_project",
    )(*views, w_out, x2)


def _rglru_scan_kernel(xc_ref, xp_ref, xn_ref, cw_ref, cb_ref, wa_ref, ba_ref, wx_ref, bx_ref, lam_ref,
                       o_ref, a_scr, u_scr, h_scr, *, reverse, n_steps):
    s = pl.program_id(1)
    ts, nb, tc = xc_ref.shape
    chunk = (n_steps - 1 - s) if reverse else s

    @pl.when(s == 0)
    def _():
        h_scr[...] = jnp.zeros(h_scr.shape, F32)

    prev = jnp.where(chunk == 0, 0.0, xp_ref[...])
    nxt = jnp.where(chunk == n_steps - 1, 0.0, xn_ref[...])
    xe = jnp.concatenate([prev, xc_ref[...], nxt], axis=0)
    cw = cw_ref[...]
    xc = cb_ref[...].reshape(1, 1, tc) + xe[0:ts] * cw[0:1].reshape(1, 1, tc)
    for j in range(1, CONV_W):
        xc = xc + xe[j:j + ts] * cw[j:j + 1].reshape(1, 1, tc)
    xc2 = xc.reshape(ts * nb, tc)
    lam = lam_ref[...]
    softplus_neg_lam = jnp.maximum(-lam, 0.0) + jnp.log1p(jnp.exp(-jnp.abs(lam)))
    for kb in range(tc // RG_BLOCK_W):
        sl = slice(kb * RG_BLOCK_W, (kb + 1) * RG_BLOCK_W)
        xb = xc2[:, sl]
        xb16 = xb.astype(BF16)
        r = jax.nn.sigmoid(jnp.dot(xb16, wa_ref[kb], preferred_element_type=F32) + ba_ref[:, sl])
        gate = jax.nn.sigmoid(jnp.dot(xb16, wx_ref[kb], preferred_element_type=F32) + bx_ref[:, sl])
        log_a = (-RG_C) * r * softplus_neg_lam[:, sl]
        a = jnp.exp(log_a)
        u = jnp.sqrt(1.0 - a * a) * (gate * xb)
        a_scr[:, :, sl] = a.reshape(ts, nb, RG_BLOCK_W)
        u_scr[:, :, sl] = u.reshape(ts, nb, RG_BLOCK_W)

    def step(t, h):
        tt = (ts - 1 - t) if reverse else t
        h = a_scr[tt] * h + u_scr[tt]
        o_ref[tt] = h
        return h

    h_scr[...] = lax.fori_loop(0, ts, step, h_scr[...], unroll=8)


def rglru_scan(yx3, conv_w, conv_b, w_a, b_a, w_x, b_x, lam, *, reverse, ts, tc):
    seq, nb, two_c = yx3.shape
    c = two_c // 2
    nc, n_steps = c // tc, seq // ts
    kb = tc // RG_BLOCK_W

    def chunk_of(s):
        return (n_steps - 1 - s) if reverse else s

    vec = lambda rows: pl.BlockSpec((rows, tc), lambda ci, s: (0, ci))
    mat = pl.BlockSpec((kb, RG_BLOCK_W, RG_BLOCK_W), lambda ci, s: (ci, 0, 0))
    return pl.pallas_call(
        functools.partial(_rglru_scan_kernel, reverse=reverse, n_steps=n_steps),
        grid=(nc, n_steps),
        in_specs=[
            pl.BlockSpec((ts, nb, tc), lambda ci, s: (chunk_of(s), 0, nc + ci)),
            pl.BlockSpec((CONV_LEFT, nb, tc),
                         lambda ci, s: (jnp.maximum(chunk_of(s) * (ts // CONV_LEFT) - 1, 0), 0, nc + ci)),
            pl.BlockSpec((1, nb, tc),
                         lambda ci, s: (jnp.minimum((chunk_of(s) + 1) * ts, seq - 1), 0, nc + ci)),
            vec(CONV_W), vec(1), mat, vec(1), mat, vec(1), vec(1),
        ],
        out_specs=pl.BlockSpec((ts, nb, tc), lambda ci, s: (chunk_of(s), 0, ci)),
        out_shape=jax.ShapeDtypeStruct((seq, nb, c), F32),
        scratch_shapes=[
            pltpu.VMEM((ts, nb, tc), F32),
            pltpu.VMEM((ts, nb, tc), F32),
            pltpu.VMEM((nb, tc), F32),
        ],
        compiler_params=_params(("parallel", "arbitrary")),
        name="rglru_scan_bwd" if reverse else "rglru_scan_fwd",
    )(yx3, yx3, yx3, conv_w, conv_b, w_a, b_a, w_x, b_x, lam)


def _gelu_tanh(y):
    return y * (0.5 * (1.0 + jnp.tanh(0.7978845608028654 * (y + 0.044715 * (y * y * y)))))


def _rglru_out_kernel(y_ref, hf_ref, hb_ref, w_ref, x_ref, o_ref):
    gated = _gelu_tanh(y_ref[...]) * (hf_ref[...] + hb_ref[...])
    o_ref[...] = x_ref[...] + jnp.dot(gated.astype(BF16), w_ref[...], preferred_element_type=F32)


def rglru_out_project(yx, hf, hb, w_out, x2, batch, seq, *, tm):
    c, d = w_out.shape
    ns = seq // tm
    return pl.pallas_call(
        _rglru_out_kernel,
        grid=(batch, ns),
        in_specs=[
            pl.BlockSpec((tm, c), lambda b, s: (s, 2 * b)),
            pl.BlockSpec((tm, c), lambda b, s: (s, b)),
            pl.BlockSpec((tm, c), lambda b, s: (s, b)),
            pl.BlockSpec((c, d), lambda b, s: (0, 0)),
            pl.BlockSpec((tm, d), lambda b, s: (b * ns + s, 0)),
        ],
        out_specs=pl.BlockSpec((tm, d), lambda b, s: (b * ns + s, 0)),
        out_shape=jax.ShapeDtypeStruct(x2.shape, F32),
        compiler_params=_params(("parallel", "parallel")),
        name="rglru_out_project",
    )(yx, hf, hb, w_out, x2)


def _ffn_kernel(x_ref, g_ref, wg_ref, wu_ref, wd_ref, c_ref, o_ref, h_scr, acc_scr, *, weighted):
    e = pl.program_id(1)
    f = pl.program_id(2)

    @pl.when((e == 0) & (f == 0))
    def _():
        h_scr[...] = _rms(x_ref[...], g_ref[...]).astype(BF16)
        acc_scr[...] = jnp.zeros(acc_scr.shape, F32)

    h = h_scr[...]
    gate = jnp.dot(h, wg_ref[...], preferred_element_type=F32)
    up = jnp.dot(h, wu_ref[...], preferred_element_type=F32)
    act = gate * jax.nn.sigmoid(gate) * up
    if weighted:
        c = c_ref[...]
        lane = lax.broadcasted_iota(jnp.int32, c.shape, 1)
        act = act * jnp.sum(jnp.where(lane == e, c, 0.0), axis=-1, keepdims=True)
    acc_scr[...] += jnp.dot(act.astype(BF16), wd_ref[...], preferred_element_type=F32)

    @pl.when((e == pl.num_programs(1) - 1) & (f == pl.num_programs(2) - 1))
    def _():
        o_ref[...] = x_ref[...] + acc_scr[...]


def swiglu_ffn(x2, gain, w_gate_up, w_down, combine, *, tm, tf):
    t, d = x2.shape
    n_e, ff, _ = w_down.shape
    nf = ff // tf
    weighted = combine is not None
    if combine is None:
        combine = jnp.ones((8, LANES), F32)
        c_spec = pl.BlockSpec((8, LANES), lambda i, e, f: (0, 0))
    else:
        c_spec = pl.BlockSpec((tm, LANES), lambda i, e, f: (i, 0))
    return pl.pallas_call(
        functools.partial(_ffn_kernel, weighted=weighted),
        grid=(t // tm, n_e, nf),
        in_specs=[
            pl.BlockSpec((tm, d), lambda i, e, f: (i, 0)),
            pl.BlockSpec((1, d), lambda i, e, f: (0, 0)),
            pl.BlockSpec((None, d, tf), lambda i, e, f: (e, 0, f)),
            pl.BlockSpec((None, d, tf), lambda i, e, f: (e, 0, nf + f)),
            pl.BlockSpec((None, tf, d), lambda i, e, f: (e, f, 0)),
            c_spec,
        ],
        out_specs=pl.BlockSpec((tm, d), lambda i, e, f: (i, 0)),
        out_shape=jax.ShapeDtypeStruct((t, d), F32),
        scratch_shapes=[pltpu.VMEM((tm, d), BF16), pltpu.VMEM((tm, d), F32)],
        compiler_params=_params(("parallel", "arbitrary", "arbitrary")),
        name="moe_ffn" if weighted else "dense_ffn",
    )(x2, gain, w_gate_up, w_gate_up, w_down, combine)


def _router_kernel(x_ref, g_ref, r_ref, b_ref, c_ref):
    h = _rms(x_ref[...], g_ref[...])
    logits = jnp.dot(h, r_ref[...], preferred_element_type=F32, precision=lax.Precision.HIGHEST)
    logits = logits + b_ref[...]
    lane = lax.broadcasted_iota(jnp.int32, logits.shape, 1)
    logits = jnp.where(lane < N_EXPERTS, logits, NEG_INF)
    v1 = jnp.max(logits, axis=-1, keepdims=True)
    i1 = jnp.min(jnp.where(logits == v1, lane, LANES), axis=-1, keepdims=True)
    rest = jnp.where(lane == i1, NEG_INF, logits)
    v2 = jnp.max(rest, axis=-1, keepdims=True)
    i2 = jnp.min(jnp.where(rest == v2, lane, LANES), axis=-1, keepdims=True)
    e2 = jnp.exp(v2 - v1)
    g1 = 1.0 / (1.0 + e2)
    g2 = e2 / (1.0 + e2)
    c_ref[...] = jnp.where(lane == i1, g1, 0.0) + jnp.where(lane == i2, g2, 0.0)


def moe_router(x2, gain, router, router_b, *, tm):
    t, d = x2.shape
    r_pad = jnp.zeros((d, LANES), F32).at[:, :N_EXPERTS].set(router)
    b_pad = jnp.zeros((1, LANES), F32).at[0, :N_EXPERTS].set(router_b)
    return pl.pallas_call(
        _router_kernel,
        grid=(t // tm,),
        in_specs=[
            pl.BlockSpec((tm, d), lambda i: (i, 0)),
            pl.BlockSpec((1, d), lambda i: (0, 0)),
            pl.BlockSpec((d, LANES), lambda i: (0, 0)),
            pl.BlockSpec((1, LANES), lambda i: (0, 0)),
        ],
        out_specs=pl.BlockSpec((tm, LANES), lambda i: (i, 0)),
        out_shape=jax.ShapeDtypeStruct((t, LANES), F32),
        compiler_params=_params(("parallel",)),
        name="moe_router",
    )(x2, gain, r_pad, b_pad)


def _inv_freq(dim, theta):
    return theta ** (-jnp.arange(0, dim, 2, dtype=F32) / dim)


def _axial_tables(seq):
    half = HEAD_DIM // 2
    t = jnp.arange(seq)
    rows = (t // GRID_W).astype(F32)
    cols = (t % GRID_W).astype(F32)
    freq = _inv_freq(half, AXIAL_THETA)
    ang_r = rows[:, None] * freq[None, :]
    ang_c = cols[:, None] * freq[None, :]
    ang = jnp.concatenate([ang_r, ang_r, ang_c, ang_c], axis=-1)
    sign = jnp.tile(jnp.concatenate([-jnp.ones(half // 2, F32), jnp.ones(half // 2, F32)]), 2)
    return jnp.cos(ang), jnp.sin(ang) * sign[None, :]


def _partial_tables(seq):
    pos = jnp.arange(seq, dtype=F32)
    freq = _inv_freq(ROT_DIMS, ROPE_THETA)
    ang = pos[:, None] * freq[None, :]
    cos = jnp.concatenate([jnp.cos(ang), jnp.cos(ang), jnp.ones((seq, HEAD_DIM - ROT_DIMS), F32)], axis=-1)
    sin = jnp.concatenate([-jnp.sin(ang), jnp.sin(ang), jnp.zeros((seq, HEAD_DIM - ROT_DIMS), F32)], axis=-1)
    return cos, sin


def _row(v):
    return v.reshape(1, -1).astype(F32)


def _rglru_layer(x2, batch, seq, norm, w_in, conv_w, conv_b, w_a, b_a, w_x, b_x, lam, w_out):
    c = w_out.shape[0]
    yx = norm_matmul_time_major(x2, _row(norm), w_in.astype(BF16), batch, seq,
                                tm=512, tn=1024, out_dtype=F32)
    yx3 = yx.reshape(seq, batch, 2 * c)
    hs = []
    for direction in range(2):
        hs.append(rglru_scan(
            yx3, conv_w, _row(conv_b), w_a[direction].astype(BF16), _row(b_a[direction]),
            w_x[direction].astype(BF16), _row(b_x[direction]), _row(lam[direction]),
            reverse=direction == 1, ts=256, tc=512).reshape(seq, batch * c))
    return rglru_out_project(yx, hs[0], hs[1], w_out.astype(BF16), x2, batch, seq, tm=512)


def _gqa_layer(x2, batch, seq, norm, w_qkv, q_gain, k_gain, w_out):
    nq = GQA_HEADS * HEAD_DIM
    nkv = GQA_KV_HEADS * HEAD_DIM
    tn = nkv
    modes = jnp.array([1] * (nq // tn) + [2] * (nkv // tn) + [0] * (nkv // tn), jnp.int32)
    cos, sin = _axial_tables(seq)
    qkv = qkv_project(x2, _row(norm), w_qkv.astype(BF16), modes, cos, sin, _row(q_gain), _row(k_gain),
                      batch, seq, tm=512, tn=tn, half=HEAD_DIM // 4, scale=HEAD_DIM ** -0.5, dil=1)
    o = gqa_attention(qkv.reshape(batch, seq, nq + 2 * nkv), tq=256, tk=512, streams=2)
    return matmul_residual(o.reshape(batch * seq, nq), w_out.astype(BF16), x2, tm=512)


def _dilated_layer(x2, batch, seq, norm, w_qkv, q_gain, k_gain, w_out):
    hw = DIL_HEADS * HEAD_DIM
    modes = jnp.array([1, 2, 0], jnp.int32)
    cos, sin = _partial_tables(seq)
    w16 = w_qkv.astype(BF16)
    outs, lses = [], []
    for g, (window, dil) in enumerate(DIL_PATTERNS):
        span = window // (2 * dil)
        assert seq % (dil * span) == 0
        qkv = qkv_project(x2, _row(norm), w16[:, g * 3 * hw:(g + 1) * 3 * hw], modes, cos, sin,
                          _row(q_gain), _row(k_gain), batch, seq,
                          tm=512, tn=hw, half=ROT_DIMS // 2, scale=HEAD_DIM ** -0.5, dil=dil)
        o, lse = dilated_attention(qkv, span, tq=2 * span)
        outs.append(o)
        lses.append(lse)
    return dilated_merge_project(outs, lses, w_out.astype(BF16), x2, batch, seq, tm=512)


def _dense_ffn_layer(x2, norm, w_gate_up, w_down):
    return swiglu_ffn(x2, _row(norm), w_gate_up.astype(BF16)[None], w_down.astype(BF16)[None], None,
                      tm=1024, tf=512)


def _moe_layer(x2, norm, router, router_b, w_gate_up, w_down):
    combine = moe_router(x2, _row(norm), router, router_b, tm=512)
    return swiglu_ffn(x2, _row(norm), w_gate_up.astype(BF16), w_down.astype(BF16), combine, tm=1024, tf=512)


def kernel(x, l0_attn_norm, l0_rg_w_in, l0_rg_conv_w, l0_rg_conv_b, l0_rg_w_a, l0_rg_b_a, l0_rg_w_x, l0_rg_b_x, l0_rg_lambda, l0_rg_w_out, l0_ffn_norm, l0_ffn_w_gate_up, l0_ffn_w_down, l1_attn_norm, l1_gqa_w_qkv, l1_gqa_q_norm, l1_gqa_k_norm, l1_gqa_w_out, l1_ffn_norm, l1_moe_router, l1_moe_router_b, l1_moe_w_gate_up, l1_moe_w_down, l2_attn_norm, l2_dil_w_qkv, l2_dil_q_norm, l2_dil_k_norm, l2_dil_w_out, l2_ffn_norm, l2_ffn_w_gate_up, l2_ffn_w_down, l3_attn_norm, l3_rg_w_in, l3_rg_conv_w, l3_rg_conv_b, l3_rg_w_a, l3_rg_b_a, l3_rg_w_x, l3_rg_b_x, l3_rg_lambda, l3_rg_w_out, l3_ffn_norm, l3_moe_router, l3_moe_router_b, l3_moe_w_gate_up, l3_moe_w_down):
    batch, seq, d = x.shape
    x2 = x.reshape(batch * seq, d)
    x2 = _rglru_layer(x2, batch, seq, l0_attn_norm, l0_rg_w_in, l0_rg_conv_w, l0_rg_conv_b, l0_rg_w_a,
                      l0_rg_b_a, l0_rg_w_x, l0_rg_b_x, l0_rg_lambda, l0_rg_w_out)
    x2 = _dense_ffn_layer(x2, l0_ffn_norm, l0_ffn_w_gate_up, l0_ffn_w_down)
    x2 = _gqa_layer(x2, batch, seq, l1_attn_norm, l1_gqa_w_qkv, l1_gqa_q_norm, l1_gqa_k_norm, l1_gqa_w_out)
    x2 = _moe_layer(x2, l1_ffn_norm, l1_moe_router, l1_moe_router_b, l1_moe_w_gate_up, l1_moe_w_down)
    x2 = _dilated_layer(x2, batch, seq, l2_attn_norm, l2_dil_w_qkv, l2_dil_q_norm, l2_dil_k_norm, l2_dil_w_out)
    x2 = _dense_ffn_layer(x2, l2_ffn_norm, l2_ffn_w_gate_up, l2_ffn_w_down)
    x2 = _rglru_layer(x2, batch, seq, l3_attn_norm, l3_rg_w_in, l3_rg_conv_w, l3_rg_conv_b, l3_rg_w_a,
                      l3_rg_b_a, l3_rg_w_x, l3_rg_b_x, l3_rg_lambda, l3_rg_w_out)
    x2 = _moe_layer(x2, l3_ffn_norm, l3_moe_router, l3_moe_router_b, l3_moe_w_gate_up, l3_moe_w_down)
    return x2.reshape(batch, seq, d)
```

```python
import functools

import jax
import jax.numpy as jnp
from jax import lax
from jax.experimental import pallas as pl
from jax.experimental.pallas import tpu as pltpu

F32 = jnp.float32
BF16 = jnp.bfloat16

EPS = 1e-6
NEG_INF = -1e30
GRID_W = 64
HEAD_DIM = 128
LANES = 128
RG_BLOCK_W = 128
CONV_W = 4
CONV_LEFT = 2
RG_C = 8.0
GQA_HEADS = 8
GQA_KV_HEADS = 2
GQA_GROUP = GQA_HEADS // GQA_KV_HEADS
AXIAL_THETA = 10000.0
DIL_PATTERNS = ((128, 1), (512, 4), (2048, 16))
DIL_HEADS = 8
ROPE_THETA = 500000.0
ROT_DIMS = HEAD_DIM // 4
N_EXPERTS = 8
TOP_K = 2
VMEM_LIMIT = 56 * 1024 * 1024

NT_DIMS = (((1,), (1,)), ((), ()))


def _params(sem, vmem=VMEM_LIMIT):
    return pltpu.CompilerParams(dimension_semantics=sem, vmem_limit_bytes=vmem)


def _rms(x, g):
    return x * lax.rsqrt(jnp.mean(x * x, axis=-1, keepdims=True) + EPS) * g


def _norm_matmul_kernel(x_ref, g_ref, w_ref, o_ref, h_scr):
    @pl.when(pl.program_id(2) == 0)
    def _():
        h_scr[...] = _rms(x_ref[...], g_ref[...]).astype(BF16)

    o_ref[...] = jnp.dot(h_scr[...], w_ref[...], preferred_element_type=F32).astype(o_ref.dtype)


def norm_matmul_time_major(x2, gain, w, batch, seq, *, tm, tn, out_dtype):
    d = x2.shape[1]
    n = w.shape[1]
    ns, nj = seq // tm, n // tn
    return pl.pallas_call(
        _norm_matmul_kernel,
        grid=(batch, ns, nj),
        in_specs=[
            pl.BlockSpec((tm, d), lambda b, s, j: (b * ns + s, 0)),
            pl.BlockSpec((1, d), lambda b, s, j: (0, 0)),
            pl.BlockSpec((d, tn), lambda b, s, j: (0, j)),
        ],
        out_specs=pl.BlockSpec((tm, tn), lambda b, s, j: (s, b * nj + j)),
        out_shape=jax.ShapeDtypeStruct((seq, batch * n), out_dtype),
        scratch_shapes=[pltpu.VMEM((tm, d), BF16)],
        compiler_params=_params(("parallel", "parallel", "arbitrary")),
        name="norm_matmul_tm",
    )(x2, gain, w)


def _rope_head(a, gain, cos, sin, lo_mask, half, scale):
    y = _rms(a, gain)
    swapped = jnp.where(lo_mask, pltpu.roll(y, LANES - half, 1), pltpu.roll(y, half, 1))
    out = y * cos + swapped * sin
    if scale != 1.0:
        out = out * scale
    return out


def _qkv_kernel(modes_ref, x_ref, g_ref, w_ref, cos_ref, sin_ref, qg_ref, kg_ref, o_ref, h_scr, *stage,
                half, scale, dil):
    j = pl.program_id(1)
    tm = x_ref.shape[0]
    per = tm // dil

    @pl.when(j == 0)
    def _():
        hn = _rms(x_ref[...], g_ref[...])
        if dil == 1:
            h_scr[...] = hn.astype(BF16)
        else:
            stage_scr, = stage
            for c in range(hn.shape[1] // LANES):
                sl = slice(c * LANES, (c + 1) * LANES)
                stage_scr[c] = hn[:, sl]
                for r in range(dil):
                    h_scr[r * per:(r + 1) * per, sl] = stage_scr[c, pl.ds(r, per, stride=dil), :].astype(BF16)

    mode = modes_ref[j]
    heads = w_ref.shape[1] // LANES

    def emit(fn):
        h = h_scr[...]
        for c in range(heads):
            sl = slice(c * LANES, (c + 1) * LANES)
            a = jnp.dot(h, w_ref[:, sl], preferred_element_type=F32)
            o_ref[:, :, sl] = fn(a).reshape(dil, per, LANES).astype(o_ref.dtype)

    def roped(gain, sc):
        cos = cos_ref[...].reshape(tm, LANES)
        sin = sin_ref[...].reshape(tm, LANES)
        lane = lax.broadcasted_iota(jnp.int32, cos.shape, 1)
        lo_mask = (lane % (2 * half)) < half
        emit(lambda a: _rope_head(a, gain, cos, sin, lo_mask, half, sc))

    @pl.when(mode == 0)
    def _():
        emit(lambda a: a)

    @pl.when(mode == 1)
    def _():
        roped(qg_ref[...], scale)

    @pl.when(mode == 2)
    def _():
        roped(kg_ref[...], 1.0)


def _residue_major(table, dil, n_tiles, per):
    s, w = table.shape
    return table.reshape(s // dil, dil, w).transpose(1, 0, 2).reshape(dil, n_tiles, per, w)


def qkv_project(x2, gain, w, modes, cos, sin, q_gain, k_gain, batch, seq, *, tm, tn, half, scale, dil):
    t, d = x2.shape
    n = w.shape[1]
    ns = seq // tm
    per = tm // dil
    cos_r = _residue_major(cos, dil, ns, per)
    sin_r = _residue_major(sin, dil, ns, per)
    table_spec = pl.BlockSpec((dil, None, per, LANES), lambda i, j, m: (0, i % ns, 0, 0))
    grid_spec = pltpu.PrefetchScalarGridSpec(
        num_scalar_prefetch=1,
        grid=(t // tm, n // tn),
        in_specs=[
            pl.BlockSpec((tm, d), lambda i, j, m: (i, 0)),
            pl.BlockSpec((1, d), lambda i, j, m: (0, 0)),
            pl.BlockSpec((d, tn), lambda i, j, m: (0, j)),
            table_spec,
            table_spec,
            pl.BlockSpec((1, LANES), lambda i, j, m: (0, 0)),
            pl.BlockSpec((1, LANES), lambda i, j, m: (0, 0)),
        ],
        out_specs=pl.BlockSpec((None, dil, None, per, tn), lambda i, j, m: (i // ns, 0, i % ns, 0, j)),
        scratch_shapes=[pltpu.VMEM((tm, d), BF16)]
        + ([pltpu.VMEM((d // LANES, tm, LANES), F32)] if dil > 1 else []),
    )
    out = pl.pallas_call(
        functools.partial(_qkv_kernel, half=half, scale=scale, dil=dil),
        grid_spec=grid_spec,
        out_shape=jax.ShapeDtypeStruct((batch, dil, ns, per, n), BF16),
        compiler_params=_params(("parallel", "arbitrary")),
        name=f"qkv_project_d{dil}",
    )(modes, x2, gain, w, cos_r, sin_r, q_gain, k_gain)
    return out.reshape(batch, dil, seq // dil, n)


def _matmul_residual_kernel(a_ref, w_ref, x_ref, o_ref):
    o_ref[...] = x_ref[...] + jnp.dot(a_ref[...], w_ref[...], preferred_element_type=F32)


def matmul_residual(a, w, x2, *, tm):
    t, k = a.shape
    n = w.shape[1]
    return pl.pallas_call(
        _matmul_residual_kernel,
        grid=(t // tm,),
        in_specs=[
            pl.BlockSpec((tm, k), lambda i: (i, 0)),
            pl.BlockSpec((k, n), lambda i: (0, 0)),
            pl.BlockSpec((tm, n), lambda i: (i, 0)),
        ],
        out_specs=pl.BlockSpec((tm, n), lambda i: (i, 0)),
        out_shape=jax.ShapeDtypeStruct((t, n), F32),
        compiler_params=_params(("parallel",)),
        name="matmul_residual",
    )(a, w, x2)


def _gqa_attn_kernel(q_ref, k_ref, v_ref, o_ref, *, tk, streams):
    tq = q_ref.shape[1]
    seq = k_ref.shape[1]
    per = GQA_GROUP // streams
    qs = [jnp.concatenate([q_ref[0, :, g * LANES:(g + 1) * LANES] for g in range(st * per, (st + 1) * per)],
                          axis=0) for st in range(streams)]
    rows = per * tq
    ones = jnp.ones((tk, LANES), BF16)

    def body(c, carry):
        start = pl.multiple_of(c * tk, tk)
        k = k_ref[0, pl.ds(start, tk), :]
        v1 = jnp.concatenate([v_ref[0, pl.ds(start, tk), :], ones], axis=1)
        new = []
        for st in range(streams):
            m_prev, acc = carry[st]
            s = lax.dot_general(qs[st], k, NT_DIMS, preferred_element_type=F32)
            m_new = jnp.maximum(m_prev, jnp.max(s, axis=-1, keepdims=True))
            alpha = jnp.exp(m_prev - m_new)
            p = jnp.exp(s - m_new).astype(BF16)
            acc = alpha * acc + jnp.dot(p, v1, preferred_element_type=F32)
            new.append((m_new, acc))
        return tuple(new)

    init = tuple((jnp.full((rows, 1), NEG_INF, F32), jnp.zeros((rows, 2 * LANES), F32)) for _ in range(streams))
    final = lax.fori_loop(0, seq // tk, body, init)
    for st in range(streams):
        acc = final[st][1]
        o = acc[:, :LANES] / acc[:, LANES:]
        for gi in range(per):
            g = st * per + gi
            o_ref[0, :, g * LANES:(g + 1) * LANES] = o[gi * tq:(gi + 1) * tq].astype(o_ref.dtype)


def gqa_attention(qkv, *, tq, tk, streams):
    b, s, _ = qkv.shape
    gw = GQA_GROUP * HEAD_DIM
    k_off = GQA_HEADS
    v_off = GQA_HEADS + GQA_KV_HEADS
    return pl.pallas_call(
        functools.partial(_gqa_attn_kernel, tk=tk, streams=streams),
        grid=(b, GQA_KV_HEADS, s // tq),
        in_specs=[
            pl.BlockSpec((1, tq, gw), lambda bb, n, i: (bb, i, n)),
            pl.BlockSpec((1, s, HEAD_DIM), lambda bb, n, i: (bb, 0, k_off + n)),
            pl.BlockSpec((1, s, HEAD_DIM), lambda bb, n, i: (bb, 0, v_off + n)),
        ],
        out_specs=pl.BlockSpec((1, tq, gw), lambda bb, n, i: (bb, i, n)),
        out_shape=jax.ShapeDtypeStruct((b, s, GQA_HEADS * HEAD_DIM), BF16),
        compiler_params=_params(("parallel", "parallel", "arbitrary")),
        name="gqa_attention",
    )(qkv, qkv, qkv)


def _dil_attn_kernel(q_ref, kp_ref, kc_ref, kn_ref, vp_ref, vc_ref, vn_ref, o_ref, lse_ref,
                     *, span, m_total):
    i = pl.program_id(2)
    tq = q_ref.shape[0]
    nk = tq + 2 * span
    row = lax.broadcasted_iota(jnp.int32, (tq, nk), 0)
    col = lax.broadcasted_iota(jnp.int32, (tq, nk), 1)
    key_idx = i * tq - span + col
    ok = (jnp.abs(col - span - row) <= span) & (key_idx >= 0) & (key_idx < m_total)
    lane = lax.broadcasted_iota(jnp.int32, (tq, LANES), 1)
    lse_tile = jnp.zeros((tq, LANES), F32)

    def window(p_ref, c_ref, n_ref, sl):
        return jnp.concatenate([p_ref[tq - span:, sl], c_ref[:, sl], n_ref[:span, sl]], axis=0)

    for h in range(DIL_HEADS):
        sl = slice(h * LANES, (h + 1) * LANES)
        k = window(kp_ref, kc_ref, kn_ref, sl)
        v = window(vp_ref, vc_ref, vn_ref, sl)
        s = lax.dot_general(q_ref[:, sl], k, NT_DIMS, preferred_element_type=F32)
        s = jnp.where(ok, s, NEG_INF)
        m = jnp.max(s, axis=-1, keepdims=True)
        p = jnp.exp(s - m)
        l = jnp.sum(p, axis=-1, keepdims=True)
        o = jnp.dot(p.astype(BF16), v, preferred_element_type=F32) / l
        o_ref[:, sl] = o.astype(o_ref.dtype)
        lse_tile = jnp.where(lane == h, m + jnp.log(l), lse_tile)
    lse_ref[...] = lse_tile


def dilated_attention(qkv, span, *, tq):
    b, dil, m_total, n = qkv.shape
    hw = DIL_HEADS * HEAD_DIM
    nb = m_total // tq

    def spec(col, shift):
        return pl.BlockSpec((None, None, tq, hw),
                            lambda bb, r, i: (bb, r, jnp.clip(i + shift, 0, nb - 1), col))

    return pl.pallas_call(
        functools.partial(_dil_attn_kernel, span=span, m_total=m_total),
        grid=(b, dil, nb),
        in_specs=[spec(0, 0), spec(1, -1), spec(1, 0), spec(1, 1), spec(2, -1), spec(2, 0), spec(2, 1)],
        out_specs=[
            pl.BlockSpec((None, None, tq, hw), lambda bb, r, i: (bb, r, i, 0)),
            pl.BlockSpec((None, None, tq, LANES), lambda bb, r, i: (bb, r, i, 0)),
        ],
        out_shape=[
            jax.ShapeDtypeStruct((b, dil, m_total, hw), BF16),
            jax.ShapeDtypeStruct((b, dil, m_total, LANES), F32),
        ],
        compiler_params=_params(("parallel", "parallel", "arbitrary")),
        name=f"dilated_attention_d{dil}",
    )(qkv, qkv, qkv, qkv, qkv, qkv, qkv)


def _dil_merge_kernel(o0_ref, o1_ref, o2_ref, l0_ref, l1_ref, l2_ref, w_ref, x_ref, out_ref, o_scr, l_scr):
    def to_tokens(ref, scr, g):
        dil, per, width = ref.shape
        for c in range(width // LANES):
            sl = slice(c * LANES, (c + 1) * LANES)
            if dil == 1:
                scr[g, c] = ref[0, :, sl].astype(F32)
            else:
                for r in range(dil):
                    scr[g, c, pl.ds(r, per, stride=dil), :] = ref[r, :, sl].astype(F32)

    for g, (o_ref, l_ref) in enumerate(((o0_ref, l0_ref), (o1_ref, l1_ref), (o2_ref, l2_ref))):
        to_tokens(o_ref, o_scr, g)
        to_tokens(l_ref, l_scr, g)
    lses = [l_scr[g, 0] for g in range(3)]
    mx = jnp.maximum(jnp.maximum(lses[0], lses[1]), lses[2])
    es = [jnp.exp(l - mx) for l in lses]
    den = es[0] + es[1] + es[2]
    ws = [e / den for e in es]
    parts = []
    for h in range(DIL_HEADS):
        acc = None
        for g in range(3):
            term = o_scr[g, h] * ws[g][:, h:h + 1]
            acc = term if acc is None else acc + term
        parts.append(acc.astype(BF16))
    merged = jnp.concatenate(parts, axis=1)
    out_ref[...] = x_ref[...] + jnp.dot(merged, w_ref[...], preferred_element_type=F32)


def dilated_merge_project(outs, lses, w_out, x2, batch, seq, *, tm):
    t, d = x2.shape
    hw = w_out.shape[0]
    ns = seq // tm

    def grouped(arr):
        _, dil, m_total, w = arr.shape
        per = tm // dil
        view = arr.reshape(batch, dil, ns, per, w)
        return view, pl.BlockSpec((None, dil, None, per, w), lambda i: (i // ns, 0, i % ns, 0, 0))

    views, specs = zip(*[grouped(a) for a in (*outs, *lses)])
    row = pl.BlockSpec((tm, d), lambda i: (i, 0))
    return pl.pallas_call(
        _dil_merge_kernel,
        grid=(t // tm,),
        in_specs=list(specs) + [pl.BlockSpec((hw, d), lambda i: (0, 0)), row],
        out_specs=row,
        out_shape=jax.ShapeDtypeStruct((t, d), F32),
        scratch_shapes=[pltpu.VMEM((3, hw // LANES, tm, LANES), F32), pltpu.VMEM((3, 1, tm, LANES), F32)],
        compiler_params=_params(("parallel",)),
        name="dilated_merge_project",
    )(*views, w_out, x2)


def _rglru_scan_kernel(xc_ref, xp_ref, xn_ref, cw_ref, cb_ref, wa_ref, ba_ref, wx_ref, bx_ref, lam_ref,
                       o_ref, a_scr, u_scr, h_scr, *, reverse, n_steps):
    s = pl.program_id(1)
    ts, nb, tc = xc_ref.shape
    chunk = (n_steps - 1 - s) if reverse else s

    @pl.when(s == 0)
    def _():
        h_scr[...] = jnp.zeros(h_scr.shape, F32)

    prev = jnp.where(chunk == 0, 0.0, xp_ref[...])
    nxt = jnp.where(chunk == n_steps - 1, 0.0, xn_ref[...])
    xe = jnp.concatenate([prev, xc_ref[...], nxt], axis=0)
    cw = cw_ref[...]
    xc = cb_ref[...].reshape(1, 1, tc) + xe[0:ts] * cw[0:1].reshape(1, 1, tc)
    for j in range(1, CONV_W):
        xc = xc + xe[j:j + ts] * cw[j:j + 1].reshape(1, 1, tc)
    xc2 = xc.reshape(ts * nb, tc)
    lam = lam_ref[...]
    softplus_neg_lam = jnp.maximum(-lam, 0.0) + jnp.log1p(jnp.exp(-jnp.abs(lam)))
    for kb in range(tc // RG_BLOCK_W):
        sl = slice(kb * RG_BLOCK_W, (kb + 1) * RG_BLOCK_W)
        xb = xc2[:, sl]
        xb16 = xb.astype(BF16)
        r = jax.nn.sigmoid(jnp.dot(xb16, wa_ref[kb], preferred_element_type=F32) + ba_ref[:, sl])
        gate = jax.nn.sigmoid(jnp.dot(xb16, wx_ref[kb], preferred_element_type=F32) + bx_ref[:, sl])
        log_a = (-RG_C) * r * softplus_neg_lam[:, sl]
        a = jnp.exp(log_a)
        u = jnp.sqrt(1.0 - a * a) * (gate * xb)
        a_scr[:, :, sl] = a.reshape(ts, nb, RG_BLOCK_W)
        u_scr[:, :, sl] = u.reshape(ts, nb, RG_BLOCK_W)

    def step(t, h):
        tt = (ts - 1 - t) if reverse else t
        h = a_scr[tt] * h + u_scr[tt]
        o_ref[tt] = h
        return h

    h_scr[...] = lax.fori_loop(0, ts, step, h_scr[...], unroll=8)


def rglru_scan(yx3, conv_w, conv_b, w_a, b_a, w_x, b_x, lam, *, reverse, ts, tc):
    seq, nb, two_c = yx3.shape
    c = two_c // 2
    nc, n_steps = c // tc, seq // ts
    kb = tc // RG_BLOCK_W

    def chunk_of(s):
        return (n_steps - 1 - s) if reverse else s

    vec = lambda rows: pl.BlockSpec((rows, tc), lambda ci, s: (0, ci))
    mat = pl.BlockSpec((kb, RG_BLOCK_W, RG_BLOCK_W), lambda ci, s: (ci, 0, 0))
    return pl.pallas_call(
        functools.partial(_rglru_scan_kernel, reverse=reverse, n_steps=n_steps),
        grid=(nc, n_steps),
        in_specs=[
            pl.BlockSpec((ts, nb, tc), lambda ci, s: (chunk_of(s), 0, nc + ci)),
            pl.BlockSpec((CONV_LEFT, nb, tc),
                         lambda ci, s: (jnp.maximum(chunk_of(s) * (ts // CONV_LEFT) - 1, 0), 0, nc + ci)),
            pl.BlockSpec((1, nb, tc),
                         lambda ci, s: (jnp.minimum((chunk_of(s) + 1) * ts, seq - 1), 0, nc + ci)),
            vec(CONV_W), vec(1), mat, vec(1), mat, vec(1), vec(1),
        ],
        out_specs=pl.BlockSpec((ts, nb, tc), lambda ci, s: (chunk_of(s), 0, ci)),
        out_shape=jax.ShapeDtypeStruct((seq, nb, c), F32),
        scratch_shapes=[
            pltpu.VMEM((ts, nb, tc), F32),
            pltpu.VMEM((ts, nb, tc), F32),
            pltpu.VMEM((nb, tc), F32),
        ],
        compiler_params=_params(("parallel", "arbitrary")),
        name="rglru_scan_bwd" if reverse else "rglru_scan_fwd",
    )(yx3, yx3, yx3, conv_w, conv_b, w_a, b_a, w_x, b_x, lam)


def _gelu_tanh(y):
    return y * (0.5 * (1.0 + jnp.tanh(0.7978845608028654 * (y + 0.044715 * (y * y * y)))))


def _rglru_out_kernel(y_ref, hf_ref, hb_ref, w_ref, x_ref, o_ref):
    gated = _gelu_tanh(y_ref[...]) * (hf_ref[...] + hb_ref[...])
    o_ref[...] = x_ref[...] + jnp.dot(gated.astype(BF16), w_ref[...], preferred_element_type=F32)


def rglru_out_project(yx, hf, hb, w_out, x2, batch, seq, *, tm):
    c, d = w_out.shape
    ns = seq // tm
    return pl.pallas_call(
        _rglru_out_kernel,
        grid=(batch, ns),
        in_specs=[
            pl.BlockSpec((tm, c), lambda b, s: (s, 2 * b)),
            pl.BlockSpec((tm, c), lambda b, s: (s, b)),
            pl.BlockSpec((tm, c), lambda b, s: (s, b)),
            pl.BlockSpec((c, d), lambda b, s: (0, 0)),
            pl.BlockSpec((tm, d), lambda b, s: (b * ns + s, 0)),
        ],
        out_specs=pl.BlockSpec((tm, d), lambda b, s: (b * ns + s, 0)),
        out_shape=jax.ShapeDtypeStruct(x2.shape, F32),
        compiler_params=_params(("parallel", "parallel")),
        name="rglru_out_project",
    )(yx, hf, hb, w_out, x2)


def _swiglu_step(h, wg_ref, wu_ref, wd_ref, acc_scr):
    gate = jnp.dot(h, wg_ref[...], preferred_element_type=F32)
    up = jnp.dot(h, wu_ref[...], preferred_element_type=F32)
    act = gate * jax.nn.sigmoid(gate) * up
    acc_scr[...] += jnp.dot(act.astype(BF16), wd_ref[...], preferred_element_type=F32)


def _ffn_kernel(x_ref, g_ref, wg_ref, wu_ref, wd_ref, o_ref, h_scr, acc_scr):
    f = pl.program_id(1)

    @pl.when(f == 0)
    def _():
        h_scr[...] = _rms(x_ref[...], g_ref[...]).astype(BF16)
        acc_scr[...] = jnp.zeros(acc_scr.shape, F32)

    _swiglu_step(h_scr[...], wg_ref, wu_ref, wd_ref, acc_scr)

    @pl.when(f == pl.num_programs(1) - 1)
    def _():
        o_ref[...] = x_ref[...] + acc_scr[...]


def swiglu_ffn(x2, gain, w_gate_up, w_down, *, tm, tf):
    t, d = x2.shape
    ff = w_down.shape[0]
    nf = ff // tf
    return pl.pallas_call(
        _ffn_kernel,
        grid=(t // tm, nf),
        in_specs=[
            pl.BlockSpec((tm, d), lambda i, f: (i, 0)),
            pl.BlockSpec((1, d), lambda i, f: (0, 0)),
            pl.BlockSpec((d, tf), lambda i, f: (0, f)),
            pl.BlockSpec((d, tf), lambda i, f: (0, nf + f)),
            pl.BlockSpec((tf, d), lambda i, f: (f, 0)),
        ],
        out_specs=pl.BlockSpec((tm, d), lambda i, f: (i, 0)),
        out_shape=jax.ShapeDtypeStruct((t, d), F32),
        scratch_shapes=[pltpu.VMEM((tm, d), BF16), pltpu.VMEM((tm, d), F32)],
        compiler_params=_params(("parallel", "arbitrary")),
        name="dense_ffn",
    )(x2, gain, w_gate_up, w_gate_up, w_down)


SUBLANES = 8


def _copy_token(src_hbm, dst, src_tok, dst_tok, sem):
    return pltpu.make_async_copy(src_hbm.at[pl.ds(src_tok * SUBLANES, SUBLANES)],
                                 dst.at[pl.ds(dst_tok * SUBLANES, SUBLANES)], sem)


def _moe_gather_kernel(idx_ref, x_hbm, o_hbm, sem, *, tm):
    p = pl.program_id(0)

    def tile_wait(tile):
        rows = pl.ds(tile * (tm * SUBLANES), tm * SUBLANES)
        pltpu.make_async_copy(o_hbm.at[rows], o_hbm.at[rows], sem.at[tile % 2]).wait()

    def issue(r, carry):
        _copy_token(x_hbm, o_hbm, idx_ref[0, r], p * tm + r, sem.at[p % 2]).start()
        return carry
    lax.fori_loop(0, tm, issue, 0, unroll=8)

    @pl.when(p >= 1)
    def _():
        tile_wait(p - 1)

    @pl.when(p == pl.num_programs(0) - 1)
    def _():
        tile_wait(p)


def moe_gather(x_tiles, row_tok, *, tm):
    n_tiles = row_tok.shape[0]
    return pl.pallas_call(
        functools.partial(_moe_gather_kernel, tm=tm),
        grid=(n_tiles,),
        in_specs=[
            pl.BlockSpec((None, 1, tm), lambda p: (p, 0, 0), memory_space=pltpu.SMEM),
            pl.BlockSpec(memory_space=pl.ANY),
        ],
        out_specs=pl.BlockSpec(memory_space=pl.ANY),
        out_shape=jax.ShapeDtypeStruct((n_tiles * tm * SUBLANES, LANES), F32),
        scratch_shapes=[pltpu.SemaphoreType.DMA((2,))],
        compiler_params=_params(("arbitrary",)),
        name="moe_gather",
    )(row_tok, x_tiles)


def _moe_expert_kernel(te_ref, meta_ref, xs_ref, g_ref, wg_ref, wu_ref, wd_ref, rw_ref, y_ref, h_scr, acc_scr):
    p = pl.program_id(0)
    f = pl.program_id(1)
    tm = acc_scr.shape[0]
    chunks = acc_scr.shape[1] // LANES
    used = p < meta_ref[0]

    @pl.when(used & (f == 0))
    def _():
        x = jnp.concatenate([xs_ref[pl.ds(c, tm, stride=SUBLANES), :] for c in range(chunks)], axis=1)
        h_scr[...] = _rms(x, g_ref[...]).astype(BF16)
        acc_scr[...] = jnp.zeros(acc_scr.shape, F32)

    @pl.when(used)
    def _():
        _swiglu_step(h_scr[...], wg_ref, wu_ref, wd_ref, acc_scr)

    @pl.when(f == pl.num_programs(1) - 1)
    def _():
        y = jnp.where(used, acc_scr[...] * rw_ref[...], 0.0)
        for c in range(chunks):
            y_ref[pl.ds(c, tm, stride=SUBLANES), :] = y[:, c * LANES:(c + 1) * LANES]


def moe_experts(xs, gain, w_gate_up, w_down, row_w, tile_expert, meta, *, tm, tf):
    n_e, ff, d = w_down.shape
    nf = ff // tf
    n_tiles = xs.shape[0] // (tm * SUBLANES)

    def f_eff(p, f, meta_ref):
        return jnp.where(p < meta_ref[0], f, nf - 1)

    grid_spec = pltpu.PrefetchScalarGridSpec(
        num_scalar_prefetch=2,
        grid=(n_tiles, nf),
        in_specs=[
            pl.BlockSpec((tm * SUBLANES, LANES), lambda p, f, te, m: (p, 0)),
            pl.BlockSpec((1, d), lambda p, f, te, m: (0, 0)),
            pl.BlockSpec((None, d, tf), lambda p, f, te, m: (te[p], 0, f_eff(p, f, m))),
            pl.BlockSpec((None, d, tf), lambda p, f, te, m: (te[p], 0, nf + f_eff(p, f, m))),
            pl.BlockSpec((None, tf, d), lambda p, f, te, m: (te[p], f_eff(p, f, m), 0)),
            pl.BlockSpec((tm, 1), lambda p, f, te, m: (p, 0)),
        ],
        out_specs=pl.BlockSpec((tm * SUBLANES, LANES), lambda p, f, te, m: (p, 0)),
        scratch_shapes=[pltpu.VMEM((tm, d), BF16), pltpu.VMEM((tm, d), F32)],
    )
    return pl.pallas_call(
        _moe_expert_kernel,
        grid_spec=grid_spec,
        out_shape=jax.ShapeDtypeStruct(xs.shape, F32),
        compiler_params=_params(("parallel", "arbitrary")),
        name="moe_experts",
    )(tile_expert, meta, xs, gain, w_gate_up, w_gate_up, w_down, row_w)


def _moe_combine_kernel(cur_ref, nxt_ref, x_ref, y_hbm, o_ref, buf, sem, *, tm):
    p = pl.program_id(0)
    n = pl.num_programs(0)
    chunks = x_ref.shape[1] // LANES

    def issue(idx_ref, slot):
        def body(r, carry):
            _copy_token(y_hbm, buf.at[slot], idx_ref[0, r], r, sem.at[slot]).start()
            return carry
        lax.fori_loop(0, 2 * tm, body, 0, unroll=8)

    @pl.when(p == 0)
    def _():
        issue(cur_ref, 0)

    @pl.when(p + 1 < n)
    def _():
        issue(nxt_ref, (p + 1) % 2)

    slot = p % 2
    pltpu.make_async_copy(y_hbm.at[pl.ds(0, 2 * tm * SUBLANES)], buf.at[slot], sem.at[slot]).wait()
    for c in range(chunks):
        first = buf[slot, pl.ds(c, tm, stride=SUBLANES), :]
        second = buf[slot, pl.ds(tm * SUBLANES + c, tm, stride=SUBLANES), :]
        sl = slice(c * LANES, (c + 1) * LANES)
        o_ref[:, sl] = x_ref[:, sl] + first + second


def moe_combine(x2, y_tiles, pos, *, tm):
    t, d = x2.shape
    n = t // tm
    idx_spec = lambda shift: pl.BlockSpec((None, 1, 2 * tm), lambda p: (jnp.minimum(p + shift, n - 1), 0, 0),
                                          memory_space=pltpu.SMEM)
    return pl.pallas_call(
        functools.partial(_moe_combine_kernel, tm=tm),
        grid=(n,),
        in_specs=[idx_spec(0), idx_spec(1), pl.BlockSpec((tm, d), lambda p: (p, 0)),
                  pl.BlockSpec(memory_space=pl.ANY)],
        out_specs=pl.BlockSpec((tm, d), lambda p: (p, 0)),
        out_shape=jax.ShapeDtypeStruct((t, d), F32),
        scratch_shapes=[pltpu.VMEM((2, 2 * tm * SUBLANES, LANES), F32), pltpu.SemaphoreType.DMA((2,))],
        compiler_params=_params(("arbitrary",)),
        name="moe_combine",
    )(pos, pos, x2, y_tiles)


def _moe_plan(route, tm):
    t = route.shape[0]
    experts = route[:, :2].astype(jnp.int32)
    gates = route[:, 2:4]
    onehot = (experts[:, :, None] == jnp.arange(N_EXPERTS)[None, None, :]).astype(jnp.int32).sum(axis=1)
    incl = jnp.cumsum(onehot, axis=0)
    rank = incl - onehot
    counts = incl[-1]
    tiles_per = (counts + tm - 1) // tm
    tile_end = jnp.cumsum(tiles_per)
    row_start = (tile_end - tiles_per) * tm
    dest = row_start[experts] + jnp.take_along_axis(rank, experts, axis=1)
    n_tiles = (TOP_K * t) // tm + N_EXPERTS
    rows = n_tiles * tm
    tok = jnp.broadcast_to(jnp.arange(t, dtype=jnp.int32)[:, None], dest.shape)
    flat = dest.reshape(-1)
    row_tok = jnp.zeros((rows,), jnp.int32).at[flat].set(tok.reshape(-1), unique_indices=True)
    row_w = jnp.zeros((rows,), F32).at[flat].set(gates.reshape(-1), unique_indices=True)
    tile_expert = jnp.minimum(jnp.searchsorted(tile_end, jnp.arange(n_tiles), side="right"),
                              N_EXPERTS - 1).astype(jnp.int32)
    meta = tile_end[-1:].astype(jnp.int32)
    pos = jnp.concatenate([dest[:, 0].reshape(t // tm, tm), dest[:, 1].reshape(t // tm, tm)], axis=1)
    return (row_tok.reshape(n_tiles, 1, tm), row_w.reshape(rows, 1), tile_expert, meta,
            pos.reshape(t // tm, 1, 2 * tm).astype(jnp.int32))


def _router_kernel(x_ref, g_ref, r_ref, b_ref, c_ref):
    h = _rms(x_ref[...], g_ref[...])
    logits = jnp.dot(h, r_ref[...], preferred_element_type=F32, precision=lax.Precision.HIGHEST)
    logits = logits + b_ref[...]
    lane = lax.broadcasted_iota(jnp.int32, logits.shape, 1)
    logits = jnp.where(lane < N_EXPERTS, logits, NEG_INF)
    v1 = jnp.max(logits, axis=-1, keepdims=True)
    i1 = jnp.min(jnp.where(logits == v1, lane, LANES), axis=-1, keepdims=True)
    rest = jnp.where(lane == i1, NEG_INF, logits)
    v2 = jnp.max(rest, axis=-1, keepdims=True)
    i2 = jnp.min(jnp.where(rest == v2, lane, LANES), axis=-1, keepdims=True)
    e2 = jnp.exp(v2 - v1)
    g1 = 1.0 / (1.0 + e2)
    g2 = e2 / (1.0 + e2)
    c_ref[...] = jnp.where(lane == 0, i1.astype(F32), jnp.where(lane == 1, i2.astype(F32),
                           jnp.where(lane == 2, g1, jnp.where(lane == 3, g2, 0.0))))


def moe_router(x2, gain, router, router_b, *, tm):
    t, d = x2.shape
    r_pad = jnp.zeros((d, LANES), F32).at[:, :N_EXPERTS].set(router)
    b_pad = jnp.zeros((1, LANES), F32).at[0, :N_EXPERTS].set(router_b)
    return pl.pallas_call(
        _router_kernel,
        grid=(t // tm,),
        in_specs=[
            pl.BlockSpec((tm, d), lambda i: (i, 0)),
            pl.BlockSpec((1, d), lambda i: (0, 0)),
            pl.BlockSpec((d, LANES), lambda i: (0, 0)),
            pl.BlockSpec((1, LANES), lambda i: (0, 0)),
        ],
        out_specs=pl.BlockSpec((tm, LANES), lambda i: (i, 0)),
        out_shape=jax.ShapeDtypeStruct((t, LANES), F32),
        compiler_params=_params(("parallel",)),
        name="moe_router",
    )(x2, gain, r_pad, b_pad)


def _inv_freq(dim, theta):
    return theta ** (-jnp.arange(0, dim, 2, dtype=F32) / dim)


def _axial_tables(seq):
    half = HEAD_DIM // 2
    t = jnp.arange(seq)
    rows = (t // GRID_W).astype(F32)
    cols = (t % GRID_W).astype(F32)
    freq = _inv_freq(half, AXIAL_THETA)
    ang_r = rows[:, None] * freq[None, :]
    ang_c = cols[:, None] * freq[None, :]
    ang = jnp.concatenate([ang_r, ang_r, ang_c, ang_c], axis=-1)
    sign = jnp.tile(jnp.concatenate([-jnp.ones(half // 2, F32), jnp.ones(half // 2, F32)]), 2)
    return jnp.cos(ang), jnp.sin(ang) * sign[None, :]


def _partial_tables(seq):
    pos = jnp.arange(seq, dtype=F32)
    freq = _inv_freq(ROT_DIMS, ROPE_THETA)
    ang = pos[:, None] * freq[None, :]
    cos = jnp.concatenate([jnp.cos(ang), jnp.cos(ang), jnp.ones((seq, HEAD_DIM - ROT_DIMS), F32)], axis=-1)
    sin = jnp.concatenate([-jnp.sin(ang), jnp.sin(ang), jnp.zeros((seq, HEAD_DIM - ROT_DIMS), F32)], axis=-1)
    return cos, sin


def _row(v):
    return v.reshape(1, -1).astype(F32)


def _rglru_layer(x2, batch, seq, norm, w_in, conv_w, conv_b, w_a, b_a, w_x, b_x, lam, w_out):
    c = w_out.shape[0]
    yx = norm_matmul_time_major(x2, _row(norm), w_in.astype(BF16), batch, seq,
                                tm=512, tn=1024, out_dtype=F32)
    yx3 = yx.reshape(seq, batch, 2 * c)
    hs = []
    for direction in range(2):
        hs.append(rglru_scan(
            yx3, conv_w, _row(conv_b), w_a[direction].astype(BF16), _row(b_a[direction]),
            w_x[direction].astype(BF16), _row(b_x[direction]), _row(lam[direction]),
            reverse=direction == 1, ts=256, tc=512).reshape(seq, batch * c))
    return rglru_out_project(yx, hs[0], hs[1], w_out.astype(BF16), x2, batch, seq, tm=512)


def _gqa_layer(x2, batch, seq, norm, w_qkv, q_gain, k_gain, w_out):
    nq = GQA_HEADS * HEAD_DIM
    nkv = GQA_KV_HEADS * HEAD_DIM
    tn = nkv
    modes = jnp.array([1] * (nq // tn) + [2] * (nkv // tn) + [0] * (nkv // tn), jnp.int32)
    cos, sin = _axial_tables(seq)
    qkv = qkv_project(x2, _row(norm), w_qkv.astype(BF16), modes, cos, sin, _row(q_gain), _row(k_gain),
                      batch, seq, tm=512, tn=tn, half=HEAD_DIM // 4, scale=HEAD_DIM ** -0.5, dil=1)
    o = gqa_attention(qkv.reshape(batch, seq, nq + 2 * nkv), tq=256, tk=512, streams=2)
    return matmul_residual(o.reshape(batch * seq, nq), w_out.astype(BF16), x2, tm=512)


def _dilated_layer(x2, batch, seq, norm, w_qkv, q_gain, k_gain, w_out):
    hw = DIL_HEADS * HEAD_DIM
    modes = jnp.array([1, 2, 0], jnp.int32)
    cos, sin = _partial_tables(seq)
    w16 = w_qkv.astype(BF16)
    outs, lses = [], []
    for g, (window, dil) in enumerate(DIL_PATTERNS):
        span = window // (2 * dil)
        assert seq % (dil * span) == 0
        qkv = qkv_project(x2, _row(norm), w16[:, g * 3 * hw:(g + 1) * 3 * hw], modes, cos, sin,
                          _row(q_gain), _row(k_gain), batch, seq,
                          tm=512, tn=hw, half=ROT_DIMS // 2, scale=HEAD_DIM ** -0.5, dil=dil)
        o, lse = dilated_attention(qkv, span, tq=2 * span)
        outs.append(o)
        lses.append(lse)
    return dilated_merge_project(outs, lses, w_out.astype(BF16), x2, batch, seq, tm=512)


def _dense_ffn_layer(x2, norm, w_gate_up, w_down):
    return swiglu_ffn(x2, _row(norm), w_gate_up.astype(BF16), w_down.astype(BF16), tm=1024, tf=512)


def _moe_layer(x2, norm, router, router_b, w_gate_up, w_down):
    tm = 512
    t, d = x2.shape
    route = moe_router(x2, _row(norm), router, router_b, tm=512)
    row_tok, row_w, tile_expert, meta, pos = _moe_plan(route, tm)
    xs = moe_gather(x2.reshape(t * d // LANES, LANES), row_tok, tm=tm)
    ys = moe_experts(xs, _row(norm), w_gate_up.astype(BF16), w_down.astype(BF16), row_w, tile_expert, meta,
                     tm=tm, tf=512)
    return moe_combine(x2, ys, pos, tm=tm)


def kernel(x, l0_attn_norm, l0_rg_w_in, l0_rg_conv_w, l0_rg_conv_b, l0_rg_w_a, l0_rg_b_a, l0_rg_w_x, l0_rg_b_x, l0_rg_lambda, l0_rg_w_out, l0_ffn_norm, l0_ffn_w_gate_up, l0_ffn_w_down, l1_attn_norm, l1_gqa_w_qkv, l1_gqa_q_norm, l1_gqa_k_norm, l1_gqa_w_out, l1_ffn_norm, l1_moe_router, l1_moe_router_b, l1_moe_w_gate_up, l1_moe_w_down, l2_attn_norm, l2_dil_w_qkv, l2_dil_q_norm, l2_dil_k_norm, l2_dil_w_out, l2_ffn_norm, l2_ffn_w_gate_up, l2_ffn_w_down, l3_attn_norm, l3_rg_w_in, l3_rg_conv_w, l3_rg_conv_b, l3_rg_w_a, l3_rg_b_a, l3_rg_w_x, l3_rg_b_x, l3_rg_lambda, l3_rg_w_out, l3_ffn_norm, l3_moe_router, l3_moe_router_b, l3_moe_w_gate_up, l3_moe_w_down):
    batch, seq, d = x.shape
    x2 = x.reshape(batch * seq, d)
    x2 = _rglru_layer(x2, batch, seq, l0_attn_norm, l0_rg_w_in, l0_rg_conv_w, l0_rg_conv_b, l0_rg_w_a,
                      l0_rg_b_a, l0_rg_w_x, l0_rg_b_x, l0_rg_lambda, l0_rg_w_out)
    x2 = _dense_ffn_layer(x2, l0_ffn_norm, l0_ffn_w_gate_up, l0_ffn_w_down)
    x2 = _gqa_layer(x2, batch, seq, l1_attn_norm, l1_gqa_w_qkv, l1_gqa_q_norm, l1_gqa_k_norm, l1_gqa_w_out)
    x2 = _moe_layer(x2, l1_ffn_norm, l1_moe_router, l1_moe_router_b, l1_moe_w_gate_up, l1_moe_w_down)
    x2 = _dilated_layer(x2, batch, seq, l2_attn_norm, l2_dil_w_qkv, l2_dil_q_norm, l2_dil_k_norm, l2_dil_w_out)
    x2 = _dense_ffn_layer(x2, l2_ffn_norm, l2_ffn_w_gate_up, l2_ffn_w_down)
    x2 = _rglru_layer(x2, batch, seq, l3_attn_norm, l3_rg_w_in, l3_rg_conv_w, l3_rg_conv_b, l3_rg_w_a,
                      l3_rg_b_a, l3_rg_w_x, l3_rg_b_x, l3_rg_lambda, l3_rg_w_out)
    x2 = _moe_layer(x2, l3_ffn_norm, l3_moe_router, l3_moe_router_b, l3_moe_w_gate_up, l3_moe_w_down)
    return x2.reshape(batch, seq, d)
```

```python
import functools

import jax
import jax.numpy as jnp
from jax import lax
from jax.experimental import pallas as pl
from jax.experimental.pallas import tpu as pltpu

F32 = jnp.float32
BF16 = jnp.bfloat16

EPS = 1e-6
NEG_INF = -1e30
GRID_W = 64
HEAD_DIM = 128
LANES = 128
RG_BLOCK_W = 128
CONV_W = 4
CONV_LEFT = 2
RG_C = 8.0
GQA_HEADS = 8
GQA_KV_HEADS = 2
GQA_GROUP = GQA_HEADS // GQA_KV_HEADS
AXIAL_THETA = 10000.0
DIL_PATTERNS = ((128, 1), (512, 4), (2048, 16))
DIL_HEADS = 8
ROPE_THETA = 500000.0
ROT_DIMS = HEAD_DIM // 4
N_EXPERTS = 8
TOP_K = 2
VMEM_LIMIT = 56 * 1024 * 1024

NT_DIMS = (((1,), (1,)), ((), ()))


def _params(sem, vmem=VMEM_LIMIT):
    return pltpu.CompilerParams(dimension_semantics=sem, vmem_limit_bytes=vmem)


def _rms(x, g):
    return x * lax.rsqrt(jnp.mean(x * x, axis=-1, keepdims=True) + EPS) * g


def _norm_matmul_kernel(x_ref, g_ref, w_ref, o_ref, h_scr):
    @pl.when(pl.program_id(2) == 0)
    def _():
        h_scr[...] = _rms(x_ref[...], g_ref[...]).astype(BF16)

    o_ref[...] = jnp.dot(h_scr[...], w_ref[...], preferred_element_type=F32).astype(o_ref.dtype)


def norm_matmul_time_major(x2, gain, w, batch, seq, *, tm, tn, out_dtype):
    d = x2.shape[1]
    n = w.shape[1]
    ns, nj = seq // tm, n // tn
    return pl.pallas_call(
        _norm_matmul_kernel,
        grid=(batch, ns, nj),
        in_specs=[
            pl.BlockSpec((tm, d), lambda b, s, j: (b * ns + s, 0)),
            pl.BlockSpec((1, d), lambda b, s, j: (0, 0)),
            pl.BlockSpec((d, tn), lambda b, s, j: (0, j)),
        ],
        out_specs=pl.BlockSpec((tm, tn), lambda b, s, j: (s, b * nj + j)),
        out_shape=jax.ShapeDtypeStruct((seq, batch * n), out_dtype),
        scratch_shapes=[pltpu.VMEM((tm, d), BF16)],
        compiler_params=_params(("parallel", "parallel", "arbitrary")),
        name="norm_matmul_tm",
    )(x2, gain, w)


def _rope_head(a, gain, cos, sin, lo_mask, half, scale):
    y = _rms(a, gain)
    swapped = jnp.where(lo_mask, pltpu.roll(y, LANES - half, 1), pltpu.roll(y, half, 1))
    out = y * cos + swapped * sin
    if scale != 1.0:
        out = out * scale
    return out


def _qkv_kernel(modes_ref, x_ref, g_ref, w_ref, cos_ref, sin_ref, qg_ref, kg_ref, o_ref, h_scr, *stage,
                half, scale, dil):
    j = pl.program_id(1)
    tm = x_ref.shape[0]
    per = tm // dil

    @pl.when(j == 0)
    def _():
        hn = _rms(x_ref[...], g_ref[...])
        if dil == 1:
            h_scr[...] = hn.astype(BF16)
        else:
            stage_scr, = stage
            for c in range(hn.shape[1] // LANES):
                sl = slice(c * LANES, (c + 1) * LANES)
                stage_scr[c] = hn[:, sl]
                for r in range(dil):
                    h_scr[r * per:(r + 1) * per, sl] = stage_scr[c, pl.ds(r, per, stride=dil), :].astype(BF16)

    mode = modes_ref[j]
    heads = w_ref.shape[1] // LANES

    def emit(fn):
        h = h_scr[...]
        for c in range(heads):
            sl = slice(c * LANES, (c + 1) * LANES)
            a = jnp.dot(h, w_ref[:, sl], preferred_element_type=F32)
            o_ref[:, :, sl] = fn(a).reshape(dil, per, LANES).astype(o_ref.dtype)

    def roped(gain, sc):
        cos = cos_ref[...].reshape(tm, LANES)
        sin = sin_ref[...].reshape(tm, LANES)
        lane = lax.broadcasted_iota(jnp.int32, cos.shape, 1)
        lo_mask = (lane % (2 * half)) < half
        emit(lambda a: _rope_head(a, gain, cos, sin, lo_mask, half, sc))

    @pl.when(mode == 0)
    def _():
        emit(lambda a: a)

    @pl.when(mode == 1)
    def _():
        roped(qg_ref[...], scale)

    @pl.when(mode == 2)
    def _():
        roped(kg_ref[...], 1.0)


def _residue_major(table, dil, n_tiles, per):
    s, w = table.shape
    return table.reshape(s // dil, dil, w).transpose(1, 0, 2).reshape(dil, n_tiles, per, w)


def qkv_project(x2, gain, w, modes, cos, sin, q_gain, k_gain, batch, seq, *, tm, tn, half, scale, dil):
    t, d = x2.shape
    n = w.shape[1]
    ns = seq // tm
    per = tm // dil
    cos_r = _residue_major(cos, dil, ns, per)
    sin_r = _residue_major(sin, dil, ns, per)
    table_spec = pl.BlockSpec((dil, None, per, LANES), lambda i, j, m: (0, i % ns, 0, 0))
    grid_spec = pltpu.PrefetchScalarGridSpec(
        num_scalar_prefetch=1,
        grid=(t // tm, n // tn),
        in_specs=[
            pl.BlockSpec((tm, d), lambda i, j, m: (i, 0)),
            pl.BlockSpec((1, d), lambda i, j, m: (0, 0)),
            pl.BlockSpec((d, tn), lambda i, j, m: (0, j)),
            table_spec,
            table_spec,
            pl.BlockSpec((1, LANES), lambda i, j, m: (0, 0)),
            pl.BlockSpec((1, LANES), lambda i, j, m: (0, 0)),
        ],
        out_specs=pl.BlockSpec((None, dil, None, per, tn), lambda i, j, m: (i // ns, 0, i % ns, 0, j)),
        scratch_shapes=[pltpu.VMEM((tm, d), BF16)]
        + ([pltpu.VMEM((d // LANES, tm, LANES), F32)] if dil > 1 else []),
    )
    out = pl.pallas_call(
        functools.partial(_qkv_kernel, half=half, scale=scale, dil=dil),
        grid_spec=grid_spec,
        out_shape=jax.ShapeDtypeStruct((batch, dil, ns, per, n), BF16),
        compiler_params=_params(("parallel", "arbitrary")),
        name=f"qkv_project_d{dil}",
    )(modes, x2, gain, w, cos_r, sin_r, q_gain, k_gain)
    return out.reshape(batch, dil, seq // dil, n)


def _matmul_residual_kernel(a_ref, w_ref, x_ref, o_ref):
    o_ref[...] = x_ref[...] + jnp.dot(a_ref[...], w_ref[...], preferred_element_type=F32)


def matmul_residual(a, w, x2, *, tm):
    t, k = a.shape
    n = w.shape[1]
    return pl.pallas_call(
        _matmul_residual_kernel,
        grid=(t // tm,),
        in_specs=[
            pl.BlockSpec((tm, k), lambda i: (i, 0)),
            pl.BlockSpec((k, n), lambda i: (0, 0)),
            pl.BlockSpec((tm, n), lambda i: (i, 0)),
        ],
        out_specs=pl.BlockSpec((tm, n), lambda i: (i, 0)),
        out_shape=jax.ShapeDtypeStruct((t, n), F32),
        compiler_params=_params(("parallel",)),
        name="matmul_residual",
    )(a, w, x2)


def _gqa_attn_kernel(q_ref, k_ref, v_ref, o_ref, *, tk, streams):
    tq = q_ref.shape[1]
    seq = k_ref.shape[1]
    per = GQA_GROUP // streams
    qs = [jnp.concatenate([q_ref[0, :, g * LANES:(g + 1) * LANES] for g in range(st * per, (st + 1) * per)],
                          axis=0) for st in range(streams)]
    rows = per * tq
    ones = jnp.ones((tk, LANES), BF16)

    def body(c, carry):
        start = pl.multiple_of(c * tk, tk)
        k = k_ref[0, pl.ds(start, tk), :]
        v1 = jnp.concatenate([v_ref[0, pl.ds(start, tk), :], ones], axis=1)
        new = []
        for st in range(streams):
            m_prev, acc = carry[st]
            s = lax.dot_general(qs[st], k, NT_DIMS, preferred_element_type=F32)
            m_new = jnp.maximum(m_prev, jnp.max(s, axis=-1, keepdims=True))
            alpha = jnp.exp(m_prev - m_new)
            p = jnp.exp(s - m_new).astype(BF16)
            acc = alpha * acc + jnp.dot(p, v1, preferred_element_type=F32)
            new.append((m_new, acc))
        return tuple(new)

    init = tuple((jnp.full((rows, 1), NEG_INF, F32), jnp.zeros((rows, 2 * LANES), F32)) for _ in range(streams))
    final = lax.fori_loop(0, seq // tk, body, init)
    for st in range(streams):
        acc = final[st][1]
        o = acc[:, :LANES] / acc[:, LANES:]
        for gi in range(per):
            g = st * per + gi
            o_ref[0, :, g * LANES:(g + 1) * LANES] = o[gi * tq:(gi + 1) * tq].astype(o_ref.dtype)


def gqa_attention(qkv, *, tq, tk, streams):
    b, s, _ = qkv.shape
    gw = GQA_GROUP * HEAD_DIM
    k_off = GQA_HEADS
    v_off = GQA_HEADS + GQA_KV_HEADS
    return pl.pallas_call(
        functools.partial(_gqa_attn_kernel, tk=tk, streams=streams),
        grid=(b, GQA_KV_HEADS, s // tq),
        in_specs=[
            pl.BlockSpec((1, tq, gw), lambda bb, n, i: (bb, i, n)),
            pl.BlockSpec((1, s, HEAD_DIM), lambda bb, n, i: (bb, 0, k_off + n)),
            pl.BlockSpec((1, s, HEAD_DIM), lambda bb, n, i: (bb, 0, v_off + n)),
        ],
        out_specs=pl.BlockSpec((1, tq, gw), lambda bb, n, i: (bb, i, n)),
        out_shape=jax.ShapeDtypeStruct((b, s, GQA_HEADS * HEAD_DIM), BF16),
        compiler_params=_params(("parallel", "parallel", "arbitrary")),
        name="gqa_attention",
    )(qkv, qkv, qkv)


def _dil_attn_kernel(q_ref, kp_ref, kc_ref, kn_ref, vp_ref, vc_ref, vn_ref, o_ref, lse_ref,
                     *, span, m_total):
    i = pl.program_id(2)
    tq = q_ref.shape[0]
    nk = tq + 2 * span
    row = lax.broadcasted_iota(jnp.int32, (tq, nk), 0)
    col = lax.broadcasted_iota(jnp.int32, (tq, nk), 1)
    key_idx = i * tq - span + col
    ok = (jnp.abs(col - span - row) <= span) & (key_idx >= 0) & (key_idx < m_total)
    lane = lax.broadcasted_iota(jnp.int32, (tq, LANES), 1)
    lse_tile = jnp.zeros((tq, LANES), F32)

    def window(p_ref, c_ref, n_ref, sl):
        return jnp.concatenate([p_ref[tq - span:, sl], c_ref[:, sl], n_ref[:span, sl]], axis=0)

    for h in range(DIL_HEADS):
        sl = slice(h * LANES, (h + 1) * LANES)
        k = window(kp_ref, kc_ref, kn_ref, sl)
        v = window(vp_ref, vc_ref, vn_ref, sl)
        s = lax.dot_general(q_ref[:, sl], k, NT_DIMS, preferred_element_type=F32)
        s = jnp.where(ok, s, NEG_INF)
        m = jnp.max(s, axis=-1, keepdims=True)
        p = jnp.exp(s - m)
        l = jnp.sum(p, axis=-1, keepdims=True)
        o = jnp.dot(p.astype(BF16), v, preferred_element_type=F32) / l
        o_ref[:, sl] = o.astype(o_ref.dtype)
        lse_tile = jnp.where(lane == h, m + jnp.log(l), lse_tile)
    lse_ref[...] = lse_tile


def dilated_attention(qkv, span, *, tq):
    b, dil, m_total, n = qkv.shape
    hw = DIL_HEADS * HEAD_DIM
    nb = m_total // tq

    def spec(col, shift):
        return pl.BlockSpec((None, None, tq, hw),
                            lambda bb, r, i: (bb, r, jnp.clip(i + shift, 0, nb - 1), col))

    return pl.pallas_call(
        functools.partial(_dil_attn_kernel, span=span, m_total=m_total),
        grid=(b, dil, nb),
        in_specs=[spec(0, 0), spec(1, -1), spec(1, 0), spec(1, 1), spec(2, -1), spec(2, 0), spec(2, 1)],
        out_specs=[
            pl.BlockSpec((None, None, tq, hw), lambda bb, r, i: (bb, r, i, 0)),
            pl.BlockSpec((None, None, tq, LANES), lambda bb, r, i: (bb, r, i, 0)),
        ],
        out_shape=[
            jax.ShapeDtypeStruct((b, dil, m_total, hw), BF16),
            jax.ShapeDtypeStruct((b, dil, m_total, LANES), F32),
        ],
        compiler_params=_params(("parallel", "parallel", "arbitrary")),
        name=f"dilated_attention_d{dil}",
    )(qkv, qkv, qkv, qkv, qkv, qkv, qkv)


def _dil_merge_kernel(o0_ref, o1_ref, o2_ref, l0_ref, l1_ref, l2_ref, w_ref, x_ref, out_ref, o_scr, l_scr):
    def to_tokens(ref, scr, g):
        dil, per, width = ref.shape
        for c in range(width // LANES):
            sl = slice(c * LANES, (c + 1) * LANES)
            if dil == 1:
                scr[g, c] = ref[0, :, sl].astype(F32)
            else:
                for r in range(dil):
                    scr[g, c, pl.ds(r, per, stride=dil), :] = ref[r, :, sl].astype(F32)

    for g, (o_ref, l_ref) in enumerate(((o0_ref, l0_ref), (o1_ref, l1_ref), (o2_ref, l2_ref))):
        to_tokens(o_ref, o_scr, g)
        to_tokens(l_ref, l_scr, g)
    lses = [l_scr[g, 0] for g in range(3)]
    mx = jnp.maximum(jnp.maximum(lses[0], lses[1]), lses[2])
    es = [jnp.exp(l - mx) for l in lses]
    den = es[0] + es[1] + es[2]
    ws = [e / den for e in es]
    parts = []
    for h in range(DIL_HEADS):
        acc = None
        for g in range(3):
            term = o_scr[g, h] * ws[g][:, h:h + 1]
            acc = term if acc is None else acc + term
        parts.append(acc.astype(BF16))
    merged = jnp.concatenate(parts, axis=1)
    out_ref[...] = x_ref[...] + jnp.dot(merged, w_ref[...], preferred_element_type=F32)


def dilated_merge_project(outs, lses, w_out, x2, batch, seq, *, tm):
    t, d = x2.shape
    hw = w_out.shape[0]
    ns = seq // tm

    def grouped(arr):
        _, dil, m_total, w = arr.shape
        per = tm // dil
        view = arr.reshape(batch, dil, ns, per, w)
        return view, pl.BlockSpec((None, dil, None, per, w), lambda i: (i // ns, 0, i % ns, 0, 0))

    views, specs = zip(*[grouped(a) for a in (*outs, *lses)])
    row = pl.BlockSpec((tm, d), lambda i: (i, 0))
    return pl.pallas_call(
        _dil_merge_kernel,
        grid=(t // tm,),
        in_specs=list(specs) + [pl.BlockSpec((hw, d), lambda i: (0, 0)), row],
        out_specs=row,
        out_shape=jax.ShapeDtypeStruct((t, d), F32),
        scratch_shapes=[pltpu.VMEM((3, hw // LANES, tm, LANES), F32), pltpu.VMEM((3, 1, tm, LANES), F32)],
        compiler_params=_params(("parallel",)),
        name="dilated_merge_project",
    )(*views, w_out, x2)


def _rglru_scan_kernel(xc_ref, xp_ref, xn_ref, cw_ref, cb_ref, wa_ref, ba_ref, wx_ref, bx_ref, lam_ref,
                       o_ref, a_scr, u_scr, h_scr, *, reverse, n_steps):
    s = pl.program_id(1)
    ts, nb, tc = xc_ref.shape
    chunk = (n_steps - 1 - s) if reverse else s

    @pl.when(s == 0)
    def _():
        h_scr[...] = jnp.zeros(h_scr.shape, F32)

    prev = jnp.where(chunk == 0, 0.0, xp_ref[...])
    nxt = jnp.where(chunk == n_steps - 1, 0.0, xn_ref[...])
    xe = jnp.concatenate([prev, xc_ref[...], nxt], axis=0)
    cw = cw_ref[...]
    xc = cb_ref[...].reshape(1, 1, tc) + xe[0:ts] * cw[0:1].reshape(1, 1, tc)
    for j in range(1, CONV_W):
        xc = xc + xe[j:j + ts] * cw[j:j + 1].reshape(1, 1, tc)
    xc2 = xc.reshape(ts * nb, tc)
    lam = lam_ref[...]
    softplus_neg_lam = jnp.maximum(-lam, 0.0) + jnp.log1p(jnp.exp(-jnp.abs(lam)))
    for kb in range(tc // RG_BLOCK_W):
        sl = slice(kb * RG_BLOCK_W, (kb + 1) * RG_BLOCK_W)
        xb = xc2[:, sl]
        xb16 = xb.astype(BF16)
        r = jax.nn.sigmoid(jnp.dot(xb16, wa_ref[kb], preferred_element_type=F32) + ba_ref[:, sl])
        gate = jax.nn.sigmoid(jnp.dot(xb16, wx_ref[kb], preferred_element_type=F32) + bx_ref[:, sl])
        log_a = (-RG_C) * r * softplus_neg_lam[:, sl]
        a = jnp.exp(log_a)
        u = jnp.sqrt(1.0 - a * a) * (gate * xb)
        a_scr[:, :, sl] = a.reshape(ts, nb, RG_BLOCK_W)
        u_scr[:, :, sl] = u.reshape(ts, nb, RG_BLOCK_W)

    def step(t, h):
        tt = (ts - 1 - t) if reverse else t
        h = a_scr[tt] * h + u_scr[tt]
        o_ref[tt] = h
        return h

    h_scr[...] = lax.fori_loop(0, ts, step, h_scr[...], unroll=8)


def rglru_scan(yx3, conv_w, conv_b, w_a, b_a, w_x, b_x, lam, *, reverse, ts, tc):
    seq, nb, two_c = yx3.shape
    c = two_c // 2
    nc, n_steps = c // tc, seq // ts
    kb = tc // RG_BLOCK_W

    def chunk_of(s):
        return (n_steps - 1 - s) if reverse else s

    vec = lambda rows: pl.BlockSpec((rows, tc), lambda ci, s: (0, ci))
    mat = pl.BlockSpec((kb, RG_BLOCK_W, RG_BLOCK_W), lambda ci, s: (ci, 0, 0))
    return pl.pallas_call(
        functools.partial(_rglru_scan_kernel, reverse=reverse, n_steps=n_steps),
        grid=(nc, n_steps),
        in_specs=[
            pl.BlockSpec((ts, nb, tc), lambda ci, s: (chunk_of(s), 0, nc + ci)),
            pl.BlockSpec((CONV_LEFT, nb, tc),
                         lambda ci, s: (jnp.maximum(chunk_of(s) * (ts // CONV_LEFT) - 1, 0), 0, nc + ci)),
            pl.BlockSpec((1, nb, tc),
                         lambda ci, s: (jnp.minimum((chunk_of(s) + 1) * ts, seq - 1), 0, nc + ci)),
            vec(CONV_W), vec(1), mat, vec(1), mat, vec(1), vec(1),
        ],
        out_specs=pl.BlockSpec((ts, nb, tc), lambda ci, s: (chunk_of(s), 0, ci)),
        out_shape=jax.ShapeDtypeStruct((seq, nb, c), F32),
        scratch_shapes=[
            pltpu.VMEM((ts, nb, tc), F32),
            pltpu.VMEM((ts, nb, tc), F32),
            pltpu.VMEM((nb, tc), F32),
        ],
        compiler_params=_params(("parallel", "arbitrary")),
        name="rglru_scan_bwd" if reverse else "rglru_scan_fwd",
    )(yx3, yx3, yx3, conv_w, conv_b, w_a, b_a, w_x, b_x, lam)


def _gelu_tanh(y):
    return y * (0.5 * (1.0 + jnp.tanh(0.7978845608028654 * (y + 0.044715 * (y * y * y)))))


def _rglru_out_kernel(y_ref, hf_ref, hb_ref, w_ref, x_ref, o_ref):
    gated = _gelu_tanh(y_ref[...]) * (hf_ref[...] + hb_ref[...])
    o_ref[...] = x_ref[...] + jnp.dot(gated.astype(BF16), w_ref[...], preferred_element_type=F32)


def rglru_out_project(yx, hf, hb, w_out, x2, batch, seq, *, tm):
    c, d = w_out.shape
    ns = seq // tm
    return pl.pallas_call(
        _rglru_out_kernel,
        grid=(batch, ns),
        in_specs=[
            pl.BlockSpec((tm, c), lambda b, s: (s, 2 * b)),
            pl.BlockSpec((tm, c), lambda b, s: (s, b)),
            pl.BlockSpec((tm, c), lambda b, s: (s, b)),
            pl.BlockSpec((c, d), lambda b, s: (0, 0)),
            pl.BlockSpec((tm, d), lambda b, s: (b * ns + s, 0)),
        ],
        out_specs=pl.BlockSpec((tm, d), lambda b, s: (b * ns + s, 0)),
        out_shape=jax.ShapeDtypeStruct(x2.shape, F32),
        compiler_params=_params(("parallel", "parallel")),
        name="rglru_out_project",
    )(yx, hf, hb, w_out, x2)


def _swiglu_step(h, wg_ref, wu_ref, wd_ref, acc_scr):
    gate = jnp.dot(h, wg_ref[...], preferred_element_type=F32)
    up = jnp.dot(h, wu_ref[...], preferred_element_type=F32)
    act = gate * jax.nn.sigmoid(gate) * up
    acc_scr[...] += jnp.dot(act.astype(BF16), wd_ref[...], preferred_element_type=F32)


def _ffn_kernel(x_ref, g_ref, wg_ref, wu_ref, wd_ref, o_ref, h_scr, acc_scr):
    f = pl.program_id(1)

    @pl.when(f == 0)
    def _():
        h_scr[...] = _rms(x_ref[...], g_ref[...]).astype(BF16)
        acc_scr[...] = jnp.zeros(acc_scr.shape, F32)

    _swiglu_step(h_scr[...], wg_ref, wu_ref, wd_ref, acc_scr)

    @pl.when(f == pl.num_programs(1) - 1)
    def _():
        o_ref[...] = x_ref[...] + acc_scr[...]


def swiglu_ffn(x2, gain, w_gate_up, w_down, *, tm, tf):
    t, d = x2.shape
    ff = w_down.shape[0]
    nf = ff // tf
    return pl.pallas_call(
        _ffn_kernel,
        grid=(t // tm, nf),
        in_specs=[
            pl.BlockSpec((tm, d), lambda i, f: (i, 0)),
            pl.BlockSpec((1, d), lambda i, f: (0, 0)),
            pl.BlockSpec((d, tf), lambda i, f: (0, f)),
            pl.BlockSpec((d, tf), lambda i, f: (0, nf + f)),
            pl.BlockSpec((tf, d), lambda i, f: (f, 0)),
        ],
        out_specs=pl.BlockSpec((tm, d), lambda i, f: (i, 0)),
        out_shape=jax.ShapeDtypeStruct((t, d), F32),
        scratch_shapes=[pltpu.VMEM((tm, d), BF16), pltpu.VMEM((tm, d), F32)],
        compiler_params=_params(("parallel", "arbitrary")),
        name="dense_ffn",
    )(x2, gain, w_gate_up, w_gate_up, w_down)


SUBLANES = 8


def _copy_token(src_hbm, dst, src_tok, dst_tok, sem):
    return pltpu.make_async_copy(src_hbm.at[pl.ds(src_tok * SUBLANES, SUBLANES)],
                                 dst.at[pl.ds(dst_tok * SUBLANES, SUBLANES)], sem)


def _moe_scatter_kernel(zero_ref, pos_ref, x_ref, o_hbm, stage, zeros_scr, sem, *, tm):
    p = pl.program_id(0)
    n = pl.num_programs(0)
    slot = p % 2
    chunks = x_ref.shape[1] // LANES
    block_rows = tm * SUBLANES

    def zero_copy(j):
        return pltpu.make_async_copy(zeros_scr, o_hbm.at[pl.ds(zero_ref[j] * SUBLANES, block_rows)], sem.at[2])

    @pl.when(p == 0)
    def _():
        zeros_scr[...] = jnp.zeros(zeros_scr.shape, F32)
        for j in range(2 * N_EXPERTS):
            zero_copy(j).start()
            zero_copy(j).wait()

    def slot_wait(s):
        rows = pl.ds(0, 2 * block_rows)
        pltpu.make_async_copy(o_hbm.at[rows], o_hbm.at[rows], sem.at[s]).wait()

    @pl.when(p >= 2)
    def _():
        slot_wait(slot)

    for c in range(chunks):
        stage[slot, pl.ds(c, tm, stride=SUBLANES), :] = x_ref[:, c * LANES:(c + 1) * LANES]

    def issue(r, carry):
        src = stage.at[slot, pl.ds(r * SUBLANES, SUBLANES)]
        for k in range(TOP_K):
            dst = o_hbm.at[pl.ds(pos_ref[0, k * tm + r] * SUBLANES, SUBLANES)]
            pltpu.make_async_copy(src, dst, sem.at[slot]).start()
        return carry
    lax.fori_loop(0, tm, issue, 0, unroll=8)

    @pl.when(p == n - 1)
    def _():
        slot_wait(slot)

        @pl.when(n >= 2)
        def _():
            slot_wait(1 - slot)


def moe_scatter(x2, pos, zero_starts, n_tiles, *, tm):
    t, d = x2.shape
    grid_spec = pltpu.PrefetchScalarGridSpec(
        num_scalar_prefetch=1,
        grid=(t // tm,),
        in_specs=[
            pl.BlockSpec((None, 1, TOP_K * tm), lambda p, z: (p, 0, 0), memory_space=pltpu.SMEM),
            pl.BlockSpec((tm, d), lambda p, z: (p, 0)),
        ],
        out_specs=pl.BlockSpec(memory_space=pl.ANY),
        scratch_shapes=[pltpu.VMEM((2, tm * SUBLANES, LANES), F32), pltpu.VMEM((tm * SUBLANES, LANES), F32),
                        pltpu.SemaphoreType.DMA((3,))],
    )
    return pl.pallas_call(
        functools.partial(_moe_scatter_kernel, tm=tm),
        grid_spec=grid_spec,
        out_shape=jax.ShapeDtypeStruct(((n_tiles + 1) * tm * SUBLANES, LANES), F32),
        compiler_params=_params(("arbitrary",)),
        name="moe_scatter",
    )(zero_starts, pos, x2)


def _moe_expert_kernel(te_ref, meta_ref, xs_ref, g_ref, wg_ref, wu_ref, wd_ref, y_ref, h_scr, acc_scr):
    p = pl.program_id(0)
    f = pl.program_id(1)
    tm = acc_scr.shape[0]
    chunks = acc_scr.shape[1] // LANES
    used = p < meta_ref[0]

    @pl.when(used & (f == 0))
    def _():
        x = jnp.concatenate([xs_ref[pl.ds(c, tm, stride=SUBLANES), :] for c in range(chunks)], axis=1)
        h_scr[...] = _rms(x, g_ref[...]).astype(BF16)
        acc_scr[...] = jnp.zeros(acc_scr.shape, F32)

    @pl.when(used)
    def _():
        _swiglu_step(h_scr[...], wg_ref, wu_ref, wd_ref, acc_scr)

    @pl.when(f == pl.num_programs(1) - 1)
    def _():
        y = jnp.where(used, acc_scr[...], 0.0)
        for c in range(chunks):
            y_ref[pl.ds(c, tm, stride=SUBLANES), :] = y[:, c * LANES:(c + 1) * LANES]


def moe_experts(xs, gain, w_gate_up, w_down, tile_expert, meta, n_tiles, *, tm, tf):
    n_e, ff, d = w_down.shape
    nf = ff // tf

    def f_eff(p, f, meta_ref):
        return jnp.where(p < meta_ref[0], f, nf - 1)

    grid_spec = pltpu.PrefetchScalarGridSpec(
        num_scalar_prefetch=2,
        grid=(n_tiles, nf),
        in_specs=[
            pl.BlockSpec((tm * SUBLANES, LANES), lambda p, f, te, m: (p, 0)),
            pl.BlockSpec((1, d), lambda p, f, te, m: (0, 0)),
            pl.BlockSpec((None, d, tf), lambda p, f, te, m: (te[p], 0, f_eff(p, f, m))),
            pl.BlockSpec((None, d, tf), lambda p, f, te, m: (te[p], 0, nf + f_eff(p, f, m))),
            pl.BlockSpec((None, tf, d), lambda p, f, te, m: (te[p], f_eff(p, f, m), 0)),
        ],
        out_specs=pl.BlockSpec((tm * SUBLANES, LANES), lambda p, f, te, m: (p, 0)),
        scratch_shapes=[pltpu.VMEM((tm, d), BF16), pltpu.VMEM((tm, d), F32)],
    )
    return pl.pallas_call(
        _moe_expert_kernel,
        grid_spec=grid_spec,
        out_shape=jax.ShapeDtypeStruct((n_tiles * tm * SUBLANES, LANES), F32),
        compiler_params=_params(("parallel", "arbitrary")),
        name="moe_experts",
    )(tile_expert, meta, xs, gain, w_gate_up, w_gate_up, w_down)


def _moe_combine_kernel(cur_ref, nxt_ref, x_ref, route_ref, y_hbm, o_ref, buf, sem, *, tm):
    p = pl.program_id(0)
    n = pl.num_programs(0)
    chunks = x_ref.shape[1] // LANES

    def issue(idx_ref, slot):
        def body(r, carry):
            _copy_token(y_hbm, buf.at[slot], idx_ref[0, r], r, sem.at[slot]).start()
            return carry
        lax.fori_loop(0, 2 * tm, body, 0, unroll=8)

    @pl.when(p == 0)
    def _():
        issue(cur_ref, 0)

    @pl.when(p + 1 < n)
    def _():
        issue(nxt_ref, (p + 1) % 2)

    slot = p % 2
    pltpu.make_async_copy(y_hbm.at[pl.ds(0, 2 * tm * SUBLANES)], buf.at[slot], sem.at[slot]).wait()
    route = route_ref[...]
    g1 = route[:, 2:3]
    g2 = route[:, 3:4]
    for c in range(chunks):
        first = buf[slot, pl.ds(c, tm, stride=SUBLANES), :]
        second = buf[slot, pl.ds(tm * SUBLANES + c, tm, stride=SUBLANES), :]
        sl = slice(c * LANES, (c + 1) * LANES)
        o_ref[:, sl] = x_ref[:, sl] + (g1 * first + g2 * second)


def moe_combine(x2, y_tiles, pos, route, *, tm):
    t, d = x2.shape
    n = t // tm
    idx_spec = lambda shift: pl.BlockSpec((None, 1, 2 * tm), lambda p: (jnp.minimum(p + shift, n - 1), 0, 0),
                                          memory_space=pltpu.SMEM)
    return pl.pallas_call(
        functools.partial(_moe_combine_kernel, tm=tm),
        grid=(n,),
        in_specs=[idx_spec(0), idx_spec(1), pl.BlockSpec((tm, d), lambda p: (p, 0)),
                  pl.BlockSpec((tm, LANES), lambda p: (p, 0)), pl.BlockSpec(memory_space=pl.ANY)],
        out_specs=pl.BlockSpec((tm, d), lambda p: (p, 0)),
        out_shape=jax.ShapeDtypeStruct((t, d), F32),
        scratch_shapes=[pltpu.VMEM((2, 2 * tm * SUBLANES, LANES), F32), pltpu.SemaphoreType.DMA((2,))],
        compiler_params=_params(("arbitrary",)),
        name="moe_combine",
    )(pos, pos, x2, route, y_tiles)


def _moe_plan(route, tm):
    t = route.shape[0]
    experts = route[:, :TOP_K].astype(jnp.int32)
    onehot = (experts[:, :, None] == jnp.arange(N_EXPERTS)[None, None, :]).astype(jnp.int32).sum(axis=1)
    incl = jnp.cumsum(onehot, axis=0)
    rank = incl - onehot
    counts = incl[-1]
    tiles_per = (counts + tm - 1) // tm
    tile_end = jnp.cumsum(tiles_per)
    row_start = (tile_end - tiles_per) * tm
    dest = row_start[experts] + jnp.take_along_axis(rank, experts, axis=1)
    n_tiles = (TOP_K * t) // tm + N_EXPERTS
    tile_expert = jnp.minimum(jnp.searchsorted(tile_end, jnp.arange(n_tiles), side="right"),
                              N_EXPERTS - 1).astype(jnp.int32)
    n_used = tile_end[-1]
    tail = jnp.minimum((n_used + jnp.arange(N_EXPERTS)) * tm, n_tiles * tm)
    zero_starts = jnp.concatenate([row_start + counts, tail]).astype(jnp.int32)
    pos = jnp.concatenate([dest[:, k].reshape(t // tm, tm) for k in range(TOP_K)], axis=1)
    return (pos.reshape(t // tm, 1, TOP_K * tm).astype(jnp.int32), zero_starts, tile_expert,
            n_used.reshape(1).astype(jnp.int32), n_tiles)


def _router_kernel(x_ref, g_ref, r_ref, b_ref, c_ref):
    h = _rms(x_ref[...], g_ref[...])
    logits = jnp.dot(h, r_ref[...], preferred_element_type=F32, precision=lax.Precision.HIGHEST)
    logits = logits + b_ref[...]
    lane = lax.broadcasted_iota(jnp.int32, logits.shape, 1)
    logits = jnp.where(lane < N_EXPERTS, logits, NEG_INF)
    v1 = jnp.max(logits, axis=-1, keepdims=True)
    i1 = jnp.min(jnp.where(logits == v1, lane, LANES), axis=-1, keepdims=True)
    rest = jnp.where(lane == i1, NEG_INF, logits)
    v2 = jnp.max(rest, axis=-1, keepdims=True)
    i2 = jnp.min(jnp.where(rest == v2, lane, LANES), axis=-1, keepdims=True)
    e2 = jnp.exp(v2 - v1)
    g1 = 1.0 / (1.0 + e2)
    g2 = e2 / (1.0 + e2)
    c_ref[...] = jnp.where(lane == 0, i1.astype(F32), jnp.where(lane == 1, i2.astype(F32),
                           jnp.where(lane == 2, g1, jnp.where(lane == 3, g2, 0.0))))


def moe_router(x2, gain, router, router_b, *, tm):
    t, d = x2.shape
    r_pad = jnp.zeros((d, LANES), F32).at[:, :N_EXPERTS].set(router)
    b_pad = jnp.zeros((1, LANES), F32).at[0, :N_EXPERTS].set(router_b)
    return pl.pallas_call(
        _router_kernel,
        grid=(t // tm,),
        in_specs=[
            pl.BlockSpec((tm, d), lambda i: (i, 0)),
            pl.BlockSpec((1, d), lambda i: (0, 0)),
            pl.BlockSpec((d, LANES), lambda i: (0, 0)),
            pl.BlockSpec((1, LANES), lambda i: (0, 0)),
        ],
        out_specs=pl.BlockSpec((tm, LANES), lambda i: (i, 0)),
        out_shape=jax.ShapeDtypeStruct((t, LANES), F32),
        compiler_params=_params(("parallel",)),
        name="moe_router",
    )(x2, gain, r_pad, b_pad)


def _inv_freq(dim, theta):
    return theta ** (-jnp.arange(0, dim, 2, dtype=F32) / dim)


def _axial_tables(seq):
    half = HEAD_DIM // 2
    t = jnp.arange(seq)
    rows = (t // GRID_W).astype(F32)
    cols = (t % GRID_W).astype(F32)
    freq = _inv_freq(half, AXIAL_THETA)
    ang_r = rows[:, None] * freq[None, :]
    ang_c = cols[:, None] * freq[None, :]
    ang = jnp.concatenate([ang_r, ang_r, ang_c, ang_c], axis=-1)
    sign = jnp.tile(jnp.concatenate([-jnp.ones(half // 2, F32), jnp.ones(half // 2, F32)]), 2)
    return jnp.cos(ang), jnp.sin(ang) * sign[None, :]


def _partial_tables(seq):
    pos = jnp.arange(seq, dtype=F32)
    freq = _inv_freq(ROT_DIMS, ROPE_THETA)
    ang = pos[:, None] * freq[None, :]
    cos = jnp.concatenate([jnp.cos(ang), jnp.cos(ang), jnp.ones((seq, HEAD_DIM - ROT_DIMS), F32)], axis=-1)
    sin = jnp.concatenate([-jnp.sin(ang), jnp.sin(ang), jnp.zeros((seq, HEAD_DIM - ROT_DIMS), F32)], axis=-1)
    return cos, sin


def _row(v):
    return v.reshape(1, -1).astype(F32)


def _rglru_layer(x2, batch, seq, norm, w_in, conv_w, conv_b, w_a, b_a, w_x, b_x, lam, w_out):
    c = w_out.shape[0]
    yx = norm_matmul_time_major(x2, _row(norm), w_in.astype(BF16), batch, seq,
                                tm=512, tn=1024, out_dtype=F32)
    yx3 = yx.reshape(seq, batch, 2 * c)
    hs = []
    for direction in range(2):
        hs.append(rglru_scan(
            yx3, conv_w, _row(conv_b), w_a[direction].astype(BF16), _row(b_a[direction]),
            w_x[direction].astype(BF16), _row(b_x[direction]), _row(lam[direction]),
            reverse=direction == 1, ts=256, tc=512).reshape(seq, batch * c))
    return rglru_out_project(yx, hs[0], hs[1], w_out.astype(BF16), x2, batch, seq, tm=512)


def _gqa_layer(x2, batch, seq, norm, w_qkv, q_gain, k_gain, w_out):
    nq = GQA_HEADS * HEAD_DIM
    nkv = GQA_KV_HEADS * HEAD_DIM
    tn = nkv
    modes = jnp.array([1] * (nq // tn) + [2] * (nkv // tn) + [0] * (nkv // tn), jnp.int32)
    cos, sin = _axial_tables(seq)
    qkv = qkv_project(x2, _row(norm), w_qkv.astype(BF16), modes, cos, sin, _row(q_gain), _row(k_gain),
                      batch, seq, tm=512, tn=tn, half=HEAD_DIM // 4, scale=HEAD_DIM ** -0.5, dil=1)
    o = gqa_attention(qkv.reshape(batch, seq, nq + 2 * nkv), tq=256, tk=512, streams=2)
    return matmul_residual(o.reshape(batch * seq, nq), w_out.astype(BF16), x2, tm=512)


def _dilated_layer(x2, batch, seq, norm, w_qkv, q_gain, k_gain, w_out):
    hw = DIL_HEADS * HEAD_DIM
    modes = jnp.array([1, 2, 0], jnp.int32)
    cos, sin = _partial_tables(seq)
    w16 = w_qkv.astype(BF16)
    outs, lses = [], []
    for g, (window, dil) in enumerate(DIL_PATTERNS):
        span = window // (2 * dil)
        assert seq % (dil * span) == 0
        qkv = qkv_project(x2, _row(norm), w16[:, g * 3 * hw:(g + 1) * 3 * hw], modes, cos, sin,
                          _row(q_gain), _row(k_gain), batch, seq,
                          tm=512, tn=hw, half=ROT_DIMS // 2, scale=HEAD_DIM ** -0.5, dil=dil)
        o, lse = dilated_attention(qkv, span, tq=2 * span)
        outs.append(o)
        lses.append(lse)
    return dilated_merge_project(outs, lses, w_out.astype(BF16), x2, batch, seq, tm=512)


def _dense_ffn_layer(x2, norm, w_gate_up, w_down):
    return swiglu_ffn(x2, _row(norm), w_gate_up.astype(BF16), w_down.astype(BF16), tm=1024, tf=512)


def _moe_layer(x2, norm, router, router_b, w_gate_up, w_down):
    tm = 512
    t, d = x2.shape
    route = moe_router(x2, _row(norm), router, router_b, tm=512)
    pos, zero_starts, tile_expert, meta, n_tiles = _moe_plan(route, tm)
    xs = moe_scatter(x2, pos, zero_starts, n_tiles, tm=tm)
    ys = moe_experts(xs, _row(norm), w_gate_up.astype(BF16), w_down.astype(BF16), tile_expert, meta, n_tiles,
                     tm=tm, tf=512)
    return moe_combine(x2, ys, pos, route, tm=tm)


def kernel(x, l0_attn_norm, l0_rg_w_in, l0_rg_conv_w, l0_rg_conv_b, l0_rg_w_a, l0_rg_b_a, l0_rg_w_x, l0_rg_b_x, l0_rg_lambda, l0_rg_w_out, l0_ffn_norm, l0_ffn_w_gate_up, l0_ffn_w_down, l1_attn_norm, l1_gqa_w_qkv, l1_gqa_q_norm, l1_gqa_k_norm, l1_gqa_w_out, l1_ffn_norm, l1_moe_router, l1_moe_router_b, l1_moe_w_gate_up, l1_moe_w_down, l2_attn_norm, l2_dil_w_qkv, l2_dil_q_norm, l2_dil_k_norm, l2_dil_w_out, l2_ffn_norm, l2_ffn_w_gate_up, l2_ffn_w_down, l3_attn_norm, l3_rg_w_in, l3_rg_conv_w, l3_rg_conv_b, l3_rg_w_a, l3_rg_b_a, l3_rg_w_x, l3_rg_b_x, l3_rg_lambda, l3_rg_w_out, l3_ffn_norm, l3_moe_router, l3_moe_router_b, l3_moe_w_gate_up, l3_moe_w_down):
    batch, seq, d = x.shape
    x2 = x.reshape(batch * seq, d)
    x2 = _rglru_layer(x2, batch, seq, l0_attn_norm, l0_rg_w_in, l0_rg_conv_w, l0_rg_conv_b, l0_rg_w_a,
                      l0_rg_b_a, l0_rg_w_x, l0_rg_b_x, l0_rg_lambda, l0_rg_w_out)
    x2 = _dense_ffn_layer(x2, l0_ffn_norm, l0_ffn_w_gate_up, l0_ffn_w_down)
    x2 = _gqa_layer(x2, batch, seq, l1_attn_norm, l1_gqa_w_qkv, l1_gqa_q_norm, l1_gqa_k_norm, l1_gqa_w_out)
    x2 = _moe_layer(x2, l1_ffn_norm, l1_moe_router, l1_moe_router_b, l1_moe_w_gate_up, l1_moe_w_down)
    x2 = _dilated_layer(x2, batch, seq, l2_attn_norm, l2_dil_w_qkv, l2_dil_q_norm, l2_dil_k_norm, l2_dil_w_out)
    x2 = _dense_ffn_layer(x2, l2_ffn_norm, l2_ffn_w_gate_up, l2_ffn_w_down)
    x2 = _rglru_layer(x2, batch, seq, l3_attn_norm, l3_rg_w_in, l3_rg_conv_w, l3_rg_conv_b, l3_rg_w_a,
                      l3_rg_b_a, l3_rg_w_x, l3_rg_b_x, l3_rg_lambda, l3_rg_w_out)
    x2 = _moe_layer(x2, l3_ffn_norm, l3_moe_router, l3_moe_router_b, l3_moe_w_gate_up, l3_moe_w_down)
    return x2.reshape(batch, seq, d)
```

```python
import functools

import jax
import jax.numpy as jnp
import numpy as np
from jax import lax
from jax.experimental import pallas as pl
from jax.experimental.pallas import tpu as pltpu

F32 = jnp.float32
BF16 = jnp.bfloat16

EPS = 1e-6
NEG_INF = -1e30
GRID_W = 64
HEAD_DIM = 128
LANES = 128
RG_BLOCK_W = 128
CONV_W = 4
CONV_LEFT = 2
RG_C = 8.0
GQA_HEADS = 8
GQA_KV_HEADS = 2
GQA_GROUP = GQA_HEADS // GQA_KV_HEADS
AXIAL_THETA = 10000.0
DIL_PATTERNS = ((128, 1), (512, 4), (2048, 16))
DIL_HEADS = 8
ROPE_THETA = 500000.0
ROT_DIMS = HEAD_DIM // 4
N_EXPERTS = 8
TOP_K = 2
VMEM_LIMIT = 56 * 1024 * 1024

NT_DIMS = (((1,), (1,)), ((), ()))
LOG2_E = 1.4426950408889634


def _params(sem, vmem=VMEM_LIMIT):
    return pltpu.CompilerParams(dimension_semantics=sem, vmem_limit_bytes=vmem)


def _rms(x, g):
    return x * lax.rsqrt(jnp.mean(x * x, axis=-1, keepdims=True) + EPS) * g


def _norm_matmul_kernel(x_ref, g_ref, w_ref, o_ref, h_scr):
    @pl.when(pl.program_id(2) == 0)
    def _():
        h_scr[...] = _rms(x_ref[...], g_ref[...]).astype(BF16)

    o_ref[...] = jnp.dot(h_scr[...], w_ref[...], preferred_element_type=F32).astype(o_ref.dtype)


def norm_matmul_time_major(x2, gain, w, batch, seq, *, tm, tn, out_dtype):
    d = x2.shape[1]
    n = w.shape[1]
    ns, nj = seq // tm, n // tn
    return pl.pallas_call(
        _norm_matmul_kernel,
        grid=(batch, ns, nj),
        in_specs=[
            pl.BlockSpec((tm, d), lambda b, s, j: (b * ns + s, 0)),
            pl.BlockSpec((1, d), lambda b, s, j: (0, 0)),
            pl.BlockSpec((d, tn), lambda b, s, j: (0, j)),
        ],
        out_specs=pl.BlockSpec((tm, tn), lambda b, s, j: (s, b * nj + j)),
        out_shape=jax.ShapeDtypeStruct((seq, batch * n), out_dtype),
        scratch_shapes=[pltpu.VMEM((tm, d), BF16)],
        compiler_params=_params(("parallel", "parallel", "arbitrary")),
        name="norm_matmul_tm",
    )(x2, gain, w)


ROPE_SHIFT = LANES // 2


def _qkv_kernel(modes_ref, x_ref, g_ref, w_ref, cos_ref, sin_ref, qg_ref, kg_ref, o_ref, h_scr, *stage,
                scale, dil):
    j = pl.program_id(1)
    tm = x_ref.shape[0]
    per = tm // dil

    @pl.when(j == 0)
    def _():
        hn = _rms(x_ref[...], g_ref[...])
        if dil == 1:
            h_scr[...] = hn.astype(BF16)
        else:
            stage_scr, = stage
            for c in range(hn.shape[1] // LANES):
                sl = slice(c * LANES, (c + 1) * LANES)
                stage_scr[c] = hn[:, sl]
                for r in range(dil):
                    h_scr[r * per:(r + 1) * per, sl] = stage_scr[c, pl.ds(r, per, stride=dil), :].astype(BF16)

    mode = modes_ref[j]
    heads = w_ref.shape[1] // LANES

    def emit(stage1, stage2):
        h = h_scr[...]
        pairs = [(c0, min(2, heads - c0)) for c0 in range(0, heads, 2)]
        accs = [jnp.dot(h, w_ref[:, c0 * LANES:(c0 + n) * LANES], preferred_element_type=F32) for c0, n in pairs]
        mids = [stage1(a) for a in accs]
        for (c0, n), a, mid in zip(pairs, accs, mids):
            a = stage2(a, mid)
            for c in range(n):
                sl = slice((c0 + c) * LANES, (c0 + c + 1) * LANES)
                o_ref[:, :, sl] = a[:, c * LANES:(c + 1) * LANES].reshape(dil, per, LANES).astype(o_ref.dtype)

    def roped(gain, sc):
        gain_rolled = pltpu.roll(gain, ROPE_SHIFT, 1)
        c1 = cos_ref[...] * (gain * sc)
        s1 = sin_ref[...] * (gain_rolled * sc)
        width = min(2, heads) * LANES
        row = lax.broadcasted_iota(jnp.int32, (width, width), 0)
        col = lax.broadcasted_iota(jnp.int32, (width, width), 1)
        head_ones = (row // LANES == col // LANES).astype(BF16)

        def sum_squares(a):
            return jnp.dot((a * a).astype(BF16), head_ones[:a.shape[1], :a.shape[1]], preferred_element_type=F32)

        def rotate(a, sumsq):
            inv = lax.rsqrt(sumsq * (1.0 / HEAD_DIM) + EPS)
            parts = []
            for c in range(a.shape[1] // LANES):
                ac = a[:, c * LANES:(c + 1) * LANES]
                parts.append(ac * c1 + pltpu.roll(ac, ROPE_SHIFT, 1) * s1)
            return jnp.concatenate(parts, axis=1) * inv
        emit(sum_squares, rotate)

    @pl.when(mode == 0)
    def _():
        emit(lambda a: None, lambda a, mid: a)

    @pl.when(mode == 1)
    def _():
        roped(qg_ref[...], scale)

    @pl.when(mode == 2)
    def _():
        roped(kg_ref[...], 1.0)


def _tile_residue_major(table, dil, tm):
    s, w = table.shape
    return table.reshape(s // tm, tm // dil, dil, w).transpose(0, 2, 1, 3).reshape(s, w)


def qkv_project(x2, gain, w, modes, cos, sin, q_gain, k_gain, batch, seq, *, tm, tn, scale, dil):
    t, d = x2.shape
    n = w.shape[1]
    ns = seq // tm
    per = tm // dil
    cos_r = _tile_residue_major(cos, dil, tm)
    sin_r = _tile_residue_major(sin, dil, tm)
    table_spec = pl.BlockSpec((tm, LANES), lambda i, j, m: (i % ns, 0))
    grid_spec = pltpu.PrefetchScalarGridSpec(
        num_scalar_prefetch=1,
        grid=(t // tm, n // tn),
        in_specs=[
            pl.BlockSpec((tm, d), lambda i, j, m: (i, 0)),
            pl.BlockSpec((1, d), lambda i, j, m: (0, 0)),
            pl.BlockSpec((d, tn), lambda i, j, m: (0, j)),
            table_spec,
            table_spec,
            pl.BlockSpec((1, LANES), lambda i, j, m: (0, 0)),
            pl.BlockSpec((1, LANES), lambda i, j, m: (0, 0)),
        ],
        out_specs=pl.BlockSpec((None, dil, None, per, tn), lambda i, j, m: (i // ns, 0, i % ns, 0, j)),
        scratch_shapes=[pltpu.VMEM((tm, d), BF16)]
        + ([pltpu.VMEM((d // LANES, tm, LANES), F32)] if dil > 1 else []),
    )
    out = pl.pallas_call(
        functools.partial(_qkv_kernel, scale=scale, dil=dil),
        grid_spec=grid_spec,
        out_shape=jax.ShapeDtypeStruct((batch, dil, ns, per, n), BF16),
        compiler_params=_params(("parallel", "arbitrary")),
        name=f"qkv_project_d{dil}",
    )(modes, x2, gain, w, cos_r, sin_r, q_gain, k_gain)
    return out.reshape(batch, dil, seq // dil, n)


def _matmul_residual_kernel(a_ref, w_ref, x_ref, o_ref):
    o_ref[...] = x_ref[...] + jnp.dot(a_ref[...], w_ref[...], preferred_element_type=F32)


def matmul_residual(a, w, x2, *, tm):
    t, k = a.shape
    n = w.shape[1]
    return pl.pallas_call(
        _matmul_residual_kernel,
        grid=(t // tm,),
        in_specs=[
            pl.BlockSpec((tm, k), lambda i: (i, 0)),
            pl.BlockSpec((k, n), lambda i: (0, 0)),
            pl.BlockSpec((tm, n), lambda i: (i, 0)),
        ],
        out_specs=pl.BlockSpec((tm, n), lambda i: (i, 0)),
        out_shape=jax.ShapeDtypeStruct((t, n), F32),
        compiler_params=_params(("parallel",)),
        name="matmul_residual",
    )(a, w, x2)


def _gqa_attn_kernel(q_ref, k_ref, v_ref, o_ref, *, tk, streams):
    tq = q_ref.shape[1]
    seq = k_ref.shape[1]
    per = GQA_GROUP // streams
    qs = [jnp.concatenate([q_ref[0, :, g * LANES:(g + 1) * LANES] for g in range(st * per, (st + 1) * per)],
                          axis=0) for st in range(streams)]
    rows = per * tq
    ones = jnp.ones((tk, LANES), BF16)

    def body(c, carry):
        start = pl.multiple_of(c * tk, tk)
        k = k_ref[0, pl.ds(start, tk), :]
        v1 = jnp.concatenate([v_ref[0, pl.ds(start, tk), :], ones], axis=1)
        new = []
        for st in range(streams):
            m_prev, acc = carry[st]
            s = lax.dot_general(qs[st], k, NT_DIMS, preferred_element_type=F32)
            m_new = jnp.maximum(m_prev, jnp.max(s, axis=-1, keepdims=True))
            alpha = jnp.exp2(m_prev - m_new)
            p = jnp.exp2(s - m_new).astype(BF16)
            acc = alpha * acc + jnp.dot(p, v1, preferred_element_type=F32)
            new.append((m_new, acc))
        return tuple(new)

    init = tuple((jnp.full((rows, 1), NEG_INF, F32), jnp.zeros((rows, 2 * LANES), F32)) for _ in range(streams))
    final = lax.fori_loop(0, seq // tk, body, init, unroll=8)
    for st in range(streams):
        acc = final[st][1]
        o = acc[:, :LANES] / acc[:, LANES:]
        for gi in range(per):
            g = st * per + gi
            o_ref[0, :, g * LANES:(g + 1) * LANES] = o[gi * tq:(gi + 1) * tq].astype(o_ref.dtype)


def gqa_attention(qkv, *, tq, tk, streams):
    b, s, _ = qkv.shape
    gw = GQA_GROUP * HEAD_DIM
    k_off = GQA_HEADS
    v_off = GQA_HEADS + GQA_KV_HEADS
    return pl.pallas_call(
        functools.partial(_gqa_attn_kernel, tk=tk, streams=streams),
        grid=(b, GQA_KV_HEADS, s // tq),
        in_specs=[
            pl.BlockSpec((1, tq, gw), lambda bb, n, i: (bb, i, n)),
            pl.BlockSpec((1, s, HEAD_DIM), lambda bb, n, i: (bb, 0, k_off + n)),
            pl.BlockSpec((1, s, HEAD_DIM), lambda bb, n, i: (bb, 0, v_off + n)),
        ],
        out_specs=pl.BlockSpec((1, tq, gw), lambda bb, n, i: (bb, i, n)),
        out_shape=jax.ShapeDtypeStruct((b, s, GQA_HEADS * HEAD_DIM), BF16),
        compiler_params=_params(("parallel", "parallel", "arbitrary")),
        name="gqa_attention",
    )(qkv, qkv, qkv)


def _dil_attn_kernel(q_ref, kp_ref, kc_ref, kn_ref, vp_ref, vc_ref, vn_ref, o_ref, lse_ref,
                     *, span, m_total):
    i = pl.program_id(2)
    tq = q_ref.shape[0]
    nk = tq + 2 * span
    row = lax.broadcasted_iota(jnp.int32, (tq, nk), 0)
    col = lax.broadcasted_iota(jnp.int32, (tq, nk), 1)
    key_idx = i * tq - span + col
    ok = (jnp.abs(col - span - row) <= span) & (key_idx >= 0) & (key_idx < m_total)
    lane = lax.broadcasted_iota(jnp.int32, (tq, LANES), 1)
    lse_tile = jnp.zeros((tq, LANES), F32)

    def window(p_ref, c_ref, n_ref, sl):
        return jnp.concatenate([p_ref[tq - span:, sl], c_ref[:, sl], n_ref[:span, sl]], axis=0)

    for h in range(DIL_HEADS):
        sl = slice(h * LANES, (h + 1) * LANES)
        k = window(kp_ref, kc_ref, kn_ref, sl)
        v = window(vp_ref, vc_ref, vn_ref, sl)
        s = lax.dot_general(q_ref[:, sl], k, NT_DIMS, preferred_element_type=F32)
        s = jnp.where(ok, s, NEG_INF)
        m = jnp.max(s, axis=-1, keepdims=True)
        p = jnp.exp(s - m)
        l = jnp.sum(p, axis=-1, keepdims=True)
        o = jnp.dot(p.astype(BF16), v, preferred_element_type=F32) / l
        o_ref[:, sl] = o.astype(o_ref.dtype)
        lse_tile = jnp.where(lane == h, m + jnp.log(l), lse_tile)
    lse_ref[...] = lse_tile


def dilated_attention(qkv, span, *, tq):
    b, dil, m_total, n = qkv.shape
    hw = DIL_HEADS * HEAD_DIM
    nb = m_total // tq

    def spec(col, shift):
        return pl.BlockSpec((None, None, tq, hw),
                            lambda bb, r, i: (bb, r, jnp.clip(i + shift, 0, nb - 1), col))

    return pl.pallas_call(
        functools.partial(_dil_attn_kernel, span=span, m_total=m_total),
        grid=(b, dil, nb),
        in_specs=[spec(0, 0), spec(1, -1), spec(1, 0), spec(1, 1), spec(2, -1), spec(2, 0), spec(2, 1)],
        out_specs=[
            pl.BlockSpec((None, None, tq, hw), lambda bb, r, i: (bb, r, i, 0)),
            pl.BlockSpec((None, None, tq, LANES), lambda bb, r, i: (bb, r, i, 0)),
        ],
        out_shape=[
            jax.ShapeDtypeStruct((b, dil, m_total, hw), BF16),
            jax.ShapeDtypeStruct((b, dil, m_total, LANES), F32),
        ],
        compiler_params=_params(("parallel", "parallel", "arbitrary")),
        name=f"dilated_attention_d{dil}",
    )(qkv, qkv, qkv, qkv, qkv, qkv, qkv)


def _dil_merge_kernel(o0_ref, o1_ref, o2_ref, l0_ref, l1_ref, l2_ref, w_ref, x_ref, out_ref, o_scr, l_scr):
    def to_tokens(ref, scr, g):
        dil, per, width = ref.shape
        for c in range(width // LANES):
            sl = slice(c * LANES, (c + 1) * LANES)
            if dil == 1:
                scr[g, c] = ref[0, :, sl].astype(F32)
            else:
                for r in range(dil):
                    scr[g, c, pl.ds(r, per, stride=dil), :] = ref[r, :, sl].astype(F32)

    for g, (o_ref, l_ref) in enumerate(((o0_ref, l0_ref), (o1_ref, l1_ref), (o2_ref, l2_ref))):
        to_tokens(o_ref, o_scr, g)
        to_tokens(l_ref, l_scr, g)
    lses = [l_scr[g, 0] for g in range(3)]
    mx = jnp.maximum(jnp.maximum(lses[0], lses[1]), lses[2])
    es = [jnp.exp(l - mx) for l in lses]
    den = es[0] + es[1] + es[2]
    ws = [e / den for e in es]
    parts = []
    for h in range(DIL_HEADS):
        acc = None
        for g in range(3):
            term = o_scr[g, h] * ws[g][:, h:h + 1]
            acc = term if acc is None else acc + term
        parts.append(acc.astype(BF16))
    merged = jnp.concatenate(parts, axis=1)
    out_ref[...] = x_ref[...] + jnp.dot(merged, w_ref[...], preferred_element_type=F32)


def dilated_merge_project(outs, lses, w_out, x2, batch, seq, *, tm):
    t, d = x2.shape
    hw = w_out.shape[0]
    ns = seq // tm

    def grouped(arr):
        _, dil, m_total, w = arr.shape
        per = tm // dil
        view = arr.reshape(batch, dil, ns, per, w)
        return view, pl.BlockSpec((None, dil, None, per, w), lambda i: (i // ns, 0, i % ns, 0, 0))

    views, specs = zip(*[grouped(a) for a in (*outs, *lses)])
    row = pl.BlockSpec((tm, d), lambda i: (i, 0))
    return pl.pallas_call(
        _dil_merge_kernel,
        grid=(t // tm,),
        in_specs=list(specs) + [pl.BlockSpec((hw, d), lambda i: (0, 0)), row],
        out_specs=row,
        out_shape=jax.ShapeDtypeStruct((t, d), F32),
        scratch_shapes=[pltpu.VMEM((3, hw // LANES, tm, LANES), F32), pltpu.VMEM((3, 1, tm, LANES), F32)],
        compiler_params=_params(("parallel",)),
        name="dilated_merge_project",
    )(*views, w_out, x2)


def _rglru_scan_kernel(xc_ref, xp_ref, xn_ref, cw_ref, cb_ref, wa_ref, ba_ref, wx_ref, bx_ref, lam_ref,
                       o_ref, a_scr, u_scr, h_scr, *, reverse, n_steps):
    s = pl.program_id(1)
    ts, nb, tc = xc_ref.shape
    chunk = (n_steps - 1 - s) if reverse else s

    @pl.when(s == 0)
    def _():
        h_scr[...] = jnp.zeros(h_scr.shape, F32)

    prev = jnp.where(chunk == 0, 0.0, xp_ref[...])
    nxt = jnp.where(chunk == n_steps - 1, 0.0, xn_ref[...])
    xe = jnp.concatenate([prev, xc_ref[...], nxt], axis=0)
    cw = cw_ref[...]
    xc = cb_ref[...].reshape(1, 1, tc) + xe[0:ts] * cw[0:1].reshape(1, 1, tc)
    for j in range(1, CONV_W):
        xc = xc + xe[j:j + ts] * cw[j:j + 1].reshape(1, 1, tc)
    xc2 = xc.reshape(ts * nb, tc)
    lam = lam_ref[...]
    softplus_neg_lam = jnp.maximum(-lam, 0.0) + jnp.log1p(jnp.exp(-jnp.abs(lam)))
    for kb in range(tc // RG_BLOCK_W):
        sl = slice(kb * RG_BLOCK_W, (kb + 1) * RG_BLOCK_W)
        xb = xc2[:, sl]
        xb16 = xb.astype(BF16)
        r = jax.nn.sigmoid(jnp.dot(xb16, wa_ref[kb], preferred_element_type=F32) + ba_ref[:, sl])
        gate = jax.nn.sigmoid(jnp.dot(xb16, wx_ref[kb], preferred_element_type=F32) + bx_ref[:, sl])
        log_a = (-RG_C) * r * softplus_neg_lam[:, sl]
        a = jnp.exp(log_a)
        u = jnp.sqrt(1.0 - a * a) * (gate * xb)
        a_scr[:, :, sl] = a.reshape(ts, nb, RG_BLOCK_W)
        u_scr[:, :, sl] = u.reshape(ts, nb, RG_BLOCK_W)

    def step(t, h):
        tt = (ts - 1 - t) if reverse else t
        h = a_scr[tt] * h + u_scr[tt]
        o_ref[tt] = h
        return h

    h_scr[...] = lax.fori_loop(0, ts, step, h_scr[...], unroll=8)


def rglru_scan(yx3, conv_w, conv_b, w_a, b_a, w_x, b_x, lam, *, reverse, ts, tc):
    seq, nb, two_c = yx3.shape
    c = two_c // 2
    nc, n_steps = c // tc, seq // ts
    kb = tc // RG_BLOCK_W

    def chunk_of(s):
        return (n_steps - 1 - s) if reverse else s

    vec = lambda rows: pl.BlockSpec((rows, tc), lambda ci, s: (0, ci))
    mat = pl.BlockSpec((kb, RG_BLOCK_W, RG_BLOCK_W), lambda ci, s: (ci, 0, 0))
    return pl.pallas_call(
        functools.partial(_rglru_scan_kernel, reverse=reverse, n_steps=n_steps),
        grid=(nc, n_steps),
        in_specs=[
            pl.BlockSpec((ts, nb, tc), lambda ci, s: (chunk_of(s), 0, nc + ci)),
            pl.BlockSpec((CONV_LEFT, nb, tc),
                         lambda ci, s: (jnp.maximum(chunk_of(s) * (ts // CONV_LEFT) - 1, 0), 0, nc + ci)),
            pl.BlockSpec((1, nb, tc),
                         lambda ci, s: (jnp.minimum((chunk_of(s) + 1) * ts, seq - 1), 0, nc + ci)),
            vec(CONV_W), vec(1), mat, vec(1), mat, vec(1), vec(1),
        ],
        out_specs=pl.BlockSpec((ts, nb, tc), lambda ci, s: (chunk_of(s), 0, ci)),
        out_shape=jax.ShapeDtypeStruct((seq, nb, c), F32),
        scratch_shapes=[
            pltpu.VMEM((ts, nb, tc), F32),
            pltpu.VMEM((ts, nb, tc), F32),
            pltpu.VMEM((nb, tc), F32),
        ],
        compiler_params=_params(("parallel", "arbitrary")),
        name="rglru_scan_bwd" if reverse else "rglru_scan_fwd",
    )(yx3, yx3, yx3, conv_w, conv_b, w_a, b_a, w_x, b_x, lam)


def _gelu_tanh(y):
    return y * (0.5 * (1.0 + jnp.tanh(0.7978845608028654 * (y + 0.044715 * (y * y * y)))))


def _rglru_out_kernel(y_ref, hf_ref, hb_ref, w_ref, x_ref, o_ref):
    gated = _gelu_tanh(y_ref[...]) * (hf_ref[...] + hb_ref[...])
    o_ref[...] = x_ref[...] + jnp.dot(gated.astype(BF16), w_ref[...], preferred_element_type=F32)


def rglru_out_project(yx, hf, hb, w_out, x2, batch, seq, *, tm):
    c, d = w_out.shape
    ns = seq // tm
    return pl.pallas_call(
        _rglru_out_kernel,
        grid=(batch, ns),
        in_specs=[
            pl.BlockSpec((tm, c), lambda b, s: (s, 2 * b)),
            pl.BlockSpec((tm, c), lambda b, s: (s, b)),
            pl.BlockSpec((tm, c), lambda b, s: (s, b)),
            pl.BlockSpec((c, d), lambda b, s: (0, 0)),
            pl.BlockSpec((tm, d), lambda b, s: (b * ns + s, 0)),
        ],
        out_specs=pl.BlockSpec((tm, d), lambda b, s: (b * ns + s, 0)),
        out_shape=jax.ShapeDtypeStruct(x2.shape, F32),
        compiler_params=_params(("parallel", "parallel")),
        name="rglru_out_project",
    )(yx, hf, hb, w_out, x2)


def _swiglu_step(h, wg_ref, wu_ref, wd_ref, acc_scr):
    gate = jnp.dot(h, wg_ref[...], preferred_element_type=F32)
    up = jnp.dot(h, wu_ref[...], preferred_element_type=F32)
    act = gate * jax.nn.sigmoid(gate) * up
    acc_scr[...] += jnp.dot(act.astype(BF16), wd_ref[...], preferred_element_type=F32)


def _ffn_kernel(x_ref, g_ref, wg_ref, wu_ref, wd_ref, o_ref, h_scr, acc_scr):
    f = pl.program_id(1)

    @pl.when(f == 0)
    def _():
        h_scr[...] = _rms(x_ref[...], g_ref[...]).astype(BF16)
        acc_scr[...] = jnp.zeros(acc_scr.shape, F32)

    _swiglu_step(h_scr[...], wg_ref, wu_ref, wd_ref, acc_scr)

    @pl.when(f == pl.num_programs(1) - 1)
    def _():
        o_ref[...] = x_ref[...] + acc_scr[...]


def swiglu_ffn(x2, gain, w_gate_up, w_down, *, tm, tf):
    t, d = x2.shape
    ff = w_down.shape[0]
    nf = ff // tf
    return pl.pallas_call(
        _ffn_kernel,
        grid=(t // tm, nf),
        in_specs=[
            pl.BlockSpec((tm, d), lambda i, f: (i, 0)),
            pl.BlockSpec((1, d), lambda i, f: (0, 0)),
            pl.BlockSpec((d, tf), lambda i, f: (0, f)),
            pl.BlockSpec((d, tf), lambda i, f: (0, nf + f)),
            pl.BlockSpec((tf, d), lambda i, f: (f, 0)),
        ],
        out_specs=pl.BlockSpec((tm, d), lambda i, f: (i, 0)),
        out_shape=jax.ShapeDtypeStruct((t, d), F32),
        scratch_shapes=[pltpu.VMEM((tm, d), BF16), pltpu.VMEM((tm, d), F32)],
        compiler_params=_params(("parallel", "arbitrary")),
        name="dense_ffn",
    )(x2, gain, w_gate_up, w_gate_up, w_down)


SUBLANES = 8


def _copy_token(src_hbm, dst, src_tok, dst_tok, sem):
    return pltpu.make_async_copy(src_hbm.at[pl.ds(src_tok * SUBLANES, SUBLANES)],
                                 dst.at[pl.ds(dst_tok * SUBLANES, SUBLANES)], sem)


def _moe_scatter_kernel(zero_ref, pos_ref, x_ref, o_hbm, stage, zeros_scr, sem, *, tm):
    p = pl.program_id(0)
    n = pl.num_programs(0)
    slot = p % 2
    chunks = x_ref.shape[1] // LANES
    block_rows = tm * SUBLANES

    def zero_copy(j):
        dst = o_hbm.at[pl.ds(zero_ref[j] * SUBLANES, zeros_scr.shape[0])]
        return pltpu.make_async_copy(zeros_scr, dst, sem.at[2])

    @pl.when(p == 0)
    def _():
        zeros_scr[...] = jnp.zeros(zeros_scr.shape, F32)
        for j in range(2 * N_EXPERTS):
            zero_copy(j).start()
            zero_copy(j).wait()

    def slot_wait(s):
        rows = pl.ds(0, 2 * block_rows)
        pltpu.make_async_copy(o_hbm.at[rows], o_hbm.at[rows], sem.at[s]).wait()

    @pl.when(p >= 2)
    def _():
        slot_wait(slot)

    for c in range(chunks):
        stage[slot, pl.ds(c, tm, stride=SUBLANES), :] = x_ref[:, c * LANES:(c + 1) * LANES]

    def issue(r, carry):
        src = stage.at[slot, pl.ds(r * SUBLANES, SUBLANES)]
        for k in range(TOP_K):
            dst = o_hbm.at[pl.ds(pos_ref[0, k * tm + r] * SUBLANES, SUBLANES)]
            pltpu.make_async_copy(src, dst, sem.at[slot]).start()
        return carry
    lax.fori_loop(0, tm, issue, 0, unroll=8)

    @pl.when(p == n - 1)
    def _():
        slot_wait(slot)

        @pl.when(n >= 2)
        def _():
            slot_wait(1 - slot)


def moe_scatter(x2, pos, zero_starts, n_tiles, *, tm, te):
    t, d = x2.shape
    grid_spec = pltpu.PrefetchScalarGridSpec(
        num_scalar_prefetch=1,
        grid=(t // tm,),
        in_specs=[
            pl.BlockSpec((None, 1, TOP_K * tm), lambda p, z: (p, 0, 0), memory_space=pltpu.SMEM),
            pl.BlockSpec((tm, d), lambda p, z: (p, 0)),
        ],
        out_specs=pl.BlockSpec(memory_space=pl.ANY),
        scratch_shapes=[pltpu.VMEM((2, tm * SUBLANES, LANES), F32), pltpu.VMEM((te * SUBLANES, LANES), F32),
                        pltpu.SemaphoreType.DMA((3,))],
    )
    return pl.pallas_call(
        functools.partial(_moe_scatter_kernel, tm=tm),
        grid_spec=grid_spec,
        out_shape=jax.ShapeDtypeStruct(((n_tiles + 1) * te * SUBLANES, LANES), F32),
        compiler_params=_params(("arbitrary",)),
        name="moe_scatter",
    )(zero_starts, pos, x2)


def _moe_expert_kernel(te_ref, meta_ref, xs_ref, g_ref, wg_ref, wu_ref, wd_ref, y_ref, h_scr, acc_scr):
    p = pl.program_id(0)
    f = pl.program_id(1)
    tm = acc_scr.shape[0]
    chunks = acc_scr.shape[1] // LANES
    used = p < meta_ref[0]

    @pl.when(used & (f == 0))
    def _():
        x = jnp.concatenate([xs_ref[pl.ds(c, tm, stride=SUBLANES), :] for c in range(chunks)], axis=1)
        h_scr[...] = _rms(x, g_ref[...]).astype(BF16)
        acc_scr[...] = jnp.zeros(acc_scr.shape, F32)

    @pl.when(used)
    def _():
        _swiglu_step(h_scr[...], wg_ref, wu_ref, wd_ref, acc_scr)

    @pl.when(f == pl.num_programs(1) - 1)
    def _():
        y = jnp.where(used, acc_scr[...], 0.0)
        for c in range(chunks):
            y_ref[pl.ds(c, tm, stride=SUBLANES), :] = y[:, c * LANES:(c + 1) * LANES]


def moe_experts(xs, gain, w_gate_up, w_down, tile_expert, meta, n_tiles, *, tm, tf):
    n_e, ff, d = w_down.shape
    nf = ff // tf

    def f_eff(p, f, meta_ref):
        return jnp.where(p < meta_ref[0], f, nf - 1)

    grid_spec = pltpu.PrefetchScalarGridSpec(
        num_scalar_prefetch=2,
        grid=(n_tiles, nf),
        in_specs=[
            pl.BlockSpec((tm * SUBLANES, LANES), lambda p, f, te, m: (p, 0)),
            pl.BlockSpec((1, d), lambda p, f, te, m: (0, 0)),
            pl.BlockSpec((None, d, tf), lambda p, f, te, m: (te[p], 0, f_eff(p, f, m))),
            pl.BlockSpec((None, d, tf), lambda p, f, te, m: (te[p], 0, nf + f_eff(p, f, m))),
            pl.BlockSpec((None, tf, d), lambda p, f, te, m: (te[p], f_eff(p, f, m), 0)),
        ],
        out_specs=pl.BlockSpec((tm * SUBLANES, LANES), lambda p, f, te, m: (p, 0)),
        scratch_shapes=[pltpu.VMEM((tm, d), BF16), pltpu.VMEM((tm, d), F32)],
    )
    return pl.pallas_call(
        _moe_expert_kernel,
        grid_spec=grid_spec,
        out_shape=jax.ShapeDtypeStruct((n_tiles * tm * SUBLANES, LANES), F32),
        compiler_params=_params(("parallel", "arbitrary")),
        name="moe_experts",
    )(tile_expert, meta, xs, gain, w_gate_up, w_gate_up, w_down)


def _moe_combine_kernel(cur_ref, nxt_ref, x_ref, route_ref, y_hbm, o_ref, buf, sem, *, tm):
    p = pl.program_id(0)
    n = pl.num_programs(0)
    chunks = x_ref.shape[1] // LANES

    def issue(idx_ref, slot):
        def body(r, carry):
            _copy_token(y_hbm, buf.at[slot], idx_ref[0, r], r, sem.at[slot]).start()
            return carry
        lax.fori_loop(0, 2 * tm, body, 0, unroll=8)

    @pl.when(p == 0)
    def _():
        issue(cur_ref, 0)

    @pl.when(p + 1 < n)
    def _():
        issue(nxt_ref, (p + 1) % 2)

    slot = p % 2
    pltpu.make_async_copy(y_hbm.at[pl.ds(0, 2 * tm * SUBLANES)], buf.at[slot], sem.at[slot]).wait()
    route = route_ref[...]
    g1 = route[:, 2:3]
    g2 = route[:, 3:4]
    for c in range(chunks):
        first = buf[slot, pl.ds(c, tm, stride=SUBLANES), :]
        second = buf[slot, pl.ds(tm * SUBLANES + c, tm, stride=SUBLANES), :]
        sl = slice(c * LANES, (c + 1) * LANES)
        o_ref[:, sl] = x_ref[:, sl] + (g1 * first + g2 * second)


def moe_combine(x2, y_tiles, pos, route, *, tm):
    t, d = x2.shape
    n = t // tm
    idx_spec = lambda shift: pl.BlockSpec((None, 1, 2 * tm), lambda p: (jnp.minimum(p + shift, n - 1), 0, 0),
                                          memory_space=pltpu.SMEM)
    return pl.pallas_call(
        functools.partial(_moe_combine_kernel, tm=tm),
        grid=(n,),
        in_specs=[idx_spec(0), idx_spec(1), pl.BlockSpec((tm, d), lambda p: (p, 0)),
                  pl.BlockSpec((tm, LANES), lambda p: (p, 0)), pl.BlockSpec(memory_space=pl.ANY)],
        out_specs=pl.BlockSpec((tm, d), lambda p: (p, 0)),
        out_shape=jax.ShapeDtypeStruct((t, d), F32),
        scratch_shapes=[pltpu.VMEM((2, 2 * tm * SUBLANES, LANES), F32), pltpu.SemaphoreType.DMA((2,))],
        compiler_params=_params(("arbitrary",)),
        name="moe_combine",
    )(pos, pos, x2, route, y_tiles)


def _moe_plan(route, te, tm):
    t = route.shape[0]
    experts = route[:, :TOP_K].astype(jnp.int32)
    onehot = (experts[:, :, None] == jnp.arange(N_EXPERTS)[None, None, :]).astype(jnp.int32).sum(axis=1)
    incl = jnp.cumsum(onehot, axis=0)
    rank = incl - onehot
    counts = incl[-1]
    tiles_per = (counts + te - 1) // te
    tile_end = jnp.cumsum(tiles_per)
    row_start = (tile_end - tiles_per) * te
    dest = row_start[experts] + jnp.take_along_axis(rank, experts, axis=1)
    n_tiles = -(-(TOP_K * t) // te) + N_EXPERTS
    tile_expert = jnp.minimum(jnp.searchsorted(tile_end, jnp.arange(n_tiles), side="right"),
                              N_EXPERTS - 1).astype(jnp.int32)
    n_used = tile_end[-1]
    tail = jnp.minimum((n_used + jnp.arange(N_EXPERTS)) * te, n_tiles * te)
    zero_starts = jnp.concatenate([row_start + counts, tail]).astype(jnp.int32)
    pos = jnp.concatenate([dest[:, k].reshape(t // tm, tm) for k in range(TOP_K)], axis=1)
    return (pos.reshape(t // tm, 1, TOP_K * tm).astype(jnp.int32), zero_starts, tile_expert,
            n_used.reshape(1).astype(jnp.int32), n_tiles)


def _router_kernel(x_ref, g_ref, r_ref, b_ref, c_ref):
    h = _rms(x_ref[...], g_ref[...])
    logits = jnp.dot(h, r_ref[...], preferred_element_type=F32, precision=lax.Precision.HIGHEST)
    logits = logits + b_ref[...]
    lane = lax.broadcasted_iota(jnp.int32, logits.shape, 1)
    logits = jnp.where(lane < N_EXPERTS, logits, NEG_INF)
    v1 = jnp.max(logits, axis=-1, keepdims=True)
    i1 = jnp.min(jnp.where(logits == v1, lane, LANES), axis=-1, keepdims=True)
    rest = jnp.where(lane == i1, NEG_INF, logits)
    v2 = jnp.max(rest, axis=-1, keepdims=True)
    i2 = jnp.min(jnp.where(rest == v2, lane, LANES), axis=-1, keepdims=True)
    e2 = jnp.exp(v2 - v1)
    g1 = 1.0 / (1.0 + e2)
    g2 = e2 / (1.0 + e2)
    c_ref[...] = jnp.where(lane == 0, i1.astype(F32), jnp.where(lane == 1, i2.astype(F32),
                           jnp.where(lane == 2, g1, jnp.where(lane == 3, g2, 0.0))))


def moe_router(x2, gain, router, router_b, *, tm):
    t, d = x2.shape
    r_pad = jnp.zeros((d, LANES), F32).at[:, :N_EXPERTS].set(router)
    b_pad = jnp.zeros((1, LANES), F32).at[0, :N_EXPERTS].set(router_b)
    return pl.pallas_call(
        _router_kernel,
        grid=(t // tm,),
        in_specs=[
            pl.BlockSpec((tm, d), lambda i: (i, 0)),
            pl.BlockSpec((1, d), lambda i: (0, 0)),
            pl.BlockSpec((d, LANES), lambda i: (0, 0)),
            pl.BlockSpec((1, LANES), lambda i: (0, 0)),
        ],
        out_specs=pl.BlockSpec((tm, LANES), lambda i: (i, 0)),
        out_shape=jax.ShapeDtypeStruct((t, LANES), F32),
        compiler_params=_params(("parallel",)),
        name="moe_router",
    )(x2, gain, r_pad, b_pad)


def _inv_freq(dim, theta):
    return theta ** (-jnp.arange(0, dim, 2, dtype=F32) / dim)


_Q = HEAD_DIM // 4
AXIAL_PERM = np.concatenate([np.arange(0, _Q), np.arange(2 * _Q, 3 * _Q), np.arange(_Q, 2 * _Q),
                             np.arange(3 * _Q, 4 * _Q)])
_R = ROT_DIMS // 2
_REST = (HEAD_DIM - ROT_DIMS) // 2
PARTIAL_PERM = np.concatenate([np.arange(0, _R), np.arange(ROT_DIMS, ROT_DIMS + _REST), np.arange(_R, ROT_DIMS),
                               np.arange(ROT_DIMS + _REST, HEAD_DIM)])


def _axial_tables(seq):
    t = jnp.arange(seq)
    rows = (t // GRID_W).astype(F32)
    cols = (t % GRID_W).astype(F32)
    freq = _inv_freq(HEAD_DIM // 2, AXIAL_THETA)
    ang_r = rows[:, None] * freq[None, :]
    ang_c = cols[:, None] * freq[None, :]
    ang = jnp.concatenate([ang_r, ang_c, ang_r, ang_c], axis=-1)
    sign = jnp.concatenate([-jnp.ones(ROPE_SHIFT, F32), jnp.ones(ROPE_SHIFT, F32)])
    return jnp.cos(ang), jnp.sin(ang) * sign[None, :]


def _partial_tables(seq):
    pos = jnp.arange(seq, dtype=F32)
    freq = _inv_freq(ROT_DIMS, ROPE_THETA)
    ang = pos[:, None] * freq[None, :]
    ones = jnp.ones((seq, _REST), F32)
    zeros = jnp.zeros((seq, _REST), F32)
    cos = jnp.concatenate([jnp.cos(ang), ones, jnp.cos(ang), ones], axis=-1)
    sin = jnp.concatenate([-jnp.sin(ang), zeros, jnp.sin(ang), zeros], axis=-1)
    return cos, sin


def _permute_qk_columns(w, head_is_qk, perm):
    idx = np.concatenate([h * HEAD_DIM + (perm if qk else np.arange(HEAD_DIM))
                          for h, qk in enumerate(head_is_qk)])
    return jnp.take(w, jnp.asarray(idx, jnp.int32), axis=1)


def _row(v):
    return v.reshape(1, -1).astype(F32)


def _rglru_layer(x2, batch, seq, norm, w_in, conv_w, conv_b, w_a, b_a, w_x, b_x, lam, w_out):
    c = w_out.shape[0]
    yx = norm_matmul_time_major(x2, _row(norm), w_in.astype(BF16), batch, seq,
                                tm=512, tn=1024, out_dtype=F32)
    yx3 = yx.reshape(seq, batch, 2 * c)
    hs = []
    for direction in range(2):
        hs.append(rglru_scan(
            yx3, conv_w, _row(conv_b), w_a[direction].astype(BF16), _row(b_a[direction]),
            w_x[direction].astype(BF16), _row(b_x[direction]), _row(lam[direction]),
            reverse=direction == 1, ts=256, tc=512).reshape(seq, batch * c))
    return rglru_out_project(yx, hs[0], hs[1], w_out.astype(BF16), x2, batch, seq, tm=512)


def _gqa_layer(x2, batch, seq, norm, w_qkv, q_gain, k_gain, w_out):
    nq = GQA_HEADS * HEAD_DIM
    nkv = GQA_KV_HEADS * HEAD_DIM
    tn = nkv
    modes = jnp.array([1] * (nq // tn) + [2] * (nkv // tn) + [0] * (nkv // tn), jnp.int32)
    cos, sin = _axial_tables(seq)
    is_qk = [True] * (GQA_HEADS + GQA_KV_HEADS) + [False] * GQA_KV_HEADS
    w16 = _permute_qk_columns(w_qkv, is_qk, AXIAL_PERM).astype(BF16)
    qkv = qkv_project(x2, _row(norm), w16, modes, cos, sin, _row(q_gain[AXIAL_PERM]), _row(k_gain[AXIAL_PERM]),
                      batch, seq, tm=512, tn=tn, scale=HEAD_DIM ** -0.5 * LOG2_E, dil=1)
    o = gqa_attention(qkv.reshape(batch, seq, nq + 2 * nkv), tq=256, tk=512, streams=2)
    return matmul_residual(o.reshape(batch * seq, nq), w_out.astype(BF16), x2, tm=512)


def _dilated_layer(x2, batch, seq, norm, w_qkv, q_gain, k_gain, w_out):
    hw = DIL_HEADS * HEAD_DIM
    modes = jnp.array([1, 2, 0], jnp.int32)
    cos, sin = _partial_tables(seq)
    is_qk = ([True] * (2 * DIL_HEADS) + [False] * DIL_HEADS) * len(DIL_PATTERNS)
    w16 = _permute_qk_columns(w_qkv, is_qk, PARTIAL_PERM).astype(BF16)
    outs, lses = [], []
    for g, (window, dil) in enumerate(DIL_PATTERNS):
        span = window // (2 * dil)
        assert seq % (dil * span) == 0
        qkv = qkv_project(x2, _row(norm), w16[:, g * 3 * hw:(g + 1) * 3 * hw], modes, cos, sin,
                          _row(q_gain[PARTIAL_PERM]), _row(k_gain[PARTIAL_PERM]), batch, seq,
                          tm=512, tn=hw, scale=HEAD_DIM ** -0.5, dil=dil)
        o, lse = dilated_attention(qkv, span, tq=2 * span)
        outs.append(o)
        lses.append(lse)
    return dilated_merge_project(outs, lses, w_out.astype(BF16), x2, batch, seq, tm=512)


def _dense_ffn_layer(x2, norm, w_gate_up, w_down):
    return swiglu_ffn(x2, _row(norm), w_gate_up.astype(BF16), w_down.astype(BF16), tm=1024, tf=512)


def _moe_layer(x2, norm, router, router_b, w_gate_up, w_down):
    tm, te = 512, 1024
    route = moe_router(x2, _row(norm), router, router_b, tm=tm)
    pos, zero_starts, tile_expert, meta, n_tiles = _moe_plan(route, te, tm)
    xs = moe_scatter(x2, pos, zero_starts, n_tiles, tm=tm, te=te)
    ys = moe_experts(xs, _row(norm), w_gate_up.astype(BF16), w_down.astype(BF16), tile_expert, meta, n_tiles,
                     tm=te, tf=512)
    return moe_combine(x2, ys, pos, route, tm=tm)


def kernel(x, l0_attn_norm, l0_rg_w_in, l0_rg_conv_w, l0_rg_conv_b, l0_rg_w_a, l0_rg_b_a, l0_rg_w_x, l0_rg_b_x, l0_rg_lambda, l0_rg_w_out, l0_ffn_norm, l0_ffn_w_gate_up, l0_ffn_w_down, l1_attn_norm, l1_gqa_w_qkv, l1_gqa_q_norm, l1_gqa_k_norm, l1_gqa_w_out, l1_ffn_norm, l1_moe_router, l1_moe_router_b, l1_moe_w_gate_up, l1_moe_w_down, l2_attn_norm, l2_dil_w_qkv, l2_dil_q_norm, l2_dil_k_norm, l2_dil_w_out, l2_ffn_norm, l2_ffn_w_gate_up, l2_ffn_w_down, l3_attn_norm, l3_rg_w_in, l3_rg_conv_w, l3_rg_conv_b, l3_rg_w_a, l3_rg_b_a, l3_rg_w_x, l3_rg_b_x, l3_rg_lambda, l3_rg_w_out, l3_ffn_norm, l3_moe_router, l3_moe_router_b, l3_moe_w_gate_up, l3_moe_w_down):
    batch, seq, d = x.shape
    x2 = x.reshape(batch * seq, d)
    x2 = _rglru_layer(x2, batch, seq, l0_attn_norm, l0_rg_w_in, l0_rg_conv_w, l0_rg_conv_b, l0_rg_w_a,
                      l0_rg_b_a, l0_rg_w_x, l0_rg_b_x, l0_rg_lambda, l0_rg_w_out)
    x2 = _dense_ffn_layer(x2, l0_ffn_norm, l0_ffn_w_gate_up, l0_ffn_w_down)
    x2 = _gqa_layer(x2, batch, seq, l1_attn_norm, l1_gqa_w_qkv, l1_gqa_q_norm, l1_gqa_k_norm, l1_gqa_w_out)
    x2 = _moe_layer(x2, l1_ffn_norm, l1_moe_router, l1_moe_router_b, l1_moe_w_gate_up, l1_moe_w_down)
    x2 = _dilated_layer(x2, batch, seq, l2_attn_norm, l2_dil_w_qkv, l2_dil_q_norm, l2_dil_k_norm, l2_dil_w_out)
    x2 = _dense_ffn_layer(x2, l2_ffn_norm, l2_ffn_w_gate_up, l2_ffn_w_down)
    x2 = _rglru_layer(x2, batch, seq, l3_attn_norm, l3_rg_w_in, l3_rg_conv_w, l3_rg_conv_b, l3_rg_w_a,
                      l3_rg_b_a, l3_rg_w_x, l3_rg_b_x, l3_rg_lambda, l3_rg_w_out)
    x2 = _moe_layer(x2, l3_ffn_norm, l3_moe_router, l3_moe_router_b, l3_moe_w_gate_up, l3_moe_w_down)
    return x2.reshape(batch, seq, d)
```

```python
import functools

import jax
import jax.numpy as jnp
import numpy as np
from jax import lax
from jax.experimental import pallas as pl
from jax.experimental.pallas import tpu as pltpu

F32 = jnp.float32
BF16 = jnp.bfloat16

EPS = 1e-6
NEG_INF = -1e30
GRID_W = 64
HEAD_DIM = 128
LANES = 128
RG_BLOCK_W = 128
CONV_W = 4
CONV_LEFT = 2
RG_C = 8.0
GQA_HEADS = 8
GQA_KV_HEADS = 2
GQA_GROUP = GQA_HEADS // GQA_KV_HEADS
AXIAL_THETA = 10000.0
DIL_PATTERNS = ((128, 1), (512, 4), (2048, 16))
DIL_HEADS = 8
ROPE_THETA = 500000.0
ROT_DIMS = HEAD_DIM // 4
N_EXPERTS = 8
TOP_K = 2
VMEM_LIMIT = 56 * 1024 * 1024

NT_DIMS = (((1,), (1,)), ((), ()))
LOG2_E = 1.4426950408889634


def _params(sem, vmem=VMEM_LIMIT):
    return pltpu.CompilerParams(dimension_semantics=sem, vmem_limit_bytes=vmem)


def _rms(x, g):
    return x * lax.rsqrt(jnp.mean(x * x, axis=-1, keepdims=True) + EPS) * g


def _norm_matmul_kernel(x_ref, g_ref, w_ref, o_ref, h_scr):
    @pl.when(pl.program_id(2) == 0)
    def _():
        h_scr[...] = _rms(x_ref[...], g_ref[...]).astype(BF16)

    o_ref[...] = jnp.dot(h_scr[...], w_ref[...], preferred_element_type=F32).astype(o_ref.dtype)


def norm_matmul_time_major(x2, gain, w, batch, seq, *, tm, tn, out_dtype):
    d = x2.shape[1]
    n = w.shape[1]
    ns, nj = seq // tm, n // tn
    return pl.pallas_call(
        _norm_matmul_kernel,
        grid=(batch, ns, nj),
        in_specs=[
            pl.BlockSpec((tm, d), lambda b, s, j: (b * ns + s, 0)),
            pl.BlockSpec((1, d), lambda b, s, j: (0, 0)),
            pl.BlockSpec((d, tn), lambda b, s, j: (0, j)),
        ],
        out_specs=pl.BlockSpec((tm, tn), lambda b, s, j: (s, b * nj + j)),
        out_shape=jax.ShapeDtypeStruct((seq, batch * n), out_dtype),
        scratch_shapes=[pltpu.VMEM((tm, d), BF16)],
        compiler_params=_params(("parallel", "parallel", "arbitrary")),
        name="norm_matmul_tm",
    )(x2, gain, w)


ROPE_SHIFT = LANES // 2


PAIR = 2 * LANES
MODE_PLAIN, MODE_QUERY, MODE_KEY = 0, 1, 2


def _qkv_kernel(x_ref, g_ref, w_ref, cos_ref, sin_ref, qg_ref, kg_ref, o_ref, h_scr, *stage,
                scale, dil, pair_modes):
    tm = x_ref.shape[0]
    per = tm // dil
    hn = _rms(x_ref[...], g_ref[...])
    if dil == 1:
        h_scr[...] = hn.astype(BF16)
    else:
        stage_scr, = stage
        for c in range(hn.shape[1] // LANES):
            sl = slice(c * LANES, (c + 1) * LANES)
            stage_scr[c] = hn[:, sl]
            for r in range(dil):
                h_scr[r * per:(r + 1) * per, sl] = stage_scr[c, pl.ds(r, per, stride=dil), :].astype(BF16)
    h = h_scr[...]

    row = lax.broadcasted_iota(jnp.int32, (PAIR, PAIR), 0)
    col = lax.broadcasted_iota(jnp.int32, (PAIR, PAIR), 1)
    head_ones = (row // LANES == col // LANES).astype(BF16)
    cos = cos_ref[...]
    sin = sin_ref[...]
    tables = {}
    for mode, gain, sc in ((MODE_QUERY, qg_ref[...], scale), (MODE_KEY, kg_ref[...], 1.0)):
        if mode in pair_modes:
            tables[mode] = (cos * (gain * sc), sin * (pltpu.roll(gain, ROPE_SHIFT, 1) * sc))

    accs = [jnp.dot(h, w_ref[:, i * PAIR:(i + 1) * PAIR], preferred_element_type=F32)
            for i in range(len(pair_modes))]
    sumsqs = [None if mode == MODE_PLAIN else
              jnp.dot((a * a).astype(BF16), head_ones, preferred_element_type=F32)
              for a, mode in zip(accs, pair_modes)]
    for i, (a, sumsq, mode) in enumerate(zip(accs, sumsqs, pair_modes)):
        for c in range(PAIR // LANES):
            sl = slice(c * LANES, (c + 1) * LANES)
            head = a[:, sl]
            if mode != MODE_PLAIN:
                c1, s1 = tables[mode]
                inv = lax.rsqrt(sumsq[:, sl] * (1.0 / HEAD_DIM) + EPS)
                head = (head * c1 + pltpu.roll(head, ROPE_SHIFT, 1) * s1) * inv
            out_sl = slice(i * PAIR + c * LANES, i * PAIR + (c + 1) * LANES)
            o_ref[:, :, out_sl] = head.reshape(dil, per, LANES).astype(o_ref.dtype)


def _tile_residue_major(table, dil, tm):
    s, w = table.shape
    return table.reshape(s // tm, tm // dil, dil, w).transpose(0, 2, 1, 3).reshape(s, w)


def qkv_project(x2, gain, w, pair_modes, cos, sin, q_gain, k_gain, batch, seq, *, tm, scale, dil):
    t, d = x2.shape
    n = w.shape[1]
    assert n == len(pair_modes) * PAIR
    ns = seq // tm
    per = tm // dil
    cos_r = _tile_residue_major(cos, dil, tm)
    sin_r = _tile_residue_major(sin, dil, tm)
    table_spec = pl.BlockSpec((tm, LANES), lambda i: (i % ns, 0))
    vec_spec = pl.BlockSpec((1, LANES), lambda i: (0, 0))
    out = pl.pallas_call(
        functools.partial(_qkv_kernel, scale=scale, dil=dil, pair_modes=tuple(pair_modes)),
        grid=(t // tm,),
        in_specs=[
            pl.BlockSpec((tm, d), lambda i: (i, 0)),
            pl.BlockSpec((1, d), lambda i: (0, 0)),
            pl.BlockSpec((d, n), lambda i: (0, 0)),
            table_spec, table_spec, vec_spec, vec_spec,
        ],
        out_specs=pl.BlockSpec((None, dil, None, per, n), lambda i: (i // ns, 0, i % ns, 0, 0)),
        out_shape=jax.ShapeDtypeStruct((batch, dil, ns, per, n), BF16),
        scratch_shapes=[pltpu.VMEM((tm, d), BF16)]
        + ([pltpu.VMEM((d // LANES, tm, LANES), F32)] if dil > 1 else []),
        compiler_params=_params(("parallel",)),
        name=f"qkv_project_d{dil}",
    )(x2, gain, w, cos_r, sin_r, q_gain, k_gain)
    return out.reshape(batch, dil, seq // dil, n)


def _matmul_residual_kernel(a_ref, w_ref, x_ref, o_ref):
    o_ref[...] = x_ref[...] + jnp.dot(a_ref[...], w_ref[...], preferred_element_type=F32)


def matmul_residual(a, w, x2, *, tm):
    t, k = a.shape
    n = w.shape[1]
    return pl.pallas_call(
        _matmul_residual_kernel,
        grid=(t // tm,),
        in_specs=[
            pl.BlockSpec((tm, k), lambda i: (i, 0)),
            pl.BlockSpec((k, n), lambda i: (0, 0)),
            pl.BlockSpec((tm, n), lambda i: (i, 0)),
        ],
        out_specs=pl.BlockSpec((tm, n), lambda i: (i, 0)),
        out_shape=jax.ShapeDtypeStruct((t, n), F32),
        compiler_params=_params(("parallel",)),
        name="matmul_residual",
    )(a, w, x2)


def _gqa_attn_kernel(q_ref, k_ref, v_ref, o_ref, *, tk, streams):
    tq = q_ref.shape[1]
    seq = k_ref.shape[1]
    per = GQA_GROUP // streams
    qs = [jnp.concatenate([q_ref[0, :, g * LANES:(g + 1) * LANES] for g in range(st * per, (st + 1) * per)],
                          axis=0) for st in range(streams)]
    rows = per * tq
    ones = jnp.ones((tk, LANES), BF16)

    def body(c, carry):
        start = pl.multiple_of(c * tk, tk)
        k = k_ref[0, pl.ds(start, tk), :]
        v1 = jnp.concatenate([v_ref[0, pl.ds(start, tk), :], ones], axis=1)
        new = []
        for st in range(streams):
            m_prev, acc = carry[st]
            s = lax.dot_general(qs[st], k, NT_DIMS, preferred_element_type=F32)
            m_new = jnp.maximum(m_prev, jnp.max(s, axis=-1, keepdims=True))
            alpha = jnp.exp2(m_prev - m_new)
            p = jnp.exp2(s - m_new).astype(BF16)
            acc = alpha * acc + jnp.dot(p, v1, preferred_element_type=F32)
            new.append((m_new, acc))
        return tuple(new)

    init = tuple((jnp.full((rows, 1), NEG_INF, F32), jnp.zeros((rows, 2 * LANES), F32)) for _ in range(streams))
    final = lax.fori_loop(0, seq // tk, body, init, unroll=16)
    for st in range(streams):
        acc = final[st][1]
        o = acc[:, :LANES] / acc[:, LANES:]
        for gi in range(per):
            g = st * per + gi
            o_ref[0, :, g * LANES:(g + 1) * LANES] = o[gi * tq:(gi + 1) * tq].astype(o_ref.dtype)


def gqa_attention(qkv, *, tq, tk, streams):
    b, s, _ = qkv.shape
    gw = GQA_GROUP * HEAD_DIM
    k_off = GQA_HEADS
    v_off = GQA_HEADS + GQA_KV_HEADS
    return pl.pallas_call(
        functools.partial(_gqa_attn_kernel, tk=tk, streams=streams),
        grid=(b, GQA_KV_HEADS, s // tq),
        in_specs=[
            pl.BlockSpec((1, tq, gw), lambda bb, n, i: (bb, i, n)),
            pl.BlockSpec((1, s, HEAD_DIM), lambda bb, n, i: (bb, 0, k_off + n)),
            pl.BlockSpec((1, s, HEAD_DIM), lambda bb, n, i: (bb, 0, v_off + n)),
        ],
        out_specs=pl.BlockSpec((1, tq, gw), lambda bb, n, i: (bb, i, n)),
        out_shape=jax.ShapeDtypeStruct((b, s, GQA_HEADS * HEAD_DIM), BF16),
        compiler_params=_params(("parallel", "parallel", "arbitrary")),
        name="gqa_attention",
    )(qkv, qkv, qkv)


def _dil_attn_kernel(q_ref, kp_ref, kc_ref, kn_ref, vp_ref, vc_ref, vn_ref, o_ref, lse_ref,
                     *, span, m_total):
    i = pl.program_id(2)
    tq = q_ref.shape[0]
    qb = 2 * span
    nk = qb + 2 * span
    row = lax.broadcasted_iota(jnp.int32, (qb, nk), 0)
    col = lax.broadcasted_iota(jnp.int32, (qb, nk), 1)
    in_band = jnp.abs(col - span - row) <= span
    lane = lax.broadcasted_iota(jnp.int32, (qb, LANES), 1)
    ones = jnp.ones((nk, LANES), BF16)
    for sub in range(tq // qb):
        key_idx = i * tq + sub * qb - span + col
        ok = in_band & (key_idx >= 0) & (key_idx < m_total)
        lse_tile = jnp.zeros((qb, LANES), F32)
        rows = slice(sub * qb, (sub + 1) * qb)
        for h in range(DIL_HEADS):
            sl = slice(h * LANES, (h + 1) * LANES)
            k = jnp.concatenate([kp_ref[:, sl], kc_ref[:, sl], kn_ref[:, sl]], axis=0)[sub * qb:sub * qb + nk]
            v = jnp.concatenate([vp_ref[:, sl], vc_ref[:, sl], vn_ref[:, sl]], axis=0)[sub * qb:sub * qb + nk]
            s = lax.dot_general(q_ref[rows, sl], k, NT_DIMS, preferred_element_type=F32)
            s = jnp.where(ok, s, NEG_INF)
            m = jnp.max(s, axis=-1, keepdims=True)
            p = jnp.exp2(s - m).astype(BF16)
            acc = jnp.dot(p, jnp.concatenate([v, ones], axis=1), preferred_element_type=F32)
            o_ref[rows, sl] = (acc[:, :LANES] / acc[:, LANES:]).astype(o_ref.dtype)
            lse_tile = jnp.where(lane == h, m + jnp.log2(acc[:, LANES:LANES + 1]), lse_tile)
        lse_ref[rows, :] = lse_tile


def dilated_attention(qkv, span, *, tq):
    b, dil, m_total, n = qkv.shape
    hw = DIL_HEADS * HEAD_DIM
    nb = m_total // tq
    halo_per_tile = tq // span
    n_halo = m_total // span

    def tile(col):
        return pl.BlockSpec((None, None, tq, hw), lambda bb, r, i: (bb, r, i, col))

    def before(col):
        return pl.BlockSpec((None, None, span, hw),
                            lambda bb, r, i: (bb, r, jnp.maximum(i * halo_per_tile - 1, 0), col))

    def after(col):
        return pl.BlockSpec((None, None, span, hw),
                            lambda bb, r, i: (bb, r, jnp.minimum((i + 1) * halo_per_tile, n_halo - 1), col))

    return pl.pallas_call(
        functools.partial(_dil_attn_kernel, span=span, m_total=m_total),
        grid=(b, dil, nb),
        in_specs=[tile(0), before(1), tile(1), after(1), before(2), tile(2), after(2)],
        out_specs=[
            pl.BlockSpec((None, None, tq, hw), lambda bb, r, i: (bb, r, i, 0)),
            pl.BlockSpec((None, None, tq, LANES), lambda bb, r, i: (bb, r, i, 0)),
        ],
        out_shape=[
            jax.ShapeDtypeStruct((b, dil, m_total, hw), BF16),
            jax.ShapeDtypeStruct((b, dil, m_total, LANES), F32),
        ],
        compiler_params=_params(("parallel", "parallel", "arbitrary")),
        name=f"dilated_attention_d{dil}",
    )(qkv, qkv, qkv, qkv, qkv, qkv, qkv)


def _dil_merge_kernel(o0_ref, o1_ref, o2_ref, l0_ref, l1_ref, l2_ref, w_ref, x_ref, out_ref, o_scr, l_scr):
    def to_tokens(ref, scr, g):
        dil, per, width = ref.shape
        for c in range(width // LANES):
            sl = slice(c * LANES, (c + 1) * LANES)
            if dil == 1:
                scr[g, c] = ref[0, :, sl].astype(F32)
            else:
                for r in range(dil):
                    scr[g, c, pl.ds(r, per, stride=dil), :] = ref[r, :, sl].astype(F32)

    for g, (o_ref, l_ref) in enumerate(((o0_ref, l0_ref), (o1_ref, l1_ref), (o2_ref, l2_ref))):
        to_tokens(o_ref, o_scr, g)
        to_tokens(l_ref, l_scr, g)
    lses = [l_scr[g, 0] for g in range(3)]
    mx = jnp.maximum(jnp.maximum(lses[0], lses[1]), lses[2])
    es = [jnp.exp2(l - mx) for l in lses]
    den = es[0] + es[1] + es[2]
    ws = [e / den for e in es]
    parts = []
    for h in range(DIL_HEADS):
        acc = None
        for g in range(3):
            term = o_scr[g, h] * ws[g][:, h:h + 1]
            acc = term if acc is None else acc + term
        parts.append(acc.astype(BF16))
    merged = jnp.concatenate(parts, axis=1)
    out_ref[...] = x_ref[...] + jnp.dot(merged, w_ref[...], preferred_element_type=F32)


def dilated_merge_project(outs, lses, w_out, x2, batch, seq, *, tm):
    t, d = x2.shape
    hw = w_out.shape[0]
    ns = seq // tm

    def grouped(arr):
        _, dil, m_total, w = arr.shape
        per = tm // dil
        view = arr.reshape(batch, dil, ns, per, w)
        return view, pl.BlockSpec((None, dil, None, per, w), lambda i: (i // ns, 0, i % ns, 0, 0))

    views, specs = zip(*[grouped(a) for a in (*outs, *lses)])
    row = pl.BlockSpec((tm, d), lambda i: (i, 0))
    return pl.pallas_call(
        _dil_merge_kernel,
        grid=(t // tm,),
        in_specs=list(specs) + [pl.BlockSpec((hw, d), lambda i: (0, 0)), row],
        out_specs=row,
        out_shape=jax.ShapeDtypeStruct((t, d), F32),
        scratch_shapes=[pltpu.VMEM((3, hw // LANES, tm, LANES), F32), pltpu.VMEM((3, 1, tm, LANES), F32)],
        compiler_params=_params(("parallel",)),
        name="dilated_merge_project",
    )(*views, w_out, x2)


def _rglru_scan_kernel(xc_ref, xp_ref, xn_ref, cw_ref, cb_ref, wa_ref, ba_ref, wx_ref, bx_ref, lam_ref,
                       o_ref, a_scr, u_scr, h_scr, *, reverse, n_steps):
    s = pl.program_id(1)
    ts, nb, tc = xc_ref.shape
    chunk = (n_steps - 1 - s) if reverse else s

    @pl.when(s == 0)
    def _():
        h_scr[...] = jnp.zeros(h_scr.shape, F32)

    prev = jnp.where(chunk == 0, 0.0, xp_ref[...])
    nxt = jnp.where(chunk == n_steps - 1, 0.0, xn_ref[...])
    xe = jnp.concatenate([prev, xc_ref[...], nxt], axis=0)
    cw = cw_ref[...]
    xc = cb_ref[...].reshape(1, 1, tc) + xe[0:ts] * cw[0:1].reshape(1, 1, tc)
    for j in range(1, CONV_W):
        xc = xc + xe[j:j + ts] * cw[j:j + 1].reshape(1, 1, tc)
    xc2 = xc.reshape(ts * nb, tc)
    lam = lam_ref[...]
    softplus_neg_lam = jnp.maximum(-lam, 0.0) + jnp.log1p(jnp.exp(-jnp.abs(lam)))
    for kb in range(tc // RG_BLOCK_W):
        sl = slice(kb * RG_BLOCK_W, (kb + 1) * RG_BLOCK_W)
        xb = xc2[:, sl]
        xb16 = xb.astype(BF16)
        r = jax.nn.sigmoid(jnp.dot(xb16, wa_ref[kb], preferred_element_type=F32) + ba_ref[:, sl])
        gate = jax.nn.sigmoid(jnp.dot(xb16, wx_ref[kb], preferred_element_type=F32) + bx_ref[:, sl])
        log_a = (-RG_C) * r * softplus_neg_lam[:, sl]
        a = jnp.exp(log_a)
        u = jnp.sqrt(1.0 - a * a) * (gate * xb)
        a_scr[:, :, sl] = a.reshape(ts, nb, RG_BLOCK_W)
        u_scr[:, :, sl] = u.reshape(ts, nb, RG_BLOCK_W)

    def step(t, h):
        tt = (ts - 1 - t) if reverse else t
        h = a_scr[tt] * h + u_scr[tt]
        o_ref[tt] = h
        return h

    h_scr[...] = lax.fori_loop(0, ts, step, h_scr[...], unroll=8)


def rglru_scan(yx3, conv_w, conv_b, w_a, b_a, w_x, b_x, lam, *, reverse, ts, tc):
    seq, nb, two_c = yx3.shape
    c = two_c // 2
    nc, n_steps = c // tc, seq // ts
    kb = tc // RG_BLOCK_W

    def chunk_of(s):
        return (n_steps - 1 - s) if reverse else s

    vec = lambda rows: pl.BlockSpec((rows, tc), lambda ci, s: (0, ci))
    mat = pl.BlockSpec((kb, RG_BLOCK_W, RG_BLOCK_W), lambda ci, s: (ci, 0, 0))
    return pl.pallas_call(
        functools.partial(_rglru_scan_kernel, reverse=reverse, n_steps=n_steps),
        grid=(nc, n_steps),
        in_specs=[
            pl.BlockSpec((ts, nb, tc), lambda ci, s: (chunk_of(s), 0, nc + ci)),
            pl.BlockSpec((CONV_LEFT, nb, tc),
                         lambda ci, s: (jnp.maximum(chunk_of(s) * (ts // CONV_LEFT) - 1, 0), 0, nc + ci)),
            pl.BlockSpec((1, nb, tc),
                         lambda ci, s: (jnp.minimum((chunk_of(s) + 1) * ts, seq - 1), 0, nc + ci)),
            vec(CONV_W), vec(1), mat, vec(1), mat, vec(1), vec(1),
        ],
        out_specs=pl.BlockSpec((ts, nb, tc), lambda ci, s: (chunk_of(s), 0, ci)),
        out_shape=jax.ShapeDtypeStruct((seq, nb, c), F32),
        scratch_shapes=[
            pltpu.VMEM((ts, nb, tc), F32),
            pltpu.VMEM((ts, nb, tc), F32),
            pltpu.VMEM((nb, tc), F32),
        ],
        compiler_params=_params(("parallel", "arbitrary")),
        name="rglru_scan_bwd" if reverse else "rglru_scan_fwd",
    )(yx3, yx3, yx3, conv_w, conv_b, w_a, b_a, w_x, b_x, lam)


def _gelu_tanh(y):
    return y * (0.5 * (1.0 + jnp.tanh(0.7978845608028654 * (y + 0.044715 * (y * y * y)))))


def _rglru_out_kernel(y_ref, hf_ref, hb_ref, w_ref, x_ref, o_ref):
    gated = _gelu_tanh(y_ref[...]) * (hf_ref[...] + hb_ref[...])
    o_ref[...] = x_ref[...] + jnp.dot(gated.astype(BF16), w_ref[...], preferred_element_type=F32)


def rglru_out_project(yx, hf, hb, w_out, x2, batch, seq, *, tm):
    c, d = w_out.shape
    ns = seq // tm
    return pl.pallas_call(
        _rglru_out_kernel,
        grid=(batch, ns),
        in_specs=[
            pl.BlockSpec((tm, c), lambda b, s: (s, 2 * b)),
            pl.BlockSpec((tm, c), lambda b, s: (s, b)),
            pl.BlockSpec((tm, c), lambda b, s: (s, b)),
            pl.BlockSpec((c, d), lambda b, s: (0, 0)),
            pl.BlockSpec((tm, d), lambda b, s: (b * ns + s, 0)),
        ],
        out_specs=pl.BlockSpec((tm, d), lambda b, s: (b * ns + s, 0)),
        out_shape=jax.ShapeDtypeStruct(x2.shape, F32),
        compiler_params=_params(("parallel", "parallel")),
        name="rglru_out_project",
    )(yx, hf, hb, w_out, x2)


def _swiglu_step(h, wg_ref, wu_ref, wd_ref, acc_scr):
    gate = jnp.dot(h, wg_ref[...], preferred_element_type=F32)
    up = jnp.dot(h, wu_ref[...], preferred_element_type=F32)
    act = gate * jax.nn.sigmoid(gate) * up
    acc_scr[...] += jnp.dot(act.astype(BF16), wd_ref[...], preferred_element_type=F32)


def _ffn_kernel(x_ref, g_ref, wg_ref, wu_ref, wd_ref, o_ref, h_scr, acc_scr):
    f = pl.program_id(1)

    @pl.when(f == 0)
    def _():
        h_scr[...] = _rms(x_ref[...], g_ref[...]).astype(BF16)
        acc_scr[...] = jnp.zeros(acc_scr.shape, F32)

    _swiglu_step(h_scr[...], wg_ref, wu_ref, wd_ref, acc_scr)

    @pl.when(f == pl.num_programs(1) - 1)
    def _():
        o_ref[...] = x_ref[...] + acc_scr[...]


def swiglu_ffn(x2, gain, w_gate_up, w_down, *, tm, tf):
    t, d = x2.shape
    ff = w_down.shape[0]
    nf = ff // tf
    return pl.pallas_call(
        _ffn_kernel,
        grid=(t // tm, nf),
        in_specs=[
            pl.BlockSpec((tm, d), lambda i, f: (i, 0)),
            pl.BlockSpec((1, d), lambda i, f: (0, 0)),
            pl.BlockSpec((d, tf), lambda i, f: (0, f)),
            pl.BlockSpec((d, tf), lambda i, f: (0, nf + f)),
            pl.BlockSpec((tf, d), lambda i, f: (f, 0)),
        ],
        out_specs=pl.BlockSpec((tm, d), lambda i, f: (i, 0)),
        out_shape=jax.ShapeDtypeStruct((t, d), F32),
        scratch_shapes=[pltpu.VMEM((tm, d), BF16), pltpu.VMEM((tm, d), F32)],
        compiler_params=_params(("parallel", "arbitrary")),
        name="dense_ffn",
    )(x2, gain, w_gate_up, w_gate_up, w_down)


SUBLANES = 8


def _copy_token(src_hbm, dst, src_tok, dst_tok, sem):
    return pltpu.make_async_copy(src_hbm.at[pl.ds(src_tok * SUBLANES, SUBLANES)],
                                 dst.at[pl.ds(dst_tok * SUBLANES, SUBLANES)], sem)


def _moe_scatter_kernel(zero_ref, pos_ref, x_ref, o_hbm, stage, zeros_scr, sem, *, tm):
    p = pl.program_id(0)
    n = pl.num_programs(0)
    slot = p % 2
    chunks = x_ref.shape[1] // LANES
    block_rows = tm * SUBLANES

    def zero_copy(j):
        dst = o_hbm.at[pl.ds(zero_ref[j] * SUBLANES, zeros_scr.shape[0])]
        return pltpu.make_async_copy(zeros_scr, dst, sem.at[2])

    @pl.when(p == 0)
    def _():
        zeros_scr[...] = jnp.zeros(zeros_scr.shape, F32)
        for j in range(2 * N_EXPERTS):
            zero_copy(j).start()
            zero_copy(j).wait()

    def slot_wait(s):
        rows = pl.ds(0, 2 * block_rows)
        pltpu.make_async_copy(o_hbm.at[rows], o_hbm.at[rows], sem.at[s]).wait()

    @pl.when(p >= 2)
    def _():
        slot_wait(slot)

    for c in range(chunks):
        stage[slot, pl.ds(c, tm, stride=SUBLANES), :] = x_ref[:, c * LANES:(c + 1) * LANES]

    def issue(r, carry):
        src = stage.at[slot, pl.ds(r * SUBLANES, SUBLANES)]
        for k in range(TOP_K):
            dst = o_hbm.at[pl.ds(pos_ref[0, k * tm + r] * SUBLANES, SUBLANES)]
            pltpu.make_async_copy(src, dst, sem.at[slot]).start()
        return carry
    lax.fori_loop(0, tm, issue, 0, unroll=8)

    @pl.when(p == n - 1)
    def _():
        slot_wait(slot)

        @pl.when(n >= 2)
        def _():
            slot_wait(1 - slot)


def moe_scatter(x2, pos, zero_starts, n_tiles, *, tm, te):
    t, d = x2.shape
    grid_spec = pltpu.PrefetchScalarGridSpec(
        num_scalar_prefetch=1,
        grid=(t // tm,),
        in_specs=[
            pl.BlockSpec((None, 1, TOP_K * tm), lambda p, z: (p, 0, 0), memory_space=pltpu.SMEM),
            pl.BlockSpec((tm, d), lambda p, z: (p, 0)),
        ],
        out_specs=pl.BlockSpec(memory_space=pl.ANY),
        scratch_shapes=[pltpu.VMEM((2, tm * SUBLANES, LANES), F32), pltpu.VMEM((te * SUBLANES, LANES), F32),
                        pltpu.SemaphoreType.DMA((3,))],
    )
    return pl.pallas_call(
        functools.partial(_moe_scatter_kernel, tm=tm),
        grid_spec=grid_spec,
        out_shape=jax.ShapeDtypeStruct(((n_tiles + 1) * te * SUBLANES, LANES), F32),
        compiler_params=_params(("arbitrary",)),
        name="moe_scatter",
    )(zero_starts, pos, x2)


def _moe_expert_kernel(te_ref, meta_ref, xs_ref, g_ref, wg_ref, wu_ref, wd_ref, y_ref, h_scr, acc_scr):
    p = pl.program_id(0)
    f = pl.program_id(1)
    tm = acc_scr.shape[0]
    chunks = acc_scr.shape[1] // LANES
    used = p < meta_ref[0]

    @pl.when(used & (f == 0))
    def _():
        x = jnp.concatenate([xs_ref[pl.ds(c, tm, stride=SUBLANES), :] for c in range(chunks)], axis=1)
        h_scr[...] = _rms(x, g_ref[...]).astype(BF16)
        acc_scr[...] = jnp.zeros(acc_scr.shape, F32)

    @pl.when(used)
    def _():
        _swiglu_step(h_scr[...], wg_ref, wu_ref, wd_ref, acc_scr)

    @pl.when(f == pl.num_programs(1) - 1)
    def _():
        y = jnp.where(used, acc_scr[...], 0.0)
        for c in range(chunks):
            y_ref[pl.ds(c, tm, stride=SUBLANES), :] = y[:, c * LANES:(c + 1) * LANES]


def moe_experts(xs, gain, w_gate_up, w_down, tile_expert, meta, n_tiles, *, tm, tf):
    n_e, ff, d = w_down.shape
    nf = ff // tf

    def f_eff(p, f, meta_ref):
        return jnp.where(p < meta_ref[0], f, nf - 1)

    grid_spec = pltpu.PrefetchScalarGridSpec(
        num_scalar_prefetch=2,
        grid=(n_tiles, nf),
        in_specs=[
            pl.BlockSpec((tm * SUBLANES, LANES), lambda p, f, te, m: (p, 0)),
            pl.BlockSpec((1, d), lambda p, f, te, m: (0, 0)),
            pl.BlockSpec((None, d, tf), lambda p, f, te, m: (te[p], 0, f_eff(p, f, m))),
            pl.BlockSpec((None, d, tf), lambda p, f, te, m: (te[p], 0, nf + f_eff(p, f, m))),
            pl.BlockSpec((None, tf, d), lambda p, f, te, m: (te[p], f_eff(p, f, m), 0)),
        ],
        out_specs=pl.BlockSpec((tm * SUBLANES, LANES), lambda p, f, te, m: (p, 0)),
        scratch_shapes=[pltpu.VMEM((tm, d), BF16), pltpu.VMEM((tm, d), F32)],
    )
    return pl.pallas_call(
        _moe_expert_kernel,
        grid_spec=grid_spec,
        out_shape=jax.ShapeDtypeStruct((n_tiles * tm * SUBLANES, LANES), F32),
        compiler_params=_params(("parallel", "arbitrary")),
        name="moe_experts",
    )(tile_expert, meta, xs, gain, w_gate_up, w_gate_up, w_down)


def _moe_combine_kernel(cur_ref, nxt_ref, x_ref, route_ref, y_hbm, o_ref, buf, sem, *, tm):
    p = pl.program_id(0)
    n = pl.num_programs(0)
    chunks = x_ref.shape[1] // LANES

    def issue(idx_ref, slot):
        def body(r, carry):
            _copy_token(y_hbm, buf.at[slot], idx_ref[0, r], r, sem.at[slot]).start()
            return carry
        lax.fori_loop(0, 2 * tm, body, 0, unroll=8)

    @pl.when(p == 0)
    def _():
        issue(cur_ref, 0)

    @pl.when(p + 1 < n)
    def _():
        issue(nxt_ref, (p + 1) % 2)

    slot = p % 2
    pltpu.make_async_copy(y_hbm.at[pl.ds(0, 2 * tm * SUBLANES)], buf.at[slot], sem.at[slot]).wait()
    route = route_ref[...]
    g1 = route[:, 2:3]
    g2 = route[:, 3:4]
    for c in range(chunks):
        first = buf[slot, pl.ds(c, tm, stride=SUBLANES), :]
        second = buf[slot, pl.ds(tm * SUBLANES + c, tm, stride=SUBLANES), :]
        sl = slice(c * LANES, (c + 1) * LANES)
        o_ref[:, sl] = x_ref[:, sl] + (g1 * first + g2 * second)


def moe_combine(x2, y_tiles, pos, route, *, tm):
    t, d = x2.shape
    n = t // tm
    idx_spec = lambda shift: pl.BlockSpec((None, 1, 2 * tm), lambda p: (jnp.minimum(p + shift, n - 1), 0, 0),
                                          memory_space=pltpu.SMEM)
    return pl.pallas_call(
        functools.partial(_moe_combine_kernel, tm=tm),
        grid=(n,),
        in_specs=[idx_spec(0), idx_spec(1), pl.BlockSpec((tm, d), lambda p: (p, 0)),
                  pl.BlockSpec((tm, LANES), lambda p: (p, 0)), pl.BlockSpec(memory_space=pl.ANY)],
        out_specs=pl.BlockSpec((tm, d), lambda p: (p, 0)),
        out_shape=jax.ShapeDtypeStruct((t, d), F32),
        scratch_shapes=[pltpu.VMEM((2, 2 * tm * SUBLANES, LANES), F32), pltpu.SemaphoreType.DMA((2,))],
        compiler_params=_params(("arbitrary",)),
        name="moe_combine",
    )(pos, pos, x2, route, y_tiles)


def _moe_plan(route, te, tm):
    t = route.shape[0]
    experts = route[:, :TOP_K].astype(jnp.int32)
    onehot = (experts[:, :, None] == jnp.arange(N_EXPERTS)[None, None, :]).astype(jnp.int32).sum(axis=1)
    incl = jnp.cumsum(onehot, axis=0)
    rank = incl - onehot
    counts = incl[-1]
    tiles_per = (counts + te - 1) // te
    tile_end = jnp.cumsum(tiles_per)
    row_start = (tile_end - tiles_per) * te
    dest = row_start[experts] + jnp.take_along_axis(rank, experts, axis=1)
    n_tiles = -(-(TOP_K * t) // te) + N_EXPERTS
    tile_expert = jnp.minimum(jnp.searchsorted(tile_end, jnp.arange(n_tiles), side="right"),
                              N_EXPERTS - 1).astype(jnp.int32)
    n_used = tile_end[-1]
    tail = jnp.minimum((n_used + jnp.arange(N_EXPERTS)) * te, n_tiles * te)
    zero_starts = jnp.concatenate([row_start + counts, tail]).astype(jnp.int32)
    pos = jnp.concatenate([dest[:, k].reshape(t // tm, tm) for k in range(TOP_K)], axis=1)
    return (pos.reshape(t // tm, 1, TOP_K * tm).astype(jnp.int32), zero_starts, tile_expert,
            n_used.reshape(1).astype(jnp.int32), n_tiles)


def _router_kernel(x_ref, g_ref, r_ref, b_ref, c_ref):
    h = _rms(x_ref[...], g_ref[...])
    logits = jnp.dot(h, r_ref[...], preferred_element_type=F32, precision=lax.Precision.HIGHEST)
    logits = logits + b_ref[...]
    lane = lax.broadcasted_iota(jnp.int32, logits.shape, 1)
    logits = jnp.where(lane < N_EXPERTS, logits, NEG_INF)
    v1 = jnp.max(logits, axis=-1, keepdims=True)
    i1 = jnp.min(jnp.where(logits == v1, lane, LANES), axis=-1, keepdims=True)
    rest = jnp.where(lane == i1, NEG_INF, logits)
    v2 = jnp.max(rest, axis=-1, keepdims=True)
    i2 = jnp.min(jnp.where(rest == v2, lane, LANES), axis=-1, keepdims=True)
    e2 = jnp.exp(v2 - v1)
    g1 = 1.0 / (1.0 + e2)
    g2 = e2 / (1.0 + e2)
    c_ref[...] = jnp.where(lane == 0, i1.astype(F32), jnp.where(lane == 1, i2.astype(F32),
                           jnp.where(lane == 2, g1, jnp.where(lane == 3, g2, 0.0))))


def moe_router(x2, gain, router, router_b, *, tm):
    t, d = x2.shape
    r_pad = jnp.zeros((d, LANES), F32).at[:, :N_EXPERTS].set(router)
    b_pad = jnp.zeros((1, LANES), F32).at[0, :N_EXPERTS].set(router_b)
    return pl.pallas_call(
        _router_kernel,
        grid=(t // tm,),
        in_specs=[
            pl.BlockSpec((tm, d), lambda i: (i, 0)),
            pl.BlockSpec((1, d), lambda i: (0, 0)),
            pl.BlockSpec((d, LANES), lambda i: (0, 0)),
            pl.BlockSpec((1, LANES), lambda i: (0, 0)),
        ],
        out_specs=pl.BlockSpec((tm, LANES), lambda i: (i, 0)),
        out_shape=jax.ShapeDtypeStruct((t, LANES), F32),
        compiler_params=_params(("parallel",)),
        name="moe_router",
    )(x2, gain, r_pad, b_pad)


def _inv_freq(dim, theta):
    return theta ** (-jnp.arange(0, dim, 2, dtype=F32) / dim)


_Q = HEAD_DIM // 4
AXIAL_PERM = np.concatenate([np.arange(0, _Q), np.arange(2 * _Q, 3 * _Q), np.arange(_Q, 2 * _Q),
                             np.arange(3 * _Q, 4 * _Q)])
_R = ROT_DIMS // 2
_REST = (HEAD_DIM - ROT_DIMS) // 2
PARTIAL_PERM = np.concatenate([np.arange(0, _R), np.arange(ROT_DIMS, ROT_DIMS + _REST), np.arange(_R, ROT_DIMS),
                               np.arange(ROT_DIMS + _REST, HEAD_DIM)])


def _axial_tables(seq):
    t = jnp.arange(seq)
    rows = (t // GRID_W).astype(F32)
    cols = (t % GRID_W).astype(F32)
    freq = _inv_freq(HEAD_DIM // 2, AXIAL_THETA)
    ang_r = rows[:, None] * freq[None, :]
    ang_c = cols[:, None] * freq[None, :]
    ang = jnp.concatenate([ang_r, ang_c, ang_r, ang_c], axis=-1)
    sign = jnp.concatenate([-jnp.ones(ROPE_SHIFT, F32), jnp.ones(ROPE_SHIFT, F32)])
    return jnp.cos(ang), jnp.sin(ang) * sign[None, :]


def _partial_tables(seq):
    pos = jnp.arange(seq, dtype=F32)
    freq = _inv_freq(ROT_DIMS, ROPE_THETA)
    ang = pos[:, None] * freq[None, :]
    ones = jnp.ones((seq, _REST), F32)
    zeros = jnp.zeros((seq, _REST), F32)
    cos = jnp.concatenate([jnp.cos(ang), ones, jnp.cos(ang), ones], axis=-1)
    sin = jnp.concatenate([-jnp.sin(ang), zeros, jnp.sin(ang), zeros], axis=-1)
    return cos, sin


def _permute_qk_columns(w, head_is_qk, perm):
    idx = np.concatenate([h * HEAD_DIM + (perm if qk else np.arange(HEAD_DIM))
                          for h, qk in enumerate(head_is_qk)])
    return jnp.take(w, jnp.asarray(idx, jnp.int32), axis=1)


def _row(v):
    return v.reshape(1, -1).astype(F32)


def _rglru_layer(x2, batch, seq, norm, w_in, conv_w, conv_b, w_a, b_a, w_x, b_x, lam, w_out):
    c = w_out.shape[0]
    yx = norm_matmul_time_major(x2, _row(norm), w_in.astype(BF16), batch, seq,
                                tm=512, tn=2048, out_dtype=F32)
    yx3 = yx.reshape(seq, batch, 2 * c)
    hs = []
    for direction in range(2):
        hs.append(rglru_scan(
            yx3, conv_w, _row(conv_b), w_a[direction].astype(BF16), _row(b_a[direction]),
            w_x[direction].astype(BF16), _row(b_x[direction]), _row(lam[direction]),
            reverse=direction == 1, ts=256, tc=512).reshape(seq, batch * c))
    return rglru_out_project(yx, hs[0], hs[1], w_out.astype(BF16), x2, batch, seq, tm=512)


def _gqa_layer(x2, batch, seq, norm, w_qkv, q_gain, k_gain, w_out):
    nq = GQA_HEADS * HEAD_DIM
    nkv = GQA_KV_HEADS * HEAD_DIM
    modes = [MODE_QUERY] * (nq // PAIR) + [MODE_KEY] * (nkv // PAIR) + [MODE_PLAIN] * (nkv // PAIR)
    cos, sin = _axial_tables(seq)
    is_qk = [True] * (GQA_HEADS + GQA_KV_HEADS) + [False] * GQA_KV_HEADS
    w16 = _permute_qk_columns(w_qkv, is_qk, AXIAL_PERM).astype(BF16)
    qkv = qkv_project(x2, _row(norm), w16, modes, cos, sin, _row(q_gain[AXIAL_PERM]), _row(k_gain[AXIAL_PERM]),
                      batch, seq, tm=512, scale=HEAD_DIM ** -0.5 * LOG2_E, dil=1)
    o = gqa_attention(qkv.reshape(batch, seq, nq + 2 * nkv), tq=256, tk=512, streams=2)
    return matmul_residual(o.reshape(batch * seq, nq), w_out.astype(BF16), x2, tm=512)


def _dilated_layer(x2, batch, seq, norm, w_qkv, q_gain, k_gain, w_out):
    hw = DIL_HEADS * HEAD_DIM
    modes = [MODE_QUERY] * (hw // PAIR) + [MODE_KEY] * (hw // PAIR) + [MODE_PLAIN] * (hw // PAIR)
    cos, sin = _partial_tables(seq)
    is_qk = ([True] * (2 * DIL_HEADS) + [False] * DIL_HEADS) * len(DIL_PATTERNS)
    w16 = _permute_qk_columns(w_qkv, is_qk, PARTIAL_PERM).astype(BF16)
    outs, lses = [], []
    for g, (window, dil) in enumerate(DIL_PATTERNS):
        span = window // (2 * dil)
        assert seq % (dil * span) == 0
        qkv = qkv_project(x2, _row(norm), w16[:, g * 3 * hw:(g + 1) * 3 * hw], modes, cos, sin,
                          _row(q_gain[PARTIAL_PERM]), _row(k_gain[PARTIAL_PERM]), batch, seq,
                          tm=512, scale=HEAD_DIM ** -0.5 * LOG2_E, dil=dil)
        o, lse = dilated_attention(qkv, span, tq=min(4 * span, seq // dil))
        outs.append(o)
        lses.append(lse)
    return dilated_merge_project(outs, lses, w_out.astype(BF16), x2, batch, seq, tm=512)


def _dense_ffn_layer(x2, norm, w_gate_up, w_down):
    return swiglu_ffn(x2, _row(norm), w_gate_up.astype(BF16), w_down.astype(BF16), tm=1024, tf=512)


def _moe_layer(x2, norm, router, router_b, w_gate_up, w_down):
    tm, te = 512, 1024
    route = moe_router(x2, _row(norm), router, router_b, tm=tm)
    pos, zero_starts, tile_expert, meta, n_tiles = _moe_plan(route, te, tm)
    xs = moe_scatter(x2, pos, zero_starts, n_tiles, tm=tm, te=te)
    ys = moe_experts(xs, _row(norm), w_gate_up.astype(BF16), w_down.astype(BF16), tile_expert, meta, n_tiles,
                     tm=te, tf=512)
    return moe_combine(x2, ys, pos, route, tm=tm)


def kernel(x, l0_attn_norm, l0_rg_w_in, l0_rg_conv_w, l0_rg_conv_b, l0_rg_w_a, l0_rg_b_a, l0_rg_w_x, l0_rg_b_x, l0_rg_lambda, l0_rg_w_out, l0_ffn_norm, l0_ffn_w_gate_up, l0_ffn_w_down, l1_attn_norm, l1_gqa_w_qkv, l1_gqa_q_norm, l1_gqa_k_norm, l1_gqa_w_out, l1_ffn_norm, l1_moe_router, l1_moe_router_b, l1_moe_w_gate_up, l1_moe_w_down, l2_attn_norm, l2_dil_w_qkv, l2_dil_q_norm, l2_dil_k_norm, l2_dil_w_out, l2_ffn_norm, l2_ffn_w_gate_up, l2_ffn_w_down, l3_attn_norm, l3_rg_w_in, l3_rg_conv_w, l3_rg_conv_b, l3_rg_w_a, l3_rg_b_a, l3_rg_w_x, l3_rg_b_x, l3_rg_lambda, l3_rg_w_out, l3_ffn_norm, l3_moe_router, l3_moe_router_b, l3_moe_w_gate_up, l3_moe_w_down):
    batch, seq, d = x.shape
    x2 = x.reshape(batch * seq, d)
    x2 = _rglru_layer(x2, batch, seq, l0_attn_norm, l0_rg_w_in, l0_rg_conv_w, l0_rg_conv_b, l0_rg_w_a,
                      l0_rg_b_a, l0_rg_w_x, l0_rg_b_x, l0_rg_lambda, l0_rg_w_out)
    x2 = _dense_ffn_layer(x2, l0_ffn_norm, l0_ffn_w_gate_up, l0_ffn_w_down)
    x2 = _gqa_layer(x2, batch, seq, l1_attn_norm, l1_gqa_w_qkv, l1_gqa_q_norm, l1_gqa_k_norm, l1_gqa_w_out)
    x2 = _moe_layer(x2, l1_ffn_norm, l1_moe_router, l1_moe_router_b, l1_moe_w_gate_up, l1_moe_w_down)
    x2 = _dilated_layer(x2, batch, seq, l2_attn_norm, l2_dil_w_qkv, l2_dil_q_norm, l2_dil_k_norm, l2_dil_w_out)
    x2 = _dense_ffn_layer(x2, l2_ffn_norm, l2_ffn_w_gate_up, l2_ffn_w_down)
    x2 = _rglru_layer(x2, batch, seq, l3_attn_norm, l3_rg_w_in, l3_rg_conv_w, l3_rg_conv_b, l3_rg_w_a,
                      l3_rg_b_a, l3_rg_w_x, l3_rg_b_x, l3_rg_lambda, l3_rg_w_out)
    x2 = _moe_layer(x2, l3_ffn_norm, l3_moe_router, l3_moe_router_b, l3_moe_w_gate_up, l3_moe_w_down)
    return x2.reshape(batch, seq, d)
```

```python
import functools

import jax
import jax.numpy as jnp
import numpy as np
from jax import lax
from jax.experimental import pallas as pl
from jax.experimental.pallas import tpu as pltpu

F32 = jnp.float32
BF16 = jnp.bfloat16

EPS = 1e-6
NEG_INF = -1e30
GRID_W = 64
HEAD_DIM = 128
LANES = 128
RG_BLOCK_W = 128
CONV_W = 4
CONV_LEFT = 2
RG_C = 8.0
GQA_HEADS = 8
GQA_KV_HEADS = 2
GQA_GROUP = GQA_HEADS // GQA_KV_HEADS
AXIAL_THETA = 10000.0
DIL_PATTERNS = ((128, 1), (512, 4), (2048, 16))
DIL_HEADS = 8
ROPE_THETA = 500000.0
ROT_DIMS = HEAD_DIM // 4
N_EXPERTS = 8
TOP_K = 2
VMEM_LIMIT = 56 * 1024 * 1024

NT_DIMS = (((1,), (1,)), ((), ()))
LOG2_E = 1.4426950408889634


def _params(sem, vmem=VMEM_LIMIT):
    return pltpu.CompilerParams(dimension_semantics=sem, vmem_limit_bytes=vmem)


def _rms(x, g):
    return x * lax.rsqrt(jnp.mean(x * x, axis=-1, keepdims=True) + EPS) * g


def _norm_matmul_kernel(x_ref, g_ref, w_ref, o_ref, h_scr):
    @pl.when(pl.program_id(2) == 0)
    def _():
        h_scr[...] = _rms(x_ref[...], g_ref[...]).astype(BF16)

    o_ref[...] = jnp.dot(h_scr[...], w_ref[...], preferred_element_type=F32).astype(o_ref.dtype)


def norm_matmul_time_major(x2, gain, w, batch, seq, *, tm, tn, out_dtype):
    d = x2.shape[1]
    n = w.shape[1]
    ns, nj = seq // tm, n // tn
    return pl.pallas_call(
        _norm_matmul_kernel,
        grid=(batch, ns, nj),
        in_specs=[
            pl.BlockSpec((tm, d), lambda b, s, j: (b * ns + s, 0)),
            pl.BlockSpec((1, d), lambda b, s, j: (0, 0)),
            pl.BlockSpec((d, tn), lambda b, s, j: (0, j)),
        ],
        out_specs=pl.BlockSpec((tm, tn), lambda b, s, j: (s, b * nj + j)),
        out_shape=jax.ShapeDtypeStruct((seq, batch * n), out_dtype),
        scratch_shapes=[pltpu.VMEM((tm, d), BF16)],
        compiler_params=_params(("parallel", "parallel", "arbitrary")),
        name="norm_matmul_tm",
    )(x2, gain, w)


ROPE_SHIFT = LANES // 2


PAIR = 2 * LANES
MODE_PLAIN, MODE_QUERY, MODE_KEY = 0, 1, 2


def _qkv_kernel(x_ref, g_ref, w_ref, cos_ref, sin_ref, qg_ref, kg_ref, o_ref, h_scr, *stage,
                scale, dil, pair_modes):
    tm = x_ref.shape[0]
    per = tm // dil
    hn = _rms(x_ref[...], g_ref[...])
    if dil == 1:
        h_scr[...] = hn.astype(BF16)
    else:
        stage_scr, = stage
        for c in range(hn.shape[1] // LANES):
            sl = slice(c * LANES, (c + 1) * LANES)
            stage_scr[c] = hn[:, sl]
            for r in range(dil):
                h_scr[r * per:(r + 1) * per, sl] = stage_scr[c, pl.ds(r, per, stride=dil), :].astype(BF16)
    h = h_scr[...]

    row = lax.broadcasted_iota(jnp.int32, (PAIR, PAIR), 0)
    col = lax.broadcasted_iota(jnp.int32, (PAIR, PAIR), 1)
    head_ones = (row // LANES == col // LANES).astype(BF16)
    cos = cos_ref[...]
    sin = sin_ref[...]
    tables = {}
    for mode, gain, sc in ((MODE_QUERY, qg_ref[...], scale), (MODE_KEY, kg_ref[...], 1.0)):
        if mode in pair_modes:
            tables[mode] = (cos * (gain * sc), sin * (pltpu.roll(gain, ROPE_SHIFT, 1) * sc))

    accs = [jnp.dot(h, w_ref[:, i * PAIR:(i + 1) * PAIR], preferred_element_type=F32)
            for i in range(len(pair_modes))]
    sumsqs = [None if mode == MODE_PLAIN else
              jnp.dot((a * a).astype(BF16), head_ones, preferred_element_type=F32)
              for a, mode in zip(accs, pair_modes)]
    for i, (a, sumsq, mode) in enumerate(zip(accs, sumsqs, pair_modes)):
        for c in range(PAIR // LANES):
            sl = slice(c * LANES, (c + 1) * LANES)
            head = a[:, sl]
            if mode != MODE_PLAIN:
                c1, s1 = tables[mode]
                inv = lax.rsqrt(sumsq[:, sl] * (1.0 / HEAD_DIM) + EPS)
                head = (head * c1 + pltpu.roll(head, ROPE_SHIFT, 1) * s1) * inv
            out_sl = slice(i * PAIR + c * LANES, i * PAIR + (c + 1) * LANES)
            o_ref[:, :, out_sl] = head.reshape(dil, per, LANES).astype(o_ref.dtype)


def _tile_residue_major(table, dil, tm):
    s, w = table.shape
    return table.reshape(s // tm, tm // dil, dil, w).transpose(0, 2, 1, 3).reshape(s, w)


def qkv_project(x2, gain, w, pair_modes, cos, sin, q_gain, k_gain, batch, seq, *, tm, scale, dil):
    t, d = x2.shape
    n = w.shape[1]
    assert n == len(pair_modes) * PAIR
    ns = seq // tm
    per = tm // dil
    cos_r = _tile_residue_major(cos, dil, tm)
    sin_r = _tile_residue_major(sin, dil, tm)
    table_spec = pl.BlockSpec((tm, LANES), lambda i: (i % ns, 0))
    vec_spec = pl.BlockSpec((1, LANES), lambda i: (0, 0))
    out = pl.pallas_call(
        functools.partial(_qkv_kernel, scale=scale, dil=dil, pair_modes=tuple(pair_modes)),
        grid=(t // tm,),
        in_specs=[
            pl.BlockSpec((tm, d), lambda i: (i, 0)),
            pl.BlockSpec((1, d), lambda i: (0, 0)),
            pl.BlockSpec((d, n), lambda i: (0, 0)),
            table_spec, table_spec, vec_spec, vec_spec,
        ],
        out_specs=pl.BlockSpec((None, dil, None, per, n), lambda i: (i // ns, 0, i % ns, 0, 0)),
        out_shape=jax.ShapeDtypeStruct((batch, dil, ns, per, n), BF16),
        scratch_shapes=[pltpu.VMEM((tm, d), BF16)]
        + ([pltpu.VMEM((d // LANES, tm, LANES), F32)] if dil > 1 else []),
        compiler_params=_params(("parallel",)),
        name=f"qkv_project_d{dil}",
    )(x2, gain, w, cos_r, sin_r, q_gain, k_gain)
    return out.reshape(batch, dil, seq // dil, n)


def _matmul_residual_kernel(a_ref, w_ref, x_ref, o_ref):
    o_ref[...] = x_ref[...] + jnp.dot(a_ref[...], w_ref[...], preferred_element_type=F32)


def matmul_residual(a, w, x2, *, tm):
    t, k = a.shape
    n = w.shape[1]
    return pl.pallas_call(
        _matmul_residual_kernel,
        grid=(t // tm,),
        in_specs=[
            pl.BlockSpec((tm, k), lambda i: (i, 0)),
            pl.BlockSpec((k, n), lambda i: (0, 0)),
            pl.BlockSpec((tm, n), lambda i: (i, 0)),
        ],
        out_specs=pl.BlockSpec((tm, n), lambda i: (i, 0)),
        out_shape=jax.ShapeDtypeStruct((t, n), F32),
        compiler_params=_params(("parallel",)),
        name="matmul_residual",
    )(a, w, x2)


def _gqa_attn_kernel(q_ref, k_ref, v_ref, o_ref, *, tk, streams):
    tq = q_ref.shape[1]
    seq = k_ref.shape[1]
    per = GQA_GROUP // streams
    qs = [jnp.concatenate([q_ref[0, :, g * LANES:(g + 1) * LANES] for g in range(st * per, (st + 1) * per)],
                          axis=0) for st in range(streams)]
    rows = per * tq
    ones = jnp.ones((tk, LANES), BF16)

    def body(c, carry):
        start = pl.multiple_of(c * tk, tk)
        k = k_ref[0, pl.ds(start, tk), :]
        v1 = jnp.concatenate([v_ref[0, pl.ds(start, tk), :], ones], axis=1)
        new = []
        for st in range(streams):
            m_prev, acc = carry[st]
            s = lax.dot_general(qs[st], k, NT_DIMS, preferred_element_type=F32)
            m_new = jnp.maximum(m_prev, jnp.max(s, axis=-1, keepdims=True))
            alpha = jnp.exp2(m_prev - m_new)
            p = jnp.exp2(s - m_new).astype(BF16)
            acc = alpha * acc + jnp.dot(p, v1, preferred_element_type=F32)
            new.append((m_new, acc))
        return tuple(new)

    init = tuple((jnp.full((rows, 1), NEG_INF, F32), jnp.zeros((rows, 2 * LANES), F32)) for _ in range(streams))
    final = lax.fori_loop(0, seq // tk, body, init, unroll=16)
    for st in range(streams):
        acc = final[st][1]
        o = acc[:, :LANES] / acc[:, LANES:]
        for gi in range(per):
            g = st * per + gi
            o_ref[0, :, g * LANES:(g + 1) * LANES] = o[gi * tq:(gi + 1) * tq].astype(o_ref.dtype)


def gqa_attention(qkv, *, tq, tk, streams):
    b, s, _ = qkv.shape
    gw = GQA_GROUP * HEAD_DIM
    k_off = GQA_HEADS
    v_off = GQA_HEADS + GQA_KV_HEADS
    return pl.pallas_call(
        functools.partial(_gqa_attn_kernel, tk=tk, streams=streams),
        grid=(b, GQA_KV_HEADS, s // tq),
        in_specs=[
            pl.BlockSpec((1, tq, gw), lambda bb, n, i: (bb, i, n)),
            pl.BlockSpec((1, s, HEAD_DIM), lambda bb, n, i: (bb, 0, k_off + n)),
            pl.BlockSpec((1, s, HEAD_DIM), lambda bb, n, i: (bb, 0, v_off + n)),
        ],
        out_specs=pl.BlockSpec((1, tq, gw), lambda bb, n, i: (bb, i, n)),
        out_shape=jax.ShapeDtypeStruct((b, s, GQA_HEADS * HEAD_DIM), BF16),
        compiler_params=_params(("parallel", "parallel", "arbitrary")),
        name="gqa_attention",
    )(qkv, qkv, qkv)


def _dil_attn_kernel(q_ref, kp_ref, kc_ref, kn_ref, vp_ref, vc_ref, vn_ref, o_ref, lse_ref,
                     *, span, m_total):
    i = pl.program_id(2)
    tq = q_ref.shape[0]
    qb = 2 * span
    nk = qb + 2 * span
    row = lax.broadcasted_iota(jnp.int32, (qb, nk), 0)
    col = lax.broadcasted_iota(jnp.int32, (qb, nk), 1)
    in_band = jnp.abs(col - span - row) <= span
    lane = lax.broadcasted_iota(jnp.int32, (qb, LANES), 1)
    ones = jnp.ones((nk, LANES), BF16)
    for sub in range(tq // qb):
        key_idx = i * tq + sub * qb - span + col
        ok = in_band & (key_idx >= 0) & (key_idx < m_total)
        lse_tile = jnp.zeros((qb, LANES), F32)
        rows = slice(sub * qb, (sub + 1) * qb)
        for h in range(DIL_HEADS):
            sl = slice(h * LANES, (h + 1) * LANES)
            k = jnp.concatenate([kp_ref[:, sl], kc_ref[:, sl], kn_ref[:, sl]], axis=0)[sub * qb:sub * qb + nk]
            v = jnp.concatenate([vp_ref[:, sl], vc_ref[:, sl], vn_ref[:, sl]], axis=0)[sub * qb:sub * qb + nk]
            s = lax.dot_general(q_ref[rows, sl], k, NT_DIMS, preferred_element_type=F32)
            s = jnp.where(ok, s, NEG_INF)
            m = jnp.max(s, axis=-1, keepdims=True)
            p = jnp.exp2(s - m).astype(BF16)
            acc = jnp.dot(p, jnp.concatenate([v, ones], axis=1), preferred_element_type=F32)
            o_ref[rows, sl] = (acc[:, :LANES] / acc[:, LANES:]).astype(o_ref.dtype)
            lse_tile = jnp.where(lane == h, m + jnp.log2(acc[:, LANES:LANES + 1]), lse_tile)
        lse_ref[rows, :] = lse_tile


def dilated_attention(qkv, span, *, tq):
    b, dil, m_total, n = qkv.shape
    hw = DIL_HEADS * HEAD_DIM
    nb = m_total // tq
    halo_per_tile = tq // span
    n_halo = m_total // span

    def tile(col):
        return pl.BlockSpec((None, None, tq, hw), lambda bb, r, i: (bb, r, i, col))

    def before(col):
        return pl.BlockSpec((None, None, span, hw),
                            lambda bb, r, i: (bb, r, jnp.maximum(i * halo_per_tile - 1, 0), col))

    def after(col):
        return pl.BlockSpec((None, None, span, hw),
                            lambda bb, r, i: (bb, r, jnp.minimum((i + 1) * halo_per_tile, n_halo - 1), col))

    return pl.pallas_call(
        functools.partial(_dil_attn_kernel, span=span, m_total=m_total),
        grid=(b, dil, nb),
        in_specs=[tile(0), before(1), tile(1), after(1), before(2), tile(2), after(2)],
        out_specs=[
            pl.BlockSpec((None, None, tq, hw), lambda bb, r, i: (bb, r, i, 0)),
            pl.BlockSpec((None, None, tq, LANES), lambda bb, r, i: (bb, r, i, 0)),
        ],
        out_shape=[
            jax.ShapeDtypeStruct((b, dil, m_total, hw), BF16),
            jax.ShapeDtypeStruct((b, dil, m_total, LANES), F32),
        ],
        compiler_params=_params(("parallel", "parallel", "arbitrary")),
        name=f"dilated_attention_d{dil}",
    )(qkv, qkv, qkv, qkv, qkv, qkv, qkv)


def _dil_merge_kernel(o0_ref, o1_ref, o2_ref, l0_ref, l1_ref, l2_ref, w_ref, x_ref, out_ref, o_scr, l_scr):
    def to_tokens(ref, scr, g):
        dil, per, width = ref.shape
        for c in range(width // LANES):
            sl = slice(c * LANES, (c + 1) * LANES)
            if dil == 1:
                scr[g, c] = ref[0, :, sl].astype(F32)
            else:
                for r in range(dil):
                    scr[g, c, pl.ds(r, per, stride=dil), :] = ref[r, :, sl].astype(F32)

    for g, (o_ref, l_ref) in enumerate(((o0_ref, l0_ref), (o1_ref, l1_ref), (o2_ref, l2_ref))):
        to_tokens(o_ref, o_scr, g)
        to_tokens(l_ref, l_scr, g)
    lses = [l_scr[g, 0] for g in range(3)]
    mx = jnp.maximum(jnp.maximum(lses[0], lses[1]), lses[2])
    es = [jnp.exp2(l - mx) for l in lses]
    den = es[0] + es[1] + es[2]
    ws = [e / den for e in es]
    parts = []
    for h in range(DIL_HEADS):
        acc = None
        for g in range(3):
            term = o_scr[g, h] * ws[g][:, h:h + 1]
            acc = term if acc is None else acc + term
        parts.append(acc.astype(BF16))
    merged = jnp.concatenate(parts, axis=1)
    out_ref[...] = x_ref[...] + jnp.dot(merged, w_ref[...], preferred_element_type=F32)


def dilated_merge_project(outs, lses, w_out, x2, batch, seq, *, tm):
    t, d = x2.shape
    hw = w_out.shape[0]
    ns = seq // tm

    def grouped(arr):
        _, dil, m_total, w = arr.shape
        per = tm // dil
        view = arr.reshape(batch, dil, ns, per, w)
        return view, pl.BlockSpec((None, dil, None, per, w), lambda i: (i // ns, 0, i % ns, 0, 0))

    views, specs = zip(*[grouped(a) for a in (*outs, *lses)])
    row = pl.BlockSpec((tm, d), lambda i: (i, 0))
    return pl.pallas_call(
        _dil_merge_kernel,
        grid=(t // tm,),
        in_specs=list(specs) + [pl.BlockSpec((hw, d), lambda i: (0, 0)), row],
        out_specs=row,
        out_shape=jax.ShapeDtypeStruct((t, d), F32),
        scratch_shapes=[pltpu.VMEM((3, hw // LANES, tm, LANES), F32), pltpu.VMEM((3, 1, tm, LANES), F32)],
        compiler_params=_params(("parallel",)),
        name="dilated_merge_project",
    )(*views, w_out, x2)


def _rglru_scan_kernel(xc_ref, xp_ref, xn_ref, cw_ref, cb_ref, wa_ref, ba_ref, wx_ref, bx_ref, lam_ref,
                       o_ref, a_scr, u_scr, h_scr, *, reverse, n_steps):
    s = pl.program_id(1)
    ts, nb, tc = xc_ref.shape
    chunk = (n_steps - 1 - s) if reverse else s

    @pl.when(s == 0)
    def _():
        h_scr[...] = jnp.zeros(h_scr.shape, F32)

    prev = jnp.where(chunk == 0, 0.0, xp_ref[...])
    nxt = jnp.where(chunk == n_steps - 1, 0.0, xn_ref[...])
    xe = jnp.concatenate([prev, xc_ref[...], nxt], axis=0)
    cw = cw_ref[...]
    xc = cb_ref[...].reshape(1, 1, tc) + xe[0:ts] * cw[0:1].reshape(1, 1, tc)
    for j in range(1, CONV_W):
        xc = xc + xe[j:j + ts] * cw[j:j + 1].reshape(1, 1, tc)
    xc2 = xc.reshape(ts * nb, tc)
    lam = lam_ref[...]
    softplus_neg_lam = jnp.maximum(-lam, 0.0) + jnp.log1p(jnp.exp(-jnp.abs(lam)))
    decay_log2 = (-RG_C * LOG2_E) * softplus_neg_lam

    def sigmoid(x16, w_ref, bias):
        half_z = jnp.dot(x16, w_ref[kb] * 0.5, preferred_element_type=F32) + 0.5 * bias
        return 0.5 * jnp.tanh(half_z) + 0.5

    for kb in range(tc // RG_BLOCK_W):
        sl = slice(kb * RG_BLOCK_W, (kb + 1) * RG_BLOCK_W)
        xb = xc2[:, sl]
        xb16 = xb.astype(BF16)
        r = sigmoid(xb16, wa_ref, ba_ref[:, sl])
        gate = sigmoid(xb16, wx_ref, bx_ref[:, sl])
        a = jnp.exp2(r * decay_log2[:, sl])
        z = 1.0 - a * a
        u = (z * lax.rsqrt(jnp.maximum(z, 1e-30))) * (gate * xb)
        a_scr[:, :, sl] = a.reshape(ts, nb, RG_BLOCK_W)
        u_scr[:, :, sl] = u.reshape(ts, nb, RG_BLOCK_W)

    def step(t, h):
        tt = (ts - 1 - t) if reverse else t
        h = a_scr[tt] * h + u_scr[tt]
        o_ref[tt] = h
        return h

    h_scr[...] = lax.fori_loop(0, ts, step, h_scr[...], unroll=8)


def rglru_scan(yx3, conv_w, conv_b, w_a, b_a, w_x, b_x, lam, *, reverse, ts, tc):
    seq, nb, two_c = yx3.shape
    c = two_c // 2
    nc, n_steps = c // tc, seq // ts
    kb = tc // RG_BLOCK_W

    def chunk_of(s):
        return (n_steps - 1 - s) if reverse else s

    vec = lambda rows: pl.BlockSpec((rows, tc), lambda ci, s: (0, ci))
    mat = pl.BlockSpec((kb, RG_BLOCK_W, RG_BLOCK_W), lambda ci, s: (ci, 0, 0))
    return pl.pallas_call(
        functools.partial(_rglru_scan_kernel, reverse=reverse, n_steps=n_steps),
        grid=(nc, n_steps),
        in_specs=[
            pl.BlockSpec((ts, nb, tc), lambda ci, s: (chunk_of(s), 0, nc + ci)),
            pl.BlockSpec((CONV_LEFT, nb, tc),
                         lambda ci, s: (jnp.maximum(chunk_of(s) * (ts // CONV_LEFT) - 1, 0), 0, nc + ci)),
            pl.BlockSpec((1, nb, tc),
                         lambda ci, s: (jnp.minimum((chunk_of(s) + 1) * ts, seq - 1), 0, nc + ci)),
            vec(CONV_W), vec(1), mat, vec(1), mat, vec(1), vec(1),
        ],
        out_specs=pl.BlockSpec((ts, nb, tc), lambda ci, s: (chunk_of(s), 0, ci)),
        out_shape=jax.ShapeDtypeStruct((seq, nb, c), F32),
        scratch_shapes=[
            pltpu.VMEM((ts, nb, tc), F32),
            pltpu.VMEM((ts, nb, tc), F32),
            pltpu.VMEM((nb, tc), F32),
        ],
        compiler_params=_params(("parallel", "arbitrary")),
        name="rglru_scan_bwd" if reverse else "rglru_scan_fwd",
    )(yx3, yx3, yx3, conv_w, conv_b, w_a, b_a, w_x, b_x, lam)


def _gelu_tanh(y):
    return y * (0.5 * (1.0 + jnp.tanh(0.7978845608028654 * (y + 0.044715 * (y * y * y)))))


def _rglru_out_kernel(y_ref, hf_ref, hb_ref, w_ref, x_ref, o_ref):
    gated = _gelu_tanh(y_ref[...]) * (hf_ref[...] + hb_ref[...])
    o_ref[...] = x_ref[...] + jnp.dot(gated.astype(BF16), w_ref[...], preferred_element_type=F32)


def rglru_out_project(yx, hf, hb, w_out, x2, batch, seq, *, tm):
    c, d = w_out.shape
    ns = seq // tm
    return pl.pallas_call(
        _rglru_out_kernel,
        grid=(batch, ns),
        in_specs=[
            pl.BlockSpec((tm, c), lambda b, s: (s, 2 * b)),
            pl.BlockSpec((tm, c), lambda b, s: (s, b)),
            pl.BlockSpec((tm, c), lambda b, s: (s, b)),
            pl.BlockSpec((c, d), lambda b, s: (0, 0)),
            pl.BlockSpec((tm, d), lambda b, s: (b * ns + s, 0)),
        ],
        out_specs=pl.BlockSpec((tm, d), lambda b, s: (b * ns + s, 0)),
        out_shape=jax.ShapeDtypeStruct(x2.shape, F32),
        compiler_params=_params(("parallel", "parallel")),
        name="rglru_out_project",
    )(yx, hf, hb, w_out, x2)


def _swiglu_step(h, wg_ref, wu_ref, wd_ref, acc_scr):
    gate = jnp.dot(h, wg_ref[...].astype(BF16), preferred_element_type=F32)
    up = jnp.dot(h, wu_ref[...].astype(BF16), preferred_element_type=F32)
    act = gate * jax.nn.sigmoid(gate) * up
    acc_scr[...] += jnp.dot(act.astype(BF16), wd_ref[...].astype(BF16), preferred_element_type=F32)


def _ffn_kernel(x_ref, g_ref, wg_ref, wu_ref, wd_ref, o_ref, h_scr, acc_scr):
    f = pl.program_id(1)

    @pl.when(f == 0)
    def _():
        h_scr[...] = _rms(x_ref[...], g_ref[...]).astype(BF16)
        acc_scr[...] = jnp.zeros(acc_scr.shape, F32)

    _swiglu_step(h_scr[...], wg_ref, wu_ref, wd_ref, acc_scr)

    @pl.when(f == pl.num_programs(1) - 1)
    def _():
        o_ref[...] = x_ref[...] + acc_scr[...]


def swiglu_ffn(x2, gain, w_gate_up, w_down, *, tm, tf):
    t, d = x2.shape
    ff = w_down.shape[0]
    nf = ff // tf
    return pl.pallas_call(
        _ffn_kernel,
        grid=(t // tm, nf),
        in_specs=[
            pl.BlockSpec((tm, d), lambda i, f: (i, 0)),
            pl.BlockSpec((1, d), lambda i, f: (0, 0)),
            pl.BlockSpec((d, tf), lambda i, f: (0, f)),
            pl.BlockSpec((d, tf), lambda i, f: (0, nf + f)),
            pl.BlockSpec((tf, d), lambda i, f: (f, 0)),
        ],
        out_specs=pl.BlockSpec((tm, d), lambda i, f: (i, 0)),
        out_shape=jax.ShapeDtypeStruct((t, d), F32),
        scratch_shapes=[pltpu.VMEM((tm, d), BF16), pltpu.VMEM((tm, d), F32)],
        compiler_params=_params(("parallel", "arbitrary")),
        name="dense_ffn",
    )(x2, gain, w_gate_up, w_gate_up, w_down)


SUBLANES = 8


def _copy_token(src_hbm, dst, src_tok, dst_tok, sem):
    return pltpu.make_async_copy(src_hbm.at[pl.ds(src_tok * SUBLANES, SUBLANES)],
                                 dst.at[pl.ds(dst_tok * SUBLANES, SUBLANES)], sem)


def _moe_scatter_kernel(zero_ref, pos_ref, x_ref, o_hbm, stage, zeros_scr, sem, *, tm):
    p = pl.program_id(0)
    n = pl.num_programs(0)
    slot = p % 2
    chunks = x_ref.shape[1] // LANES
    block_rows = tm * SUBLANES

    def zero_copy(j):
        dst = o_hbm.at[pl.ds(zero_ref[j] * SUBLANES, zeros_scr.shape[0])]
        return pltpu.make_async_copy(zeros_scr, dst, sem.at[2])

    @pl.when(p == 0)
    def _():
        zeros_scr[...] = jnp.zeros(zeros_scr.shape, F32)
        for j in range(2 * N_EXPERTS):
            zero_copy(j).start()
            zero_copy(j).wait()

    def slot_wait(s):
        rows = pl.ds(0, 2 * block_rows)
        pltpu.make_async_copy(o_hbm.at[rows], o_hbm.at[rows], sem.at[s]).wait()

    @pl.when(p >= 2)
    def _():
        slot_wait(slot)

    for c in range(chunks):
        stage[slot, pl.ds(c, tm, stride=SUBLANES), :] = x_ref[:, c * LANES:(c + 1) * LANES]

    def issue(r, carry):
        src = stage.at[slot, pl.ds(r * SUBLANES, SUBLANES)]
        for k in range(TOP_K):
            dst = o_hbm.at[pl.ds(pos_ref[0, k * tm + r] * SUBLANES, SUBLANES)]
            pltpu.make_async_copy(src, dst, sem.at[slot]).start()
        return carry
    lax.fori_loop(0, tm, issue, 0, unroll=8)

    @pl.when(p == n - 1)
    def _():
        slot_wait(slot)

        @pl.when(n >= 2)
        def _():
            slot_wait(1 - slot)


def moe_scatter(x2, pos, zero_starts, n_tiles, *, tm, te):
    t, d = x2.shape
    grid_spec = pltpu.PrefetchScalarGridSpec(
        num_scalar_prefetch=1,
        grid=(t // tm,),
        in_specs=[
            pl.BlockSpec((None, 1, TOP_K * tm), lambda p, z: (p, 0, 0), memory_space=pltpu.SMEM),
            pl.BlockSpec((tm, d), lambda p, z: (p, 0)),
        ],
        out_specs=pl.BlockSpec(memory_space=pl.ANY),
        scratch_shapes=[pltpu.VMEM((2, tm * SUBLANES, LANES), F32), pltpu.VMEM((te * SUBLANES, LANES), F32),
                        pltpu.SemaphoreType.DMA((3,))],
    )
    return pl.pallas_call(
        functools.partial(_moe_scatter_kernel, tm=tm),
        grid_spec=grid_spec,
        out_shape=jax.ShapeDtypeStruct(((n_tiles + 1) * te * SUBLANES, LANES), F32),
        compiler_params=_params(("arbitrary",)),
        name="moe_scatter",
    )(zero_starts, pos, x2)


def _moe_expert_kernel(te_ref, meta_ref, xs_ref, g_ref, wg_ref, wu_ref, wd_ref, y_ref, h_scr, acc_scr):
    p = pl.program_id(0)
    f = pl.program_id(1)
    tm = acc_scr.shape[0]
    chunks = acc_scr.shape[1] // LANES
    used = p < meta_ref[0]

    @pl.when(used & (f == 0))
    def _():
        x = jnp.concatenate([xs_ref[pl.ds(c, tm, stride=SUBLANES), :] for c in range(chunks)], axis=1)
        h_scr[...] = _rms(x, g_ref[...]).astype(BF16)
        acc_scr[...] = jnp.zeros(acc_scr.shape, F32)

    @pl.when(used)
    def _():
        _swiglu_step(h_scr[...], wg_ref, wu_ref, wd_ref, acc_scr)

    @pl.when(f == pl.num_programs(1) - 1)
    def _():
        y = jnp.where(used, acc_scr[...], 0.0)
        for c in range(chunks):
            y_ref[pl.ds(c, tm, stride=SUBLANES), :] = y[:, c * LANES:(c + 1) * LANES]


def moe_experts(xs, gain, w_gate_up, w_down, tile_expert, meta, n_tiles, *, tm, tf):
    n_e, ff, d = w_down.shape
    nf = ff // tf

    def f_eff(p, f, meta_ref):
        return jnp.where(p < meta_ref[0], f, nf - 1)

    grid_spec = pltpu.PrefetchScalarGridSpec(
        num_scalar_prefetch=2,
        grid=(n_tiles, nf),
        in_specs=[
            pl.BlockSpec((tm * SUBLANES, LANES), lambda p, f, te, m: (p, 0)),
            pl.BlockSpec((1, d), lambda p, f, te, m: (0, 0)),
            pl.BlockSpec((None, d, tf), lambda p, f, te, m: (te[p], 0, f_eff(p, f, m))),
            pl.BlockSpec((None, d, tf), lambda p, f, te, m: (te[p], 0, nf + f_eff(p, f, m))),
            pl.BlockSpec((None, tf, d), lambda p, f, te, m: (te[p], f_eff(p, f, m), 0)),
        ],
        out_specs=pl.BlockSpec((tm * SUBLANES, LANES), lambda p, f, te, m: (p, 0)),
        scratch_shapes=[pltpu.VMEM((tm, d), BF16), pltpu.VMEM((tm, d), F32)],
    )
    return pl.pallas_call(
        _moe_expert_kernel,
        grid_spec=grid_spec,
        out_shape=jax.ShapeDtypeStruct((n_tiles * tm * SUBLANES, LANES), F32),
        compiler_params=_params(("parallel", "arbitrary")),
        name="moe_experts",
    )(tile_expert, meta, xs, gain, w_gate_up, w_gate_up, w_down)


def _moe_combine_kernel(cur_ref, nxt_ref, x_ref, route_ref, y_hbm, o_ref, buf, sem, *, tm):
    p = pl.program_id(0)
    n = pl.num_programs(0)
    chunks = x_ref.shape[1] // LANES

    def issue(idx_ref, slot):
        def body(r, carry):
            _copy_token(y_hbm, buf.at[slot], idx_ref[0, r], r, sem.at[slot]).start()
            return carry
        lax.fori_loop(0, 2 * tm, body, 0, unroll=8)

    @pl.when(p == 0)
    def _():
        issue(cur_ref, 0)

    @pl.when(p + 1 < n)
    def _():
        issue(nxt_ref, (p + 1) % 2)

    slot = p % 2
    pltpu.make_async_copy(y_hbm.at[pl.ds(0, 2 * tm * SUBLANES)], buf.at[slot], sem.at[slot]).wait()
    route = route_ref[...]
    g1 = route[:, 2:3]
    g2 = route[:, 3:4]
    for c in range(chunks):
        first = buf[slot, pl.ds(c, tm, stride=SUBLANES), :]
        second = buf[slot, pl.ds(tm * SUBLANES + c, tm, stride=SUBLANES), :]
        sl = slice(c * LANES, (c + 1) * LANES)
        o_ref[:, sl] = x_ref[:, sl] + (g1 * first + g2 * second)


def moe_combine(x2, y_tiles, pos, route, *, tm):
    t, d = x2.shape
    n = t // tm
    idx_spec = lambda shift: pl.BlockSpec((None, 1, 2 * tm), lambda p: (jnp.minimum(p + shift, n - 1), 0, 0),
                                          memory_space=pltpu.SMEM)
    return pl.pallas_call(
        functools.partial(_moe_combine_kernel, tm=tm),
        grid=(n,),
        in_specs=[idx_spec(0), idx_spec(1), pl.BlockSpec((tm, d), lambda p: (p, 0)),
                  pl.BlockSpec((tm, LANES), lambda p: (p, 0)), pl.BlockSpec(memory_space=pl.ANY)],
        out_specs=pl.BlockSpec((tm, d), lambda p: (p, 0)),
        out_shape=jax.ShapeDtypeStruct((t, d), F32),
        scratch_shapes=[pltpu.VMEM((2, 2 * tm * SUBLANES, LANES), F32), pltpu.SemaphoreType.DMA((2,))],
        compiler_params=_params(("arbitrary",)),
        name="moe_combine",
    )(pos, pos, x2, route, y_tiles)


def _moe_plan(route, te, tm):
    t = route.shape[0]
    experts = route[:, :TOP_K].astype(jnp.int32)
    onehot = (experts[:, :, None] == jnp.arange(N_EXPERTS)[None, None, :]).astype(jnp.int32).sum(axis=1)
    incl = jnp.cumsum(onehot, axis=0)
    rank = incl - onehot
    counts = incl[-1]
    tiles_per = (counts + te - 1) // te
    tile_end = jnp.cumsum(tiles_per)
    row_start = (tile_end - tiles_per) * te
    dest = row_start[experts] + jnp.take_along_axis(rank, experts, axis=1)
    n_tiles = -(-(TOP_K * t) // te) + N_EXPERTS
    tile_expert = jnp.minimum(jnp.searchsorted(tile_end, jnp.arange(n_tiles), side="right"),
                              N_EXPERTS - 1).astype(jnp.int32)
    n_used = tile_end[-1]
    tail = jnp.minimum((n_used + jnp.arange(N_EXPERTS)) * te, n_tiles * te)
    zero_starts = jnp.concatenate([row_start + counts, tail]).astype(jnp.int32)
    pos = jnp.concatenate([dest[:, k].reshape(t // tm, tm) for k in range(TOP_K)], axis=1)
    return (pos.reshape(t // tm, 1, TOP_K * tm).astype(jnp.int32), zero_starts, tile_expert,
            n_used.reshape(1).astype(jnp.int32), n_tiles)


def _router_kernel(x_ref, g_ref, r_ref, b_ref, c_ref):
    h = _rms(x_ref[...], g_ref[...])
    logits = jnp.dot(h, r_ref[...], preferred_element_type=F32, precision=lax.Precision.HIGHEST)
    logits = logits + b_ref[...]
    lane = lax.broadcasted_iota(jnp.int32, logits.shape, 1)
    logits = jnp.where(lane < N_EXPERTS, logits, NEG_INF)
    v1 = jnp.max(logits, axis=-1, keepdims=True)
    i1 = jnp.min(jnp.where(logits == v1, lane, LANES), axis=-1, keepdims=True)
    rest = jnp.where(lane == i1, NEG_INF, logits)
    v2 = jnp.max(rest, axis=-1, keepdims=True)
    i2 = jnp.min(jnp.where(rest == v2, lane, LANES), axis=-1, keepdims=True)
    e2 = jnp.exp(v2 - v1)
    g1 = 1.0 / (1.0 + e2)
    g2 = e2 / (1.0 + e2)
    c_ref[...] = jnp.where(lane == 0, i1.astype(F32), jnp.where(lane == 1, i2.astype(F32),
                           jnp.where(lane == 2, g1, jnp.where(lane == 3, g2, 0.0))))


def moe_router(x2, gain, router, router_b, *, tm):
    t, d = x2.shape
    r_pad = jnp.zeros((d, LANES), F32).at[:, :N_EXPERTS].set(router)
    b_pad = jnp.zeros((1, LANES), F32).at[0, :N_EXPERTS].set(router_b)
    return pl.pallas_call(
        _router_kernel,
        grid=(t // tm,),
        in_specs=[
            pl.BlockSpec((tm, d), lambda i: (i, 0)),
            pl.BlockSpec((1, d), lambda i: (0, 0)),
            pl.BlockSpec((d, LANES), lambda i: (0, 0)),
            pl.BlockSpec((1, LANES), lambda i: (0, 0)),
        ],
        out_specs=pl.BlockSpec((tm, LANES), lambda i: (i, 0)),
        out_shape=jax.ShapeDtypeStruct((t, LANES), F32),
        compiler_params=_params(("parallel",)),
        name="moe_router",
    )(x2, gain, r_pad, b_pad)


def _inv_freq(dim, theta):
    return theta ** (-jnp.arange(0, dim, 2, dtype=F32) / dim)


_Q = HEAD_DIM // 4
AXIAL_PERM = np.concatenate([np.arange(0, _Q), np.arange(2 * _Q, 3 * _Q), np.arange(_Q, 2 * _Q),
                             np.arange(3 * _Q, 4 * _Q)])
_R = ROT_DIMS // 2
_REST = (HEAD_DIM - ROT_DIMS) // 2
PARTIAL_PERM = np.concatenate([np.arange(0, _R), np.arange(ROT_DIMS, ROT_DIMS + _REST), np.arange(_R, ROT_DIMS),
                               np.arange(ROT_DIMS + _REST, HEAD_DIM)])


def _axial_tables(seq):
    t = jnp.arange(seq)
    rows = (t // GRID_W).astype(F32)
    cols = (t % GRID_W).astype(F32)
    freq = _inv_freq(HEAD_DIM // 2, AXIAL_THETA)
    ang_r = rows[:, None] * freq[None, :]
    ang_c = cols[:, None] * freq[None, :]
    ang = jnp.concatenate([ang_r, ang_c, ang_r, ang_c], axis=-1)
    sign = jnp.concatenate([-jnp.ones(ROPE_SHIFT, F32), jnp.ones(ROPE_SHIFT, F32)])
    return jnp.cos(ang), jnp.sin(ang) * sign[None, :]


def _partial_tables(seq):
    pos = jnp.arange(seq, dtype=F32)
    freq = _inv_freq(ROT_DIMS, ROPE_THETA)
    ang = pos[:, None] * freq[None, :]
    ones = jnp.ones((seq, _REST), F32)
    zeros = jnp.zeros((seq, _REST), F32)
    cos = jnp.concatenate([jnp.cos(ang), ones, jnp.cos(ang), ones], axis=-1)
    sin = jnp.concatenate([-jnp.sin(ang), zeros, jnp.sin(ang), zeros], axis=-1)
    return cos, sin


def _permute_qk_columns(w, head_is_qk, perm):
    idx = np.concatenate([h * HEAD_DIM + (perm if qk else np.arange(HEAD_DIM))
                          for h, qk in enumerate(head_is_qk)])
    return jnp.take(w, jnp.asarray(idx, jnp.int32), axis=1)


def _row(v):
    return v.reshape(1, -1).astype(F32)


def _rglru_layer(x2, batch, seq, norm, w_in, conv_w, conv_b, w_a, b_a, w_x, b_x, lam, w_out):
    c = w_out.shape[0]
    yx = norm_matmul_time_major(x2, _row(norm), w_in.astype(BF16), batch, seq,
                                tm=512, tn=2048, out_dtype=F32)
    yx3 = yx.reshape(seq, batch, 2 * c)
    hs = []
    for direction in range(2):
        hs.append(rglru_scan(
            yx3, conv_w, _row(conv_b), w_a[direction].astype(BF16), _row(b_a[direction]),
            w_x[direction].astype(BF16), _row(b_x[direction]), _row(lam[direction]),
            reverse=direction == 1, ts=256, tc=512).reshape(seq, batch * c))
    return rglru_out_project(yx, hs[0], hs[1], w_out.astype(BF16), x2, batch, seq, tm=512)


def _gqa_layer(x2, batch, seq, norm, w_qkv, q_gain, k_gain, w_out):
    nq = GQA_HEADS * HEAD_DIM
    nkv = GQA_KV_HEADS * HEAD_DIM
    modes = [MODE_QUERY] * (nq // PAIR) + [MODE_KEY] * (nkv // PAIR) + [MODE_PLAIN] * (nkv // PAIR)
    cos, sin = _axial_tables(seq)
    is_qk = [True] * (GQA_HEADS + GQA_KV_HEADS) + [False] * GQA_KV_HEADS
    w16 = _permute_qk_columns(w_qkv, is_qk, AXIAL_PERM).astype(BF16)
    qkv = qkv_project(x2, _row(norm), w16, modes, cos, sin, _row(q_gain[AXIAL_PERM]), _row(k_gain[AXIAL_PERM]),
                      batch, seq, tm=512, scale=HEAD_DIM ** -0.5 * LOG2_E, dil=1)
    o = gqa_attention(qkv.reshape(batch, seq, nq + 2 * nkv), tq=256, tk=512, streams=2)
    return matmul_residual(o.reshape(batch * seq, nq), w_out.astype(BF16), x2, tm=512)


def _dilated_layer(x2, batch, seq, norm, w_qkv, q_gain, k_gain, w_out):
    hw = DIL_HEADS * HEAD_DIM
    modes = [MODE_QUERY] * (hw // PAIR) + [MODE_KEY] * (hw // PAIR) + [MODE_PLAIN] * (hw // PAIR)
    cos, sin = _partial_tables(seq)
    is_qk = ([True] * (2 * DIL_HEADS) + [False] * DIL_HEADS) * len(DIL_PATTERNS)
    w16 = _permute_qk_columns(w_qkv, is_qk, PARTIAL_PERM).astype(BF16)
    outs, lses = [], []
    for g, (window, dil) in enumerate(DIL_PATTERNS):
        span = window // (2 * dil)
        assert seq % (dil * span) == 0
        qkv = qkv_project(x2, _row(norm), w16[:, g * 3 * hw:(g + 1) * 3 * hw], modes, cos, sin,
                          _row(q_gain[PARTIAL_PERM]), _row(k_gain[PARTIAL_PERM]), batch, seq,
                          tm=512, scale=HEAD_DIM ** -0.5 * LOG2_E, dil=dil)
        o, lse = dilated_attention(qkv, span, tq=min(4 * span, seq // dil))
        outs.append(o)
        lses.append(lse)
    return dilated_merge_project(outs, lses, w_out.astype(BF16), x2, batch, seq, tm=512)


def _dense_ffn_layer(x2, norm, w_gate_up, w_down):
    return swiglu_ffn(x2, _row(norm), w_gate_up.astype(BF16), w_down.astype(BF16), tm=1024, tf=512)


def _moe_layer(x2, norm, router, router_b, w_gate_up, w_down):
    tm, te = 512, 1024
    route = moe_router(x2, _row(norm), router, router_b, tm=tm)
    pos, zero_starts, tile_expert, meta, n_tiles = _moe_plan(route, te, tm)
    xs = moe_scatter(x2, pos, zero_starts, n_tiles, tm=tm, te=te)
    ys = moe_experts(xs, _row(norm), w_gate_up, w_down, tile_expert, meta, n_tiles,
                     tm=te, tf=512)
    return moe_combine(x2, ys, pos, route, tm=tm)


def kernel(x, l0_attn_norm, l0_rg_w_in, l0_rg_conv_w, l0_rg_conv_b, l0_rg_w_a, l0_rg_b_a, l0_rg_w_x, l0_rg_b_x, l0_rg_lambda, l0_rg_w_out, l0_ffn_norm, l0_ffn_w_gate_up, l0_ffn_w_down, l1_attn_norm, l1_gqa_w_qkv, l1_gqa_q_norm, l1_gqa_k_norm, l1_gqa_w_out, l1_ffn_norm, l1_moe_router, l1_moe_router_b, l1_moe_w_gate_up, l1_moe_w_down, l2_attn_norm, l2_dil_w_qkv, l2_dil_q_norm, l2_dil_k_norm, l2_dil_w_out, l2_ffn_norm, l2_ffn_w_gate_up, l2_ffn_w_down, l3_attn_norm, l3_rg_w_in, l3_rg_conv_w, l3_rg_conv_b, l3_rg_w_a, l3_rg_b_a, l3_rg_w_x, l3_rg_b_x, l3_rg_lambda, l3_rg_w_out, l3_ffn_norm, l3_moe_router, l3_moe_router_b, l3_moe_w_gate_up, l3_moe_w_down):
    batch, seq, d = x.shape
    x2 = x.reshape(batch * seq, d)
    x2 = _rglru_layer(x2, batch, seq, l0_attn_norm, l0_rg_w_in, l0_rg_conv_w, l0_rg_conv_b, l0_rg_w_a,
                      l0_rg_b_a, l0_rg_w_x, l0_rg_b_x, l0_rg_lambda, l0_rg_w_out)
    x2 = _dense_ffn_layer(x2, l0_ffn_norm, l0_ffn_w_gate_up, l0_ffn_w_down)
    x2 = _gqa_layer(x2, batch, seq, l1_attn_norm, l1_gqa_w_qkv, l1_gqa_q_norm, l1_gqa_k_norm, l1_gqa_w_out)
    x2 = _moe_layer(x2, l1_ffn_norm, l1_moe_router, l1_moe_router_b, l1_moe_w_gate_up, l1_moe_w_down)
    x2 = _dilated_layer(x2, batch, seq, l2_attn_norm, l2_dil_w_qkv, l2_dil_q_norm, l2_dil_k_norm, l2_dil_w_out)
    x2 = _dense_ffn_layer(x2, l2_ffn_norm, l2_ffn_w_gate_up, l2_ffn_w_down)
    x2 = _rglru_layer(x2, batch, seq, l3_attn_norm, l3_rg_w_in, l3_rg_conv_w, l3_rg_conv_b, l3_rg_w_a,
                      l3_rg_b_a, l3_rg_w_x, l3_rg_b_x, l3_rg_lambda, l3_rg_w_out)
    x2 = _moe_layer(x2, l3_ffn_norm, l3_moe_router, l3_moe_router_b, l3_moe_w_gate_up, l3_moe_w_down)
    return x2.reshape(batch, seq, d)
```

```python
import functools

import jax
import jax.numpy as jnp
import numpy as np
from jax import lax
from jax.experimental import pallas as pl
from jax.experimental.pallas import tpu as pltpu

F32 = jnp.float32
BF16 = jnp.bfloat16

EPS = 1e-6
NEG_INF = -1e30
GRID_W = 64
HEAD_DIM = 128
LANES = 128
RG_BLOCK_W = 128
CONV_W = 4
CONV_LEFT = 2
RG_C = 8.0
GQA_HEADS = 8
GQA_KV_HEADS = 2
GQA_GROUP = GQA_HEADS // GQA_KV_HEADS
AXIAL_THETA = 10000.0
DIL_PATTERNS = ((128, 1), (512, 4), (2048, 16))
DIL_HEADS = 8
ROPE_THETA = 500000.0
ROT_DIMS = HEAD_DIM // 4
N_EXPERTS = 8
TOP_K = 2
VMEM_LIMIT = 56 * 1024 * 1024

NT_DIMS = (((1,), (1,)), ((), ()))
LOG2_E = 1.4426950408889634


def _params(sem, vmem=VMEM_LIMIT):
    return pltpu.CompilerParams(dimension_semantics=sem, vmem_limit_bytes=vmem)


def _rms(x, g):
    return x * lax.rsqrt(jnp.mean(x * x, axis=-1, keepdims=True) + EPS) * g


def _norm_matmul_kernel(x_ref, g_ref, w_ref, o_ref, h_scr):
    @pl.when(pl.program_id(2) == 0)
    def _():
        h_scr[...] = _rms(x_ref[...], g_ref[...]).astype(BF16)

    o_ref[...] = jnp.dot(h_scr[...], w_ref[...], preferred_element_type=F32).astype(o_ref.dtype)


def norm_matmul_time_major(x2, gain, w, batch, seq, *, tm, tn, out_dtype):
    d = x2.shape[1]
    n = w.shape[1]
    ns, nj = seq // tm, n // tn
    return pl.pallas_call(
        _norm_matmul_kernel,
        grid=(batch, ns, nj),
        in_specs=[
            pl.BlockSpec((tm, d), lambda b, s, j: (b * ns + s, 0)),
            pl.BlockSpec((1, d), lambda b, s, j: (0, 0)),
            pl.BlockSpec((d, tn), lambda b, s, j: (0, j)),
        ],
        out_specs=pl.BlockSpec((tm, tn), lambda b, s, j: (s, b * nj + j)),
        out_shape=jax.ShapeDtypeStruct((seq, batch * n), out_dtype),
        scratch_shapes=[pltpu.VMEM((tm, d), BF16)],
        compiler_params=_params(("parallel", "parallel", "arbitrary")),
        name="norm_matmul_tm",
    )(x2, gain, w)


ROPE_SHIFT = LANES // 2


PAIR = 2 * LANES
MODE_PLAIN, MODE_QUERY, MODE_KEY = 0, 1, 2


def _qkv_kernel(x_ref, g_ref, w_ref, cos_ref, sin_ref, qg_ref, kg_ref, o_ref, h_scr, *stage,
                scale, dil, pair_modes):
    tm = x_ref.shape[0]
    per = tm // dil
    hn = _rms(x_ref[...], g_ref[...])
    if dil == 1:
        h_scr[...] = hn.astype(BF16)
    else:
        stage_scr, = stage
        for c in range(hn.shape[1] // LANES):
            sl = slice(c * LANES, (c + 1) * LANES)
            stage_scr[c] = hn[:, sl]
            for r in range(dil):
                h_scr[r * per:(r + 1) * per, sl] = stage_scr[c, pl.ds(r, per, stride=dil), :].astype(BF16)
    h = h_scr[...]

    row = lax.broadcasted_iota(jnp.int32, (PAIR, PAIR), 0)
    col = lax.broadcasted_iota(jnp.int32, (PAIR, PAIR), 1)
    head_ones = (row // LANES == col // LANES).astype(BF16)
    cos = cos_ref[...]
    sin = sin_ref[...]
    tables = {}
    for mode, gain, sc in ((MODE_QUERY, qg_ref[...], scale), (MODE_KEY, kg_ref[...], 1.0)):
        if mode in pair_modes:
            tables[mode] = (cos * (gain * sc), sin * (pltpu.roll(gain, ROPE_SHIFT, 1) * sc))

    accs = [jnp.dot(h, w_ref[:, i * PAIR:(i + 1) * PAIR], preferred_element_type=F32)
            for i in range(len(pair_modes))]
    sumsqs = [None if mode == MODE_PLAIN else
              jnp.dot((a * a).astype(BF16), head_ones, preferred_element_type=F32)
              for a, mode in zip(accs, pair_modes)]
    for i, (a, sumsq, mode) in enumerate(zip(accs, sumsqs, pair_modes)):
        for c in range(PAIR // LANES):
            sl = slice(c * LANES, (c + 1) * LANES)
            head = a[:, sl]
            if mode != MODE_PLAIN:
                c1, s1 = tables[mode]
                inv = lax.rsqrt(sumsq[:, sl] * (1.0 / HEAD_DIM) + EPS)
                head = (head * c1 + pltpu.roll(head, ROPE_SHIFT, 1) * s1) * inv
            out_sl = slice(i * PAIR + c * LANES, i * PAIR + (c + 1) * LANES)
            o_ref[:, :, out_sl] = head.reshape(dil, per, LANES).astype(o_ref.dtype)


def _tile_residue_major(table, dil, tm):
    s, w = table.shape
    return table.reshape(s // tm, tm // dil, dil, w).transpose(0, 2, 1, 3).reshape(s, w)


def qkv_project(x2, gain, w, pair_modes, cos, sin, q_gain, k_gain, batch, seq, *, tm, scale, dil):
    t, d = x2.shape
    n = w.shape[1]
    assert n == len(pair_modes) * PAIR
    ns = seq // tm
    per = tm // dil
    cos_r = _tile_residue_major(cos, dil, tm)
    sin_r = _tile_residue_major(sin, dil, tm)
    table_spec = pl.BlockSpec((tm, LANES), lambda i: (i % ns, 0))
    vec_spec = pl.BlockSpec((1, LANES), lambda i: (0, 0))
    out = pl.pallas_call(
        functools.partial(_qkv_kernel, scale=scale, dil=dil, pair_modes=tuple(pair_modes)),
        grid=(t // tm,),
        in_specs=[
            pl.BlockSpec((tm, d), lambda i: (i, 0)),
            pl.BlockSpec((1, d), lambda i: (0, 0)),
            pl.BlockSpec((d, n), lambda i: (0, 0)),
            table_spec, table_spec, vec_spec, vec_spec,
        ],
        out_specs=pl.BlockSpec((None, dil, None, per, n), lambda i: (i // ns, 0, i % ns, 0, 0)),
        out_shape=jax.ShapeDtypeStruct((batch, dil, ns, per, n), BF16),
        scratch_shapes=[pltpu.VMEM((tm, d), BF16)]
        + ([pltpu.VMEM((d // LANES, tm, LANES), F32)] if dil > 1 else []),
        compiler_params=_params(("parallel",)),
        name=f"qkv_project_d{dil}",
    )(x2, gain, w, cos_r, sin_r, q_gain, k_gain)
    return out.reshape(batch, dil, seq // dil, n)


def _matmul_residual_kernel(a_ref, w_ref, x_ref, o_ref):
    o_ref[...] = x_ref[...] + jnp.dot(a_ref[...], w_ref[...], preferred_element_type=F32)


def matmul_residual(a, w, x2, *, tm):
    t, k = a.shape
    n = w.shape[1]
    return pl.pallas_call(
        _matmul_residual_kernel,
        grid=(t // tm,),
        in_specs=[
            pl.BlockSpec((tm, k), lambda i: (i, 0)),
            pl.BlockSpec((k, n), lambda i: (0, 0)),
            pl.BlockSpec((tm, n), lambda i: (i, 0)),
        ],
        out_specs=pl.BlockSpec((tm, n), lambda i: (i, 0)),
        out_shape=jax.ShapeDtypeStruct((t, n), F32),
        compiler_params=_params(("parallel",)),
        name="matmul_residual",
    )(a, w, x2)


def _gqa_attn_kernel(q_ref, k_ref, v_ref, o_ref, *, tk, streams):
    tq = q_ref.shape[1]
    seq = k_ref.shape[1]
    per = GQA_GROUP // streams
    qs = [jnp.concatenate([q_ref[0, :, g * LANES:(g + 1) * LANES] for g in range(st * per, (st + 1) * per)],
                          axis=0) for st in range(streams)]
    rows = per * tq
    ones = jnp.ones((tk, LANES), BF16)

    def body(c, carry):
        start = pl.multiple_of(c * tk, tk)
        k = k_ref[0, pl.ds(start, tk), :]
        v1 = jnp.concatenate([v_ref[0, pl.ds(start, tk), :], ones], axis=1)
        new = []
        for st in range(streams):
            m_prev, acc = carry[st]
            s = lax.dot_general(qs[st], k, NT_DIMS, preferred_element_type=F32)
            m_new = jnp.maximum(m_prev, jnp.max(s, axis=-1, keepdims=True))
            alpha = jnp.exp2(m_prev - m_new)
            p = jnp.exp2(s - m_new).astype(BF16)
            acc = alpha * acc + jnp.dot(p, v1, preferred_element_type=F32)
            new.append((m_new, acc))
        return tuple(new)

    init = tuple((jnp.full((rows, 1), NEG_INF, F32), jnp.zeros((rows, 2 * LANES), F32)) for _ in range(streams))
    final = lax.fori_loop(0, seq // tk, body, init, unroll=16)
    for st in range(streams):
        acc = final[st][1]
        o = acc[:, :LANES] / acc[:, LANES:]
        for gi in range(per):
            g = st * per + gi
            o_ref[0, :, g * LANES:(g + 1) * LANES] = o[gi * tq:(gi + 1) * tq].astype(o_ref.dtype)


def gqa_attention(qkv, *, tq, tk, streams):
    b, s, _ = qkv.shape
    gw = GQA_GROUP * HEAD_DIM
    k_off = GQA_HEADS
    v_off = GQA_HEADS + GQA_KV_HEADS
    return pl.pallas_call(
        functools.partial(_gqa_attn_kernel, tk=tk, streams=streams),
        grid=(b, GQA_KV_HEADS, s // tq),
        in_specs=[
            pl.BlockSpec((1, tq, gw), lambda bb, n, i: (bb, i, n)),
            pl.BlockSpec((1, s, HEAD_DIM), lambda bb, n, i: (bb, 0, k_off + n)),
            pl.BlockSpec((1, s, HEAD_DIM), lambda bb, n, i: (bb, 0, v_off + n)),
        ],
        out_specs=pl.BlockSpec((1, tq, gw), lambda bb, n, i: (bb, i, n)),
        out_shape=jax.ShapeDtypeStruct((b, s, GQA_HEADS * HEAD_DIM), BF16),
        compiler_params=_params(("parallel", "parallel", "arbitrary")),
        name="gqa_attention",
    )(qkv, qkv, qkv)


def _dil_attn_kernel(q_ref, kp_ref, kc_ref, kn_ref, vp_ref, vc_ref, vn_ref, o_ref, lse_ref,
                     *, span, m_total):
    i = pl.program_id(2)
    tq = q_ref.shape[0]
    qb = 2 * span
    nk = qb + 2 * span
    row = lax.broadcasted_iota(jnp.int32, (qb, nk), 0)
    col = lax.broadcasted_iota(jnp.int32, (qb, nk), 1)
    in_band = jnp.abs(col - span - row) <= span
    lane = lax.broadcasted_iota(jnp.int32, (qb, LANES), 1)
    ones = jnp.ones((nk, LANES), BF16)
    for sub in range(tq // qb):
        key_idx = i * tq + sub * qb - span + col
        ok = in_band & (key_idx >= 0) & (key_idx < m_total)
        lse_tile = jnp.zeros((qb, LANES), F32)
        rows = slice(sub * qb, (sub + 1) * qb)
        win = slice(sub * qb, sub * qb + nk)
        heads = [slice(h * LANES, (h + 1) * LANES) for h in range(DIL_HEADS)]
        scores = [lax.dot_general(q_ref[rows, sl],
                                  jnp.concatenate([kp_ref[:, sl], kc_ref[:, sl], kn_ref[:, sl]], axis=0)[win],
                                  NT_DIMS, preferred_element_type=F32) for sl in heads]
        maxes, probs = [], []
        for s in scores:
            s = jnp.where(ok, s, NEG_INF)
            m = jnp.max(s, axis=-1, keepdims=True)
            maxes.append(m)
            probs.append(jnp.exp2(s - m).astype(BF16))
        accs = [jnp.dot(p, jnp.concatenate(
                    [jnp.concatenate([vp_ref[:, sl], vc_ref[:, sl], vn_ref[:, sl]], axis=0)[win], ones], axis=1),
                    preferred_element_type=F32) for p, sl in zip(probs, heads)]
        for h, (acc, m, sl) in enumerate(zip(accs, maxes, heads)):
            o_ref[rows, sl] = (acc[:, :LANES] / acc[:, LANES:]).astype(o_ref.dtype)
            lse_tile = jnp.where(lane == h, m + jnp.log2(acc[:, LANES:LANES + 1]), lse_tile)
        lse_ref[rows, :] = lse_tile


def dilated_attention(qkv, span, *, tq):
    b, dil, m_total, n = qkv.shape
    hw = DIL_HEADS * HEAD_DIM
    nb = m_total // tq
    halo_per_tile = tq // span
    n_halo = m_total // span

    def tile(col):
        return pl.BlockSpec((None, None, tq, hw), lambda bb, r, i: (bb, r, i, col))

    def before(col):
        return pl.BlockSpec((None, None, span, hw),
                            lambda bb, r, i: (bb, r, jnp.maximum(i * halo_per_tile - 1, 0), col))

    def after(col):
        return pl.BlockSpec((None, None, span, hw),
                            lambda bb, r, i: (bb, r, jnp.minimum((i + 1) * halo_per_tile, n_halo - 1), col))

    return pl.pallas_call(
        functools.partial(_dil_attn_kernel, span=span, m_total=m_total),
        grid=(b, dil, nb),
        in_specs=[tile(0), before(1), tile(1), after(1), before(2), tile(2), after(2)],
        out_specs=[
            pl.BlockSpec((None, None, tq, hw), lambda bb, r, i: (bb, r, i, 0)),
            pl.BlockSpec((None, None, tq, LANES), lambda bb, r, i: (bb, r, i, 0)),
        ],
        out_shape=[
            jax.ShapeDtypeStruct((b, dil, m_total, hw), BF16),
            jax.ShapeDtypeStruct((b, dil, m_total, LANES), F32),
        ],
        compiler_params=_params(("parallel", "parallel", "arbitrary")),
        name=f"dilated_attention_d{dil}",
    )(qkv, qkv, qkv, qkv, qkv, qkv, qkv)


def _dil_merge_kernel(o0_ref, o1_ref, o2_ref, l0_ref, l1_ref, l2_ref, w_ref, x_ref, out_ref, o_scr, l_scr):
    def to_tokens(ref, scr, g):
        dil, per, width = ref.shape
        for c in range(width // LANES):
            sl = slice(c * LANES, (c + 1) * LANES)
            if dil == 1:
                scr[g, c] = ref[0, :, sl].astype(F32)
            else:
                for r in range(dil):
                    scr[g, c, pl.ds(r, per, stride=dil), :] = ref[r, :, sl].astype(F32)

    for g, (o_ref, l_ref) in enumerate(((o0_ref, l0_ref), (o1_ref, l1_ref), (o2_ref, l2_ref))):
        to_tokens(o_ref, o_scr, g)
        to_tokens(l_ref, l_scr, g)
    lses = [l_scr[g, 0] for g in range(3)]
    mx = jnp.maximum(jnp.maximum(lses[0], lses[1]), lses[2])
    es = [jnp.exp2(l - mx) for l in lses]
    den = es[0] + es[1] + es[2]
    ws = [e / den for e in es]
    parts = []
    for h in range(DIL_HEADS):
        acc = None
        for g in range(3):
            term = o_scr[g, h] * ws[g][:, h:h + 1]
            acc = term if acc is None else acc + term
        parts.append(acc.astype(BF16))
    merged = jnp.concatenate(parts, axis=1)
    out_ref[...] = x_ref[...] + jnp.dot(merged, w_ref[...], preferred_element_type=F32)


def dilated_merge_project(outs, lses, w_out, x2, batch, seq, *, tm):
    t, d = x2.shape
    hw = w_out.shape[0]
    ns = seq // tm

    def grouped(arr):
        _, dil, m_total, w = arr.shape
        per = tm // dil
        view = arr.reshape(batch, dil, ns, per, w)
        return view, pl.BlockSpec((None, dil, None, per, w), lambda i: (i // ns, 0, i % ns, 0, 0))

    views, specs = zip(*[grouped(a) for a in (*outs, *lses)])
    row = pl.BlockSpec((tm, d), lambda i: (i, 0))
    return pl.pallas_call(
        _dil_merge_kernel,
        grid=(t // tm,),
        in_specs=list(specs) + [pl.BlockSpec((hw, d), lambda i: (0, 0)), row],
        out_specs=row,
        out_shape=jax.ShapeDtypeStruct((t, d), F32),
        scratch_shapes=[pltpu.VMEM((3, hw // LANES, tm, LANES), F32), pltpu.VMEM((3, 1, tm, LANES), F32)],
        compiler_params=_params(("parallel",)),
        name="dilated_merge_project",
    )(*views, w_out, x2)


def _rglru_scan_kernel(xc_ref, xp_ref, xn_ref, cw_ref, cb_ref, wa_ref, ba_ref, wx_ref, bx_ref, lam_ref,
                       o_ref, a_scr, u_scr, h_scr, *, reverse, n_steps):
    s = pl.program_id(1)
    ts, nb, tc = xc_ref.shape
    chunk = (n_steps - 1 - s) if reverse else s

    @pl.when(s == 0)
    def _():
        h_scr[...] = jnp.zeros(h_scr.shape, F32)

    prev = jnp.where(chunk == 0, 0.0, xp_ref[...])
    nxt = jnp.where(chunk == n_steps - 1, 0.0, xn_ref[...])
    xe = jnp.concatenate([prev, xc_ref[...], nxt], axis=0)
    cw = cw_ref[...]
    xc = cb_ref[...].reshape(1, 1, tc) + xe[0:ts] * cw[0:1].reshape(1, 1, tc)
    for j in range(1, CONV_W):
        xc = xc + xe[j:j + ts] * cw[j:j + 1].reshape(1, 1, tc)
    xc2 = xc.reshape(ts * nb, tc)
    lam = lam_ref[...]
    softplus_neg_lam = jnp.maximum(-lam, 0.0) + jnp.log1p(jnp.exp(-jnp.abs(lam)))
    decay_log2 = (-RG_C * LOG2_E) * softplus_neg_lam

    def sigmoid(x16, w_ref, bias):
        half_z = jnp.dot(x16, w_ref[kb] * 0.5, preferred_element_type=F32) + 0.5 * bias
        return 0.5 * jnp.tanh(half_z) + 0.5

    for kb in range(tc // RG_BLOCK_W):
        sl = slice(kb * RG_BLOCK_W, (kb + 1) * RG_BLOCK_W)
        xb = xc2[:, sl]
        xb16 = xb.astype(BF16)
        r = sigmoid(xb16, wa_ref, ba_ref[:, sl])
        gate = sigmoid(xb16, wx_ref, bx_ref[:, sl])
        a = jnp.exp2(r * decay_log2[:, sl])
        z = 1.0 - a * a
        u = (z * lax.rsqrt(jnp.maximum(z, 1e-30))) * (gate * xb)
        a_scr[:, :, sl] = a.reshape(ts, nb, RG_BLOCK_W)
        u_scr[:, :, sl] = u.reshape(ts, nb, RG_BLOCK_W)

    def step(t, h):
        tt = (ts - 1 - t) if reverse else t
        h = a_scr[tt] * h + u_scr[tt]
        o_ref[tt] = h
        return h

    h_scr[...] = lax.fori_loop(0, ts, step, h_scr[...], unroll=8)


def rglru_scan(yx3, conv_w, conv_b, w_a, b_a, w_x, b_x, lam, *, reverse, ts, tc):
    seq, nb, two_c = yx3.shape
    c = two_c // 2
    nc, n_steps = c // tc, seq // ts
    kb = tc // RG_BLOCK_W

    def chunk_of(s):
        return (n_steps - 1 - s) if reverse else s

    vec = lambda rows: pl.BlockSpec((rows, tc), lambda ci, s: (0, ci))
    mat = pl.BlockSpec((kb, RG_BLOCK_W, RG_BLOCK_W), lambda ci, s: (ci, 0, 0))
    return pl.pallas_call(
        functools.partial(_rglru_scan_kernel, reverse=reverse, n_steps=n_steps),
        grid=(nc, n_steps),
        in_specs=[
            pl.BlockSpec((ts, nb, tc), lambda ci, s: (chunk_of(s), 0, nc + ci)),
            pl.BlockSpec((CONV_LEFT, nb, tc),
                         lambda ci, s: (jnp.maximum(chunk_of(s) * (ts // CONV_LEFT) - 1, 0), 0, nc + ci)),
            pl.BlockSpec((1, nb, tc),
                         lambda ci, s: (jnp.minimum((chunk_of(s) + 1) * ts, seq - 1), 0, nc + ci)),
            vec(CONV_W), vec(1), mat, vec(1), mat, vec(1), vec(1),
        ],
        out_specs=pl.BlockSpec((ts, nb, tc), lambda ci, s: (chunk_of(s), 0, ci)),
        out_shape=jax.ShapeDtypeStruct((seq, nb, c), F32),
        scratch_shapes=[
            pltpu.VMEM((ts, nb, tc), F32),
            pltpu.VMEM((ts, nb, tc), F32),
            pltpu.VMEM((nb, tc), F32),
        ],
        compiler_params=_params(("parallel", "arbitrary")),
        name="rglru_scan_bwd" if reverse else "rglru_scan_fwd",
    )(yx3, yx3, yx3, conv_w, conv_b, w_a, b_a, w_x, b_x, lam)


def _gelu_tanh(y):
    return y * (0.5 * (1.0 + jnp.tanh(0.7978845608028654 * (y + 0.044715 * (y * y * y)))))


def _rglru_out_kernel(y_ref, hf_ref, hb_ref, w_ref, x_ref, o_ref):
    gated = _gelu_tanh(y_ref[...]) * (hf_ref[...] + hb_ref[...])
    o_ref[...] = x_ref[...] + jnp.dot(gated.astype(BF16), w_ref[...], preferred_element_type=F32)


def rglru_out_project(yx, hf, hb, w_out, x2, batch, seq, *, tm):
    c, d = w_out.shape
    ns = seq // tm
    return pl.pallas_call(
        _rglru_out_kernel,
        grid=(batch, ns),
        in_specs=[
            pl.BlockSpec((tm, c), lambda b, s: (s, 2 * b)),
            pl.BlockSpec((tm, c), lambda b, s: (s, b)),
            pl.BlockSpec((tm, c), lambda b, s: (s, b)),
            pl.BlockSpec((c, d), lambda b, s: (0, 0)),
            pl.BlockSpec((tm, d), lambda b, s: (b * ns + s, 0)),
        ],
        out_specs=pl.BlockSpec((tm, d), lambda b, s: (b * ns + s, 0)),
        out_shape=jax.ShapeDtypeStruct(x2.shape, F32),
        compiler_params=_params(("parallel", "parallel")),
        name="rglru_out_project",
    )(yx, hf, hb, w_out, x2)


def _swiglu_step(h, wg_ref, wu_ref, wd_ref, acc_scr):
    gate = jnp.dot(h, wg_ref[...].astype(BF16), preferred_element_type=F32)
    up = jnp.dot(h, wu_ref[...].astype(BF16), preferred_element_type=F32)
    act = gate * jax.nn.sigmoid(gate) * up
    acc_scr[...] += jnp.dot(act.astype(BF16), wd_ref[...].astype(BF16), preferred_element_type=F32)


def _ffn_kernel(x_ref, g_ref, wg_ref, wu_ref, wd_ref, o_ref, h_scr, acc_scr):
    f = pl.program_id(1)

    @pl.when(f == 0)
    def _():
        h_scr[...] = _rms(x_ref[...], g_ref[...]).astype(BF16)
        acc_scr[...] = jnp.zeros(acc_scr.shape, F32)

    _swiglu_step(h_scr[...], wg_ref, wu_ref, wd_ref, acc_scr)

    @pl.when(f == pl.num_programs(1) - 1)
    def _():
        o_ref[...] = x_ref[...] + acc_scr[...]


def swiglu_ffn(x2, gain, w_gate_up, w_down, *, tm, tf):
    t, d = x2.shape
    ff = w_down.shape[0]
    nf = ff // tf
    return pl.pallas_call(
        _ffn_kernel,
        grid=(t // tm, nf),
        in_specs=[
            pl.BlockSpec((tm, d), lambda i, f: (i, 0)),
            pl.BlockSpec((1, d), lambda i, f: (0, 0)),
            pl.BlockSpec((d, tf), lambda i, f: (0, f)),
            pl.BlockSpec((d, tf), lambda i, f: (0, nf + f)),
            pl.BlockSpec((tf, d), lambda i, f: (f, 0)),
        ],
        out_specs=pl.BlockSpec((tm, d), lambda i, f: (i, 0)),
        out_shape=jax.ShapeDtypeStruct((t, d), F32),
        scratch_shapes=[pltpu.VMEM((tm, d), BF16), pltpu.VMEM((tm, d), F32)],
        compiler_params=_params(("parallel", "arbitrary")),
        name="dense_ffn",
    )(x2, gain, w_gate_up, w_gate_up, w_down)


SUBLANES = 8


def _copy_token(src_hbm, dst, src_tok, dst_tok, sem):
    return pltpu.make_async_copy(src_hbm.at[pl.ds(src_tok * SUBLANES, SUBLANES)],
                                 dst.at[pl.ds(dst_tok * SUBLANES, SUBLANES)], sem)


def _moe_scatter_kernel(zero_ref, pos_ref, x_ref, o_hbm, stage, zeros_scr, sem, *, tm):
    p = pl.program_id(0)
    n = pl.num_programs(0)
    slot = p % 2
    chunks = x_ref.shape[1] // LANES
    block_rows = tm * SUBLANES

    def zero_copy(j):
        dst = o_hbm.at[pl.ds(zero_ref[j] * SUBLANES, zeros_scr.shape[0])]
        return pltpu.make_async_copy(zeros_scr, dst, sem.at[2])

    @pl.when(p == 0)
    def _():
        zeros_scr[...] = jnp.zeros(zeros_scr.shape, F32)
        for j in range(2 * N_EXPERTS):
            zero_copy(j).start()
            zero_copy(j).wait()

    def slot_wait(s):
        rows = pl.ds(0, 2 * block_rows)
        pltpu.make_async_copy(o_hbm.at[rows], o_hbm.at[rows], sem.at[s]).wait()

    @pl.when(p >= 2)
    def _():
        slot_wait(slot)

    for c in range(chunks):
        stage[slot, pl.ds(c, tm, stride=SUBLANES), :] = x_ref[:, c * LANES:(c + 1) * LANES]

    def issue(r, carry):
        src = stage.at[slot, pl.ds(r * SUBLANES, SUBLANES)]
        for k in range(TOP_K):
            dst = o_hbm.at[pl.ds(pos_ref[0, k * tm + r] * SUBLANES, SUBLANES)]
            pltpu.make_async_copy(src, dst, sem.at[slot]).start()
        return carry
    lax.fori_loop(0, tm, issue, 0, unroll=8)

    @pl.when(p == n - 1)
    def _():
        slot_wait(slot)

        @pl.when(n >= 2)
        def _():
            slot_wait(1 - slot)


def moe_scatter(x2, pos, zero_starts, n_tiles, *, tm, te):
    t, d = x2.shape
    grid_spec = pltpu.PrefetchScalarGridSpec(
        num_scalar_prefetch=1,
        grid=(t // tm,),
        in_specs=[
            pl.BlockSpec((None, 1, TOP_K * tm), lambda p, z: (p, 0, 0), memory_space=pltpu.SMEM),
            pl.BlockSpec((tm, d), lambda p, z: (p, 0)),
        ],
        out_specs=pl.BlockSpec(memory_space=pl.ANY),
        scratch_shapes=[pltpu.VMEM((2, tm * SUBLANES, LANES), F32), pltpu.VMEM((te * SUBLANES, LANES), F32),
                        pltpu.SemaphoreType.DMA((3,))],
    )
    return pl.pallas_call(
        functools.partial(_moe_scatter_kernel, tm=tm),
        grid_spec=grid_spec,
        out_shape=jax.ShapeDtypeStruct(((n_tiles + 1) * te * SUBLANES, LANES), F32),
        compiler_params=_params(("arbitrary",)),
        name="moe_scatter",
    )(zero_starts, pos, x2)


def _moe_expert_kernel(te_ref, meta_ref, xs_ref, g_ref, wg_ref, wu_ref, wd_ref, y_ref, h_scr, acc_scr):
    p = pl.program_id(0)
    f = pl.program_id(1)
    tm = acc_scr.shape[0]
    chunks = acc_scr.shape[1] // LANES
    used = p < meta_ref[0]

    @pl.when(used & (f == 0))
    def _():
        x = jnp.concatenate([xs_ref[pl.ds(c, tm, stride=SUBLANES), :] for c in range(chunks)], axis=1)
        h_scr[...] = _rms(x, g_ref[...]).astype(BF16)
        acc_scr[...] = jnp.zeros(acc_scr.shape, F32)

    @pl.when(used)
    def _():
        _swiglu_step(h_scr[...], wg_ref, wu_ref, wd_ref, acc_scr)

    @pl.when(f == pl.num_programs(1) - 1)
    def _():
        y = jnp.where(used, acc_scr[...], 0.0)
        for c in range(chunks):
            y_ref[pl.ds(c, tm, stride=SUBLANES), :] = y[:, c * LANES:(c + 1) * LANES]


def moe_experts(xs, gain, w_gate_up, w_down, tile_expert, meta, n_tiles, *, tm, tf):
    n_e, ff, d = w_down.shape
    nf = ff // tf

    def f_eff(p, f, meta_ref):
        return jnp.where(p < meta_ref[0], f, nf - 1)

    grid_spec = pltpu.PrefetchScalarGridSpec(
        num_scalar_prefetch=2,
        grid=(n_tiles, nf),
        in_specs=[
            pl.BlockSpec((tm * SUBLANES, LANES), lambda p, f, te, m: (p, 0)),
            pl.BlockSpec((1, d), lambda p, f, te, m: (0, 0)),
            pl.BlockSpec((None, d, tf), lambda p, f, te, m: (te[p], 0, f_eff(p, f, m))),
            pl.BlockSpec((None, d, tf), lambda p, f, te, m: (te[p], 0, nf + f_eff(p, f, m))),
            pl.BlockSpec((None, tf, d), lambda p, f, te, m: (te[p], f_eff(p, f, m), 0)),
        ],
        out_specs=pl.BlockSpec((tm * SUBLANES, LANES), lambda p, f, te, m: (p, 0)),
        scratch_shapes=[pltpu.VMEM((tm, d), BF16), pltpu.VMEM((tm, d), F32)],
    )
    return pl.pallas_call(
        _moe_expert_kernel,
        grid_spec=grid_spec,
        out_shape=jax.ShapeDtypeStruct((n_tiles * tm * SUBLANES, LANES), F32),
        compiler_params=_params(("parallel", "arbitrary")),
        name="moe_experts",
    )(tile_expert, meta, xs, gain, w_gate_up, w_gate_up, w_down)


def _moe_combine_kernel(cur_ref, nxt_ref, x_ref, route_ref, y_hbm, o_ref, buf, sem, *, tm):
    p = pl.program_id(0)
    n = pl.num_programs(0)
    chunks = x_ref.shape[1] // LANES

    def issue(idx_ref, slot):
        def body(r, carry):
            _copy_token(y_hbm, buf.at[slot], idx_ref[0, r], r, sem.at[slot]).start()
            return carry
        lax.fori_loop(0, 2 * tm, body, 0, unroll=8)

    @pl.when(p == 0)
    def _():
        issue(cur_ref, 0)

    @pl.when(p + 1 < n)
    def _():
        issue(nxt_ref, (p + 1) % 2)

    slot = p % 2
    pltpu.make_async_copy(y_hbm.at[pl.ds(0, 2 * tm * SUBLANES)], buf.at[slot], sem.at[slot]).wait()
    route = route_ref[...]
    g1 = route[:, 2:3]
    g2 = route[:, 3:4]
    for c in range(chunks):
        first = buf[slot, pl.ds(c, tm, stride=SUBLANES), :]
        second = buf[slot, pl.ds(tm * SUBLANES + c, tm, stride=SUBLANES), :]
        sl = slice(c * LANES, (c + 1) * LANES)
        o_ref[:, sl] = x_ref[:, sl] + (g1 * first + g2 * second)


def moe_combine(x2, y_tiles, pos, route, *, tm):
    t, d = x2.shape
    n = t // tm
    idx_spec = lambda shift: pl.BlockSpec((None, 1, 2 * tm), lambda p: (jnp.minimum(p + shift, n - 1), 0, 0),
                                          memory_space=pltpu.SMEM)
    return pl.pallas_call(
        functools.partial(_moe_combine_kernel, tm=tm),
        grid=(n,),
        in_specs=[idx_spec(0), idx_spec(1), pl.BlockSpec((tm, d), lambda p: (p, 0)),
                  pl.BlockSpec((tm, LANES), lambda p: (p, 0)), pl.BlockSpec(memory_space=pl.ANY)],
        out_specs=pl.BlockSpec((tm, d), lambda p: (p, 0)),
        out_shape=jax.ShapeDtypeStruct((t, d), F32),
        scratch_shapes=[pltpu.VMEM((2, 2 * tm * SUBLANES, LANES), F32), pltpu.SemaphoreType.DMA((2,))],
        compiler_params=_params(("arbitrary",)),
        name="moe_combine",
    )(pos, pos, x2, route, y_tiles)


def _moe_plan(route, te, tm):
    t = route.shape[0]
    experts = route[:, :TOP_K].astype(jnp.int32)
    onehot = (experts[:, :, None] == jnp.arange(N_EXPERTS)[None, None, :]).astype(jnp.int32).sum(axis=1)
    incl = jnp.cumsum(onehot, axis=0)
    rank = incl - onehot
    counts = incl[-1]
    tiles_per = (counts + te - 1) // te
    tile_end = jnp.cumsum(tiles_per)
    row_start = (tile_end - tiles_per) * te
    dest = row_start[experts] + jnp.take_along_axis(rank, experts, axis=1)
    n_tiles = -(-(TOP_K * t) // te) + N_EXPERTS
    tile_expert = jnp.minimum(jnp.searchsorted(tile_end, jnp.arange(n_tiles), side="right"),
                              N_EXPERTS - 1).astype(jnp.int32)
    n_used = tile_end[-1]
    tail = jnp.minimum((n_used + jnp.arange(N_EXPERTS)) * te, n_tiles * te)
    zero_starts = jnp.concatenate([row_start + counts, tail]).astype(jnp.int32)
    pos = jnp.concatenate([dest[:, k].reshape(t // tm, tm) for k in range(TOP_K)], axis=1)
    return (pos.reshape(t // tm, 1, TOP_K * tm).astype(jnp.int32), zero_starts, tile_expert,
            n_used.reshape(1).astype(jnp.int32), n_tiles)


def _router_kernel(x_ref, g_ref, r_ref, b_ref, c_ref):
    h = _rms(x_ref[...], g_ref[...])
    r = r_ref[...]
    h_hi = h.astype(BF16)
    h_lo = (h - h_hi.astype(F32)).astype(BF16)
    r_hi = r.astype(BF16)
    r_lo = (r - r_hi.astype(F32)).astype(BF16)
    both = jnp.dot(h_hi, jnp.concatenate([r_hi, r_lo], axis=1), preferred_element_type=F32)
    logits = both[:, :LANES] + both[:, LANES:] + jnp.dot(h_lo, r_hi, preferred_element_type=F32)
    logits = logits + b_ref[...]
    lane = lax.broadcasted_iota(jnp.int32, logits.shape, 1)
    logits = jnp.where(lane < N_EXPERTS, logits, NEG_INF)
    v1 = jnp.max(logits, axis=-1, keepdims=True)
    i1 = jnp.min(jnp.where(logits == v1, lane, LANES), axis=-1, keepdims=True)
    rest = jnp.where(lane == i1, NEG_INF, logits)
    v2 = jnp.max(rest, axis=-1, keepdims=True)
    i2 = jnp.min(jnp.where(rest == v2, lane, LANES), axis=-1, keepdims=True)
    e2 = jnp.exp(v2 - v1)
    g1 = 1.0 / (1.0 + e2)
    g2 = e2 / (1.0 + e2)
    c_ref[...] = jnp.where(lane == 0, i1.astype(F32), jnp.where(lane == 1, i2.astype(F32),
                           jnp.where(lane == 2, g1, jnp.where(lane == 3, g2, 0.0))))


def moe_router(x2, gain, router, router_b, *, tm):
    t, d = x2.shape
    r_pad = jnp.zeros((d, LANES), F32).at[:, :N_EXPERTS].set(router)
    b_pad = jnp.zeros((1, LANES), F32).at[0, :N_EXPERTS].set(router_b)
    return pl.pallas_call(
        _router_kernel,
        grid=(t // tm,),
        in_specs=[
            pl.BlockSpec((tm, d), lambda i: (i, 0)),
            pl.BlockSpec((1, d), lambda i: (0, 0)),
            pl.BlockSpec((d, LANES), lambda i: (0, 0)),
            pl.BlockSpec((1, LANES), lambda i: (0, 0)),
        ],
        out_specs=pl.BlockSpec((tm, LANES), lambda i: (i, 0)),
        out_shape=jax.ShapeDtypeStruct((t, LANES), F32),
        compiler_params=_params(("parallel",)),
        name="moe_router",
    )(x2, gain, r_pad, b_pad)


def _inv_freq(dim, theta):
    return theta ** (-jnp.arange(0, dim, 2, dtype=F32) / dim)


_Q = HEAD_DIM // 4
AXIAL_PERM = np.concatenate([np.arange(0, _Q), np.arange(2 * _Q, 3 * _Q), np.arange(_Q, 2 * _Q),
                             np.arange(3 * _Q, 4 * _Q)])
_R = ROT_DIMS // 2
_REST = (HEAD_DIM - ROT_DIMS) // 2
PARTIAL_PERM = np.concatenate([np.arange(0, _R), np.arange(ROT_DIMS, ROT_DIMS + _REST), np.arange(_R, ROT_DIMS),
                               np.arange(ROT_DIMS + _REST, HEAD_DIM)])


def _axial_tables(seq):
    t = jnp.arange(seq)
    rows = (t // GRID_W).astype(F32)
    cols = (t % GRID_W).astype(F32)
    freq = _inv_freq(HEAD_DIM // 2, AXIAL_THETA)
    ang_r = rows[:, None] * freq[None, :]
    ang_c = cols[:, None] * freq[None, :]
    ang = jnp.concatenate([ang_r, ang_c, ang_r, ang_c], axis=-1)
    sign = jnp.concatenate([-jnp.ones(ROPE_SHIFT, F32), jnp.ones(ROPE_SHIFT, F32)])
    return jnp.cos(ang), jnp.sin(ang) * sign[None, :]


def _partial_tables(seq):
    pos = jnp.arange(seq, dtype=F32)
    freq = _inv_freq(ROT_DIMS, ROPE_THETA)
    ang = pos[:, None] * freq[None, :]
    ones = jnp.ones((seq, _REST), F32)
    zeros = jnp.zeros((seq, _REST), F32)
    cos = jnp.concatenate([jnp.cos(ang), ones, jnp.cos(ang), ones], axis=-1)
    sin = jnp.concatenate([-jnp.sin(ang), zeros, jnp.sin(ang), zeros], axis=-1)
    return cos, sin


def _permute_qk_columns(w, head_is_qk, perm):
    idx = np.concatenate([h * HEAD_DIM + (perm if qk else np.arange(HEAD_DIM))
                          for h, qk in enumerate(head_is_qk)])
    return jnp.take(w, jnp.asarray(idx, jnp.int32), axis=1)


def _row(v):
    return v.reshape(1, -1).astype(F32)


def _rglru_layer(x2, batch, seq, norm, w_in, conv_w, conv_b, w_a, b_a, w_x, b_x, lam, w_out):
    c = w_out.shape[0]
    yx = norm_matmul_time_major(x2, _row(norm), w_in.astype(BF16), batch, seq,
                                tm=512, tn=2048, out_dtype=F32)
    yx3 = yx.reshape(seq, batch, 2 * c)
    hs = []
    for direction in range(2):
        hs.append(rglru_scan(
            yx3, conv_w, _row(conv_b), w_a[direction].astype(BF16), _row(b_a[direction]),
            w_x[direction].astype(BF16), _row(b_x[direction]), _row(lam[direction]),
            reverse=direction == 1, ts=256, tc=512).reshape(seq, batch * c))
    return rglru_out_project(yx, hs[0], hs[1], w_out.astype(BF16), x2, batch, seq, tm=512)


def _gqa_layer(x2, batch, seq, norm, w_qkv, q_gain, k_gain, w_out):
    nq = GQA_HEADS * HEAD_DIM
    nkv = GQA_KV_HEADS * HEAD_DIM
    modes = [MODE_QUERY] * (nq // PAIR) + [MODE_KEY] * (nkv // PAIR) + [MODE_PLAIN] * (nkv // PAIR)
    cos, sin = _axial_tables(seq)
    is_qk = [True] * (GQA_HEADS + GQA_KV_HEADS) + [False] * GQA_KV_HEADS
    w16 = _permute_qk_columns(w_qkv, is_qk, AXIAL_PERM).astype(BF16)
    qkv = qkv_project(x2, _row(norm), w16, modes, cos, sin, _row(q_gain[AXIAL_PERM]), _row(k_gain[AXIAL_PERM]),
                      batch, seq, tm=512, scale=HEAD_DIM ** -0.5 * LOG2_E, dil=1)
    o = gqa_attention(qkv.reshape(batch, seq, nq + 2 * nkv), tq=256, tk=512, streams=2)
    return matmul_residual(o.reshape(batch * seq, nq), w_out.astype(BF16), x2, tm=512)


def _dilated_layer(x2, batch, seq, norm, w_qkv, q_gain, k_gain, w_out):
    hw = DIL_HEADS * HEAD_DIM
    modes = [MODE_QUERY] * (hw // PAIR) + [MODE_KEY] * (hw // PAIR) + [MODE_PLAIN] * (hw // PAIR)
    cos, sin = _partial_tables(seq)
    is_qk = ([True] * (2 * DIL_HEADS) + [False] * DIL_HEADS) * len(DIL_PATTERNS)
    w16 = _permute_qk_columns(w_qkv, is_qk, PARTIAL_PERM).astype(BF16)
    outs, lses = [], []
    for g, (window, dil) in enumerate(DIL_PATTERNS):
        span = window // (2 * dil)
        assert seq % (dil * span) == 0
        qkv = qkv_project(x2, _row(norm), w16[:, g * 3 * hw:(g + 1) * 3 * hw], modes, cos, sin,
                          _row(q_gain[PARTIAL_PERM]), _row(k_gain[PARTIAL_PERM]), batch, seq,
                          tm=512, scale=HEAD_DIM ** -0.5 * LOG2_E, dil=dil)
        o, lse = dilated_attention(qkv, span, tq=min(4 * span, seq // dil))
        outs.append(o)
        lses.append(lse)
    return dilated_merge_project(outs, lses, w_out.astype(BF16), x2, batch, seq, tm=512)


def _dense_ffn_layer(x2, norm, w_gate_up, w_down):
    return swiglu_ffn(x2, _row(norm), w_gate_up.astype(BF16), w_down.astype(BF16), tm=1024, tf=512)


def _moe_layer(x2, norm, router, router_b, w_gate_up, w_down):
    tm, te = 512, 1024
    route = moe_router(x2, _row(norm), router, router_b, tm=tm)
    pos, zero_starts, tile_expert, meta, n_tiles = _moe_plan(route, te, tm)
    xs = moe_scatter(x2, pos, zero_starts, n_tiles, tm=tm, te=te)
    ys = moe_experts(xs, _row(norm), w_gate_up, w_down, tile_expert, meta, n_tiles,
                     tm=te, tf=512)
    return moe_combine(x2, ys, pos, route, tm=tm)


def kernel(x, l0_attn_norm, l0_rg_w_in, l0_rg_conv_w, l0_rg_conv_b, l0_rg_w_a, l0_rg_b_a, l0_rg_w_x, l0_rg_b_x, l0_rg_lambda, l0_rg_w_out, l0_ffn_norm, l0_ffn_w_gate_up, l0_ffn_w_down, l1_attn_norm, l1_gqa_w_qkv, l1_gqa_q_norm, l1_gqa_k_norm, l1_gqa_w_out, l1_ffn_norm, l1_moe_router, l1_moe_router_b, l1_moe_w_gate_up, l1_moe_w_down, l2_attn_norm, l2_dil_w_qkv, l2_dil_q_norm, l2_dil_k_norm, l2_dil_w_out, l2_ffn_norm, l2_ffn_w_gate_up, l2_ffn_w_down, l3_attn_norm, l3_rg_w_in, l3_rg_conv_w, l3_rg_conv_b, l3_rg_w_a, l3_rg_b_a, l3_rg_w_x, l3_rg_b_x, l3_rg_lambda, l3_rg_w_out, l3_ffn_norm, l3_moe_router, l3_moe_router_b, l3_moe_w_gate_up, l3_moe_w_down):
    batch, seq, d = x.shape
    x2 = x.reshape(batch * seq, d)
    x2 = _rglru_layer(x2, batch, seq, l0_attn_norm, l0_rg_w_in, l0_rg_conv_w, l0_rg_conv_b, l0_rg_w_a,
                      l0_rg_b_a, l0_rg_w_x, l0_rg_b_x, l0_rg_lambda, l0_rg_w_out)
    x2 = _dense_ffn_layer(x2, l0_ffn_norm, l0_ffn_w_gate_up, l0_ffn_w_down)
    x2 = _gqa_layer(x2, batch, seq, l1_attn_norm, l1_gqa_w_qkv, l1_gqa_q_norm, l1_gqa_k_norm, l1_gqa_w_out)
    x2 = _moe_layer(x2, l1_ffn_norm, l1_moe_router, l1_moe_router_b, l1_moe_w_gate_up, l1_moe_w_down)
    x2 = _dilated_layer(x2, batch, seq, l2_attn_norm, l2_dil_w_qkv, l2_dil_q_norm, l2_dil_k_norm, l2_dil_w_out)
    x2 = _dense_ffn_layer(x2, l2_ffn_norm, l2_ffn_w_gate_up, l2_ffn_w_down)
    x2 = _rglru_layer(x2, batch, seq, l3_attn_norm, l3_rg_w_in, l3_rg_conv_w, l3_rg_conv_b, l3_rg_w_a,
                      l3_rg_b_a, l3_rg_w_x, l3_rg_b_x, l3_rg_lambda, l3_rg_w_out)
    x2 = _moe_layer(x2, l3_ffn_norm, l3_moe_router, l3_moe_router_b, l3_moe_w_gate_up, l3_moe_w_down)
    return x2.reshape(batch, seq, d)
```

```python
import functools

import jax
import jax.numpy as jnp
import numpy as np
from jax import lax
from jax.experimental import pallas as pl
from jax.experimental.pallas import tpu as pltpu

F32 = jnp.float32
BF16 = jnp.bfloat16

EPS = 1e-6
NEG_INF = -1e30
GRID_W = 64
HEAD_DIM = 128
LANES = 128
RG_BLOCK_W = 128
CONV_W = 4
CONV_LEFT = 2
RG_C = 8.0
GQA_HEADS = 8
GQA_KV_HEADS = 2
GQA_GROUP = GQA_HEADS // GQA_KV_HEADS
AXIAL_THETA = 10000.0
DIL_PATTERNS = ((128, 1), (512, 4), (2048, 16))
DIL_HEADS = 8
ROPE_THETA = 500000.0
ROT_DIMS = HEAD_DIM // 4
N_EXPERTS = 8
TOP_K = 2
VMEM_LIMIT = 56 * 1024 * 1024

NT_DIMS = (((1,), (1,)), ((), ()))
LOG2_E = 1.4426950408889634


def _params(sem, vmem=VMEM_LIMIT):
    return pltpu.CompilerParams(dimension_semantics=sem, vmem_limit_bytes=vmem)


def _rms(x, g):
    return x * lax.rsqrt(jnp.mean(x * x, axis=-1, keepdims=True) + EPS) * g


def _norm_matmul_kernel(x_ref, g_ref, w_ref, o_ref, h_scr):
    @pl.when(pl.program_id(2) == 0)
    def _():
        h_scr[...] = _rms(x_ref[...], g_ref[...]).astype(BF16)

    o_ref[...] = jnp.dot(h_scr[...], w_ref[...], preferred_element_type=F32).astype(o_ref.dtype)


def norm_matmul_time_major(x2, gain, w, batch, seq, *, tm, tn, out_dtype):
    d = x2.shape[1]
    n = w.shape[1]
    ns, nj = seq // tm, n // tn
    return pl.pallas_call(
        _norm_matmul_kernel,
        grid=(batch, ns, nj),
        in_specs=[
            pl.BlockSpec((tm, d), lambda b, s, j: (b * ns + s, 0)),
            pl.BlockSpec((1, d), lambda b, s, j: (0, 0)),
            pl.BlockSpec((d, tn), lambda b, s, j: (0, j)),
        ],
        out_specs=pl.BlockSpec((tm, tn), lambda b, s, j: (s, b * nj + j)),
        out_shape=jax.ShapeDtypeStruct((seq, batch * n), out_dtype),
        scratch_shapes=[pltpu.VMEM((tm, d), BF16)],
        compiler_params=_params(("parallel", "parallel", "arbitrary")),
        name="norm_matmul_tm",
    )(x2, gain, w)


ROPE_SHIFT = LANES // 2


PAIR = 2 * LANES
MODE_PLAIN, MODE_QUERY, MODE_KEY = 0, 1, 2


def _qkv_kernel(x_ref, g_ref, w_ref, cos_ref, sin_ref, qg_ref, kg_ref, o_ref, h_scr, *stage,
                scale, dil, pair_modes):
    tm = x_ref.shape[0]
    per = tm // dil
    hn = _rms(x_ref[...], g_ref[...])
    if dil == 1:
        h_scr[...] = hn.astype(BF16)
    else:
        stage_scr, = stage
        for c in range(hn.shape[1] // LANES):
            sl = slice(c * LANES, (c + 1) * LANES)
            stage_scr[c] = hn[:, sl]
            for r in range(dil):
                h_scr[r * per:(r + 1) * per, sl] = stage_scr[c, pl.ds(r, per, stride=dil), :].astype(BF16)
    h = h_scr[...]

    row = lax.broadcasted_iota(jnp.int32, (PAIR, PAIR), 0)
    col = lax.broadcasted_iota(jnp.int32, (PAIR, PAIR), 1)
    head_ones = (row // LANES == col // LANES).astype(BF16)
    cos = cos_ref[...]
    sin = sin_ref[...]
    tables = {}
    for mode, gain, sc in ((MODE_QUERY, qg_ref[...], scale), (MODE_KEY, kg_ref[...], 1.0)):
        if mode in pair_modes:
            tables[mode] = (cos * (gain * sc), sin * (pltpu.roll(gain, ROPE_SHIFT, 1) * sc))

    accs = [jnp.dot(h, w_ref[:, i * PAIR:(i + 1) * PAIR], preferred_element_type=F32)
            for i in range(len(pair_modes))]
    sumsqs = [None if mode == MODE_PLAIN else
              jnp.dot((a * a).astype(BF16), head_ones, preferred_element_type=F32)
              for a, mode in zip(accs, pair_modes)]
    for i, (a, sumsq, mode) in enumerate(zip(accs, sumsqs, pair_modes)):
        for c in range(PAIR // LANES):
            sl = slice(c * LANES, (c + 1) * LANES)
            head = a[:, sl]
            if mode != MODE_PLAIN:
                c1, s1 = tables[mode]
                inv = lax.rsqrt(sumsq[:, sl] * (1.0 / HEAD_DIM) + EPS)
                head = (head * c1 + pltpu.roll(head, ROPE_SHIFT, 1) * s1) * inv
            out_sl = slice(i * PAIR + c * LANES, i * PAIR + (c + 1) * LANES)
            o_ref[:, :, out_sl] = head.reshape(dil, per, LANES).astype(o_ref.dtype)


def _tile_residue_major(table, dil, tm):
    s, w = table.shape
    return table.reshape(s // tm, tm // dil, dil, w).transpose(0, 2, 1, 3).reshape(s, w)


def qkv_project(x2, gain, w, pair_modes, cos, sin, q_gain, k_gain, batch, seq, *, tm, scale, dil):
    t, d = x2.shape
    n = w.shape[1]
    assert n == len(pair_modes) * PAIR
    ns = seq // tm
    per = tm // dil
    cos_r = _tile_residue_major(cos, dil, tm)
    sin_r = _tile_residue_major(sin, dil, tm)
    table_spec = pl.BlockSpec((tm, LANES), lambda i: (i % ns, 0))
    vec_spec = pl.BlockSpec((1, LANES), lambda i: (0, 0))
    out = pl.pallas_call(
        functools.partial(_qkv_kernel, scale=scale, dil=dil, pair_modes=tuple(pair_modes)),
        grid=(t // tm,),
        in_specs=[
            pl.BlockSpec((tm, d), lambda i: (i, 0)),
            pl.BlockSpec((1, d), lambda i: (0, 0)),
            pl.BlockSpec((d, n), lambda i: (0, 0)),
            table_spec, table_spec, vec_spec, vec_spec,
        ],
        out_specs=pl.BlockSpec((None, dil, None, per, n), lambda i: (i // ns, 0, i % ns, 0, 0)),
        out_shape=jax.ShapeDtypeStruct((batch, dil, ns, per, n), BF16),
        scratch_shapes=[pltpu.VMEM((tm, d), BF16)]
        + ([pltpu.VMEM((d // LANES, tm, LANES), F32)] if dil > 1 else []),
        compiler_params=_params(("parallel",)),
        name=f"qkv_project_d{dil}",
    )(x2, gain, w, cos_r, sin_r, q_gain, k_gain)
    return out.reshape(batch, dil, seq // dil, n)


def _matmul_residual_kernel(a_ref, w_ref, x_ref, o_ref):
    o_ref[...] = x_ref[...] + jnp.dot(a_ref[...], w_ref[...], preferred_element_type=F32)


def matmul_residual(a, w, x2, *, tm):
    t, k = a.shape
    n = w.shape[1]
    return pl.pallas_call(
        _matmul_residual_kernel,
        grid=(t // tm,),
        in_specs=[
            pl.BlockSpec((tm, k), lambda i: (i, 0)),
            pl.BlockSpec((k, n), lambda i: (0, 0)),
            pl.BlockSpec((tm, n), lambda i: (i, 0)),
        ],
        out_specs=pl.BlockSpec((tm, n), lambda i: (i, 0)),
        out_shape=jax.ShapeDtypeStruct((t, n), F32),
        compiler_params=_params(("parallel",)),
        name="matmul_residual",
    )(a, w, x2)


def _gqa_attn_kernel(q_ref, k_ref, v_ref, o_ref, *, tk, streams):
    tq = q_ref.shape[1]
    seq = k_ref.shape[1]
    per = GQA_GROUP // streams
    qs = [jnp.concatenate([q_ref[0, :, g * LANES:(g + 1) * LANES] for g in range(st * per, (st + 1) * per)],
                          axis=0) for st in range(streams)]
    rows = per * tq
    ones = jnp.ones((tk, LANES), BF16)

    def body(c, carry):
        start = pl.multiple_of(c * tk, tk)
        k = k_ref[0, pl.ds(start, tk), :]
        v1 = jnp.concatenate([v_ref[0, pl.ds(start, tk), :], ones], axis=1)
        new = []
        for st in range(streams):
            m_prev, acc = carry[st]
            s = lax.dot_general(qs[st], k, NT_DIMS, preferred_element_type=F32)
            m_new = jnp.maximum(m_prev, jnp.max(s, axis=-1, keepdims=True))
            alpha = jnp.exp2(m_prev - m_new)
            p = jnp.exp2(s - m_new).astype(BF16)
            acc = alpha * acc + jnp.dot(p, v1, preferred_element_type=F32)
            new.append((m_new, acc))
        return tuple(new)

    init = tuple((jnp.full((rows, 1), NEG_INF, F32), jnp.zeros((rows, 2 * LANES), F32)) for _ in range(streams))
    final = lax.fori_loop(0, seq // tk, body, init, unroll=16)
    for st in range(streams):
        acc = final[st][1]
        o = acc[:, :LANES] / acc[:, LANES:]
        for gi in range(per):
            g = st * per + gi
            o_ref[0, :, g * LANES:(g + 1) * LANES] = o[gi * tq:(gi + 1) * tq].astype(o_ref.dtype)


def gqa_attention(qkv, *, tq, tk, streams):
    b, s, _ = qkv.shape
    gw = GQA_GROUP * HEAD_DIM
    k_off = GQA_HEADS
    v_off = GQA_HEADS + GQA_KV_HEADS
    return pl.pallas_call(
        functools.partial(_gqa_attn_kernel, tk=tk, streams=streams),
        grid=(b, GQA_KV_HEADS, s // tq),
        in_specs=[
            pl.BlockSpec((1, tq, gw), lambda bb, n, i: (bb, i, n)),
            pl.BlockSpec((1, s, HEAD_DIM), lambda bb, n, i: (bb, 0, k_off + n)),
            pl.BlockSpec((1, s, HEAD_DIM), lambda bb, n, i: (bb, 0, v_off + n)),
        ],
        out_specs=pl.BlockSpec((1, tq, gw), lambda bb, n, i: (bb, i, n)),
        out_shape=jax.ShapeDtypeStruct((b, s, GQA_HEADS * HEAD_DIM), BF16),
        compiler_params=_params(("parallel", "parallel", "arbitrary")),
        name="gqa_attention",
    )(qkv, qkv, qkv)


def _dil_attn_kernel(q_ref, kp_ref, kc_ref, kn_ref, vp_ref, vc_ref, vn_ref, o_ref, lse_ref,
                     *, span, m_total):
    i = pl.program_id(2)
    tq = q_ref.shape[0]
    qb = 2 * span
    nk = qb + 2 * span
    row = lax.broadcasted_iota(jnp.int32, (qb, nk), 0)
    col = lax.broadcasted_iota(jnp.int32, (qb, nk), 1)
    in_band = jnp.abs(col - span - row) <= span
    lane = lax.broadcasted_iota(jnp.int32, (qb, LANES), 1)
    ones = jnp.ones((nk, LANES), BF16)
    for sub in range(tq // qb):
        key_idx = i * tq + sub * qb - span + col
        ok = in_band & (key_idx >= 0) & (key_idx < m_total)
        lse_tile = jnp.zeros((qb, LANES), F32)
        rows = slice(sub * qb, (sub + 1) * qb)
        win = slice(sub * qb, sub * qb + nk)
        heads = [slice(h * LANES, (h + 1) * LANES) for h in range(DIL_HEADS)]
        scores = [lax.dot_general(q_ref[rows, sl],
                                  jnp.concatenate([kp_ref[:, sl], kc_ref[:, sl], kn_ref[:, sl]], axis=0)[win],
                                  NT_DIMS, preferred_element_type=F32) for sl in heads]
        maxes, probs = [], []
        for s in scores:
            s = jnp.where(ok, s, NEG_INF)
            m = jnp.max(s, axis=-1, keepdims=True)
            maxes.append(m)
            probs.append(jnp.exp2(s - m).astype(BF16))
        accs = [jnp.dot(p, jnp.concatenate(
                    [jnp.concatenate([vp_ref[:, sl], vc_ref[:, sl], vn_ref[:, sl]], axis=0)[win], ones], axis=1),
                    preferred_element_type=F32) for p, sl in zip(probs, heads)]
        for h, (acc, m, sl) in enumerate(zip(accs, maxes, heads)):
            o_ref[rows, sl] = (acc[:, :LANES] / acc[:, LANES:]).astype(o_ref.dtype)
            lse_tile = jnp.where(lane == h, m + jnp.log2(acc[:, LANES:LANES + 1]), lse_tile)
        lse_ref[rows, :] = lse_tile


def dilated_attention(qkv, span, *, tq):
    b, dil, m_total, n = qkv.shape
    hw = DIL_HEADS * HEAD_DIM
    nb = m_total // tq
    halo_per_tile = tq // span
    n_halo = m_total // span

    def tile(col):
        return pl.BlockSpec((None, None, tq, hw), lambda bb, r, i: (bb, r, i, col))

    def before(col):
        return pl.BlockSpec((None, None, span, hw),
                            lambda bb, r, i: (bb, r, jnp.maximum(i * halo_per_tile - 1, 0), col))

    def after(col):
        return pl.BlockSpec((None, None, span, hw),
                            lambda bb, r, i: (bb, r, jnp.minimum((i + 1) * halo_per_tile, n_halo - 1), col))

    return pl.pallas_call(
        functools.partial(_dil_attn_kernel, span=span, m_total=m_total),
        grid=(b, dil, nb),
        in_specs=[tile(0), before(1), tile(1), after(1), before(2), tile(2), after(2)],
        out_specs=[
            pl.BlockSpec((None, None, tq, hw), lambda bb, r, i: (bb, r, i, 0)),
            pl.BlockSpec((None, None, tq, LANES), lambda bb, r, i: (bb, r, i, 0)),
        ],
        out_shape=[
            jax.ShapeDtypeStruct((b, dil, m_total, hw), BF16),
            jax.ShapeDtypeStruct((b, dil, m_total, LANES), F32),
        ],
        compiler_params=_params(("parallel", "parallel", "arbitrary")),
        name=f"dilated_attention_d{dil}",
    )(qkv, qkv, qkv, qkv, qkv, qkv, qkv)


def _dil_merge_kernel(o0_ref, o1_ref, o2_ref, l0_ref, l1_ref, l2_ref, w_ref, x_ref, out_ref, o_scr, l_scr):
    def to_tokens(ref, scr, g):
        dil, per, width = ref.shape
        for c in range(width // LANES):
            sl = slice(c * LANES, (c + 1) * LANES)
            if dil == 1:
                scr[g, c] = ref[0, :, sl].astype(F32)
            else:
                for r in range(dil):
                    scr[g, c, pl.ds(r, per, stride=dil), :] = ref[r, :, sl].astype(F32)

    for g, (o_ref, l_ref) in enumerate(((o0_ref, l0_ref), (o1_ref, l1_ref), (o2_ref, l2_ref))):
        to_tokens(o_ref, o_scr, g)
        to_tokens(l_ref, l_scr, g)
    lses = [l_scr[g, 0] for g in range(3)]
    mx = jnp.maximum(jnp.maximum(lses[0], lses[1]), lses[2])
    es = [jnp.exp2(l - mx) for l in lses]
    den = es[0] + es[1] + es[2]
    ws = [e / den for e in es]
    parts = []
    for h in range(DIL_HEADS):
        acc = None
        for g in range(3):
            term = o_scr[g, h] * ws[g][:, h:h + 1]
            acc = term if acc is None else acc + term
        parts.append(acc.astype(BF16))
    merged = jnp.concatenate(parts, axis=1)
    out_ref[...] = x_ref[...] + jnp.dot(merged, w_ref[...], preferred_element_type=F32)


def dilated_merge_project(outs, lses, w_out, x2, batch, seq, *, tm):
    t, d = x2.shape
    hw = w_out.shape[0]
    ns = seq // tm

    def grouped(arr):
        _, dil, m_total, w = arr.shape
        per = tm // dil
        view = arr.reshape(batch, dil, ns, per, w)
        return view, pl.BlockSpec((None, dil, None, per, w), lambda i: (i // ns, 0, i % ns, 0, 0))

    views, specs = zip(*[grouped(a) for a in (*outs, *lses)])
    row = pl.BlockSpec((tm, d), lambda i: (i, 0))
    return pl.pallas_call(
        _dil_merge_kernel,
        grid=(t // tm,),
        in_specs=list(specs) + [pl.BlockSpec((hw, d), lambda i: (0, 0)), row],
        out_specs=row,
        out_shape=jax.ShapeDtypeStruct((t, d), F32),
        scratch_shapes=[pltpu.VMEM((3, hw // LANES, tm, LANES), F32), pltpu.VMEM((3, 1, tm, LANES), F32)],
        compiler_params=_params(("parallel",)),
        name="dilated_merge_project",
    )(*views, w_out, x2)


def _rglru_scan_kernel(xc_ref, xp_ref, xn_ref, cw_ref, cb_ref, wa_ref, ba_ref, wx_ref, bx_ref, lam_ref,
                       o_ref, a_scr, u_scr, h_scr, *, reverse, n_steps):
    s = pl.program_id(1)
    ts, nb, tc = xc_ref.shape
    chunk = (n_steps - 1 - s) if reverse else s

    @pl.when(s == 0)
    def _():
        h_scr[...] = jnp.zeros(h_scr.shape, F32)

    prev = jnp.where(chunk == 0, 0.0, xp_ref[...])
    nxt = jnp.where(chunk == n_steps - 1, 0.0, xn_ref[...])
    xe = jnp.concatenate([prev, xc_ref[...], nxt], axis=0)
    cw = cw_ref[...]
    xc = cb_ref[...].reshape(1, 1, tc) + xe[0:ts] * cw[0:1].reshape(1, 1, tc)
    for j in range(1, CONV_W):
        xc = xc + xe[j:j + ts] * cw[j:j + 1].reshape(1, 1, tc)
    xc2 = xc.reshape(ts * nb, tc)
    lam = lam_ref[...]
    softplus_neg_lam = jnp.maximum(-lam, 0.0) + jnp.log1p(jnp.exp(-jnp.abs(lam)))
    decay_log2 = (-RG_C * LOG2_E) * softplus_neg_lam

    def sigmoid(x16, w_ref, bias):
        half_z = jnp.dot(x16, w_ref[kb] * 0.5, preferred_element_type=F32) + 0.5 * bias
        return 0.5 * jnp.tanh(half_z) + 0.5

    for kb in range(tc // RG_BLOCK_W):
        sl = slice(kb * RG_BLOCK_W, (kb + 1) * RG_BLOCK_W)
        xb = xc2[:, sl]
        xb16 = xb.astype(BF16)
        r = sigmoid(xb16, wa_ref, ba_ref[:, sl])
        gate = sigmoid(xb16, wx_ref, bx_ref[:, sl])
        a = jnp.exp2(r * decay_log2[:, sl])
        z = 1.0 - a * a
        u = (z * lax.rsqrt(jnp.maximum(z, 1e-30))) * (gate * xb)
        a_scr[:, :, sl] = a.reshape(ts, nb, RG_BLOCK_W)
        u_scr[:, :, sl] = u.reshape(ts, nb, RG_BLOCK_W)

    def step(t, h):
        tt = (ts - 1 - t) if reverse else t
        h = a_scr[tt] * h + u_scr[tt]
        o_ref[tt] = h
        return h

    h_scr[...] = lax.fori_loop(0, ts, step, h_scr[...], unroll=8)


def rglru_scan(yx3, conv_w, conv_b, w_a, b_a, w_x, b_x, lam, *, reverse, ts, tc):
    seq, nb, two_c = yx3.shape
    c = two_c // 2
    nc, n_steps = c // tc, seq // ts
    kb = tc // RG_BLOCK_W

    def chunk_of(s):
        return (n_steps - 1 - s) if reverse else s

    vec = lambda rows: pl.BlockSpec((rows, tc), lambda ci, s: (0, ci))
    mat = pl.BlockSpec((kb, RG_BLOCK_W, RG_BLOCK_W), lambda ci, s: (ci, 0, 0))
    return pl.pallas_call(
        functools.partial(_rglru_scan_kernel, reverse=reverse, n_steps=n_steps),
        grid=(nc, n_steps),
        in_specs=[
            pl.BlockSpec((ts, nb, tc), lambda ci, s: (chunk_of(s), 0, nc + ci)),
            pl.BlockSpec((CONV_LEFT, nb, tc),
                         lambda ci, s: (jnp.maximum(chunk_of(s) * (ts // CONV_LEFT) - 1, 0), 0, nc + ci)),
            pl.BlockSpec((1, nb, tc),
                         lambda ci, s: (jnp.minimum((chunk_of(s) + 1) * ts, seq - 1), 0, nc + ci)),
            vec(CONV_W), vec(1), mat, vec(1), mat, vec(1), vec(1),
        ],
        out_specs=pl.BlockSpec((ts, nb, tc), lambda ci, s: (chunk_of(s), 0, ci)),
        out_shape=jax.ShapeDtypeStruct((seq, nb, c), F32),
        scratch_shapes=[
            pltpu.VMEM((ts, nb, tc), F32),
            pltpu.VMEM((ts, nb, tc), F32),
            pltpu.VMEM((nb, tc), F32),
        ],
        compiler_params=_params(("parallel", "arbitrary")),
        name="rglru_scan_bwd" if reverse else "rglru_scan_fwd",
    )(yx3, yx3, yx3, conv_w, conv_b, w_a, b_a, w_x, b_x, lam)


def _gelu_tanh(y):
    return y * (0.5 * (1.0 + jnp.tanh(0.7978845608028654 * (y + 0.044715 * (y * y * y)))))


def _rglru_out_kernel(y_ref, hf_ref, hb_ref, w_ref, x_ref, o_ref):
    gated = _gelu_tanh(y_ref[...]) * (hf_ref[...] + hb_ref[...])
    o_ref[...] = x_ref[...] + jnp.dot(gated.astype(BF16), w_ref[...], preferred_element_type=F32)


def rglru_out_project(yx, hf, hb, w_out, x2, batch, seq, *, tm):
    c, d = w_out.shape
    ns = seq // tm
    return pl.pallas_call(
        _rglru_out_kernel,
        grid=(batch, ns),
        in_specs=[
            pl.BlockSpec((tm, c), lambda b, s: (s, 2 * b)),
            pl.BlockSpec((tm, c), lambda b, s: (s, b)),
            pl.BlockSpec((tm, c), lambda b, s: (s, b)),
            pl.BlockSpec((c, d), lambda b, s: (0, 0)),
            pl.BlockSpec((tm, d), lambda b, s: (b * ns + s, 0)),
        ],
        out_specs=pl.BlockSpec((tm, d), lambda b, s: (b * ns + s, 0)),
        out_shape=jax.ShapeDtypeStruct(x2.shape, F32),
        compiler_params=_params(("parallel", "parallel")),
        name="rglru_out_project",
    )(yx, hf, hb, w_out, x2)


def _swiglu_step(h, wg_ref, wu_ref, wd_ref, acc_scr):
    gate = jnp.dot(h, wg_ref[...].astype(BF16), preferred_element_type=F32)
    up = jnp.dot(h, wu_ref[...].astype(BF16), preferred_element_type=F32)
    act = gate * jax.nn.sigmoid(gate) * up
    acc_scr[...] += jnp.dot(act.astype(BF16), wd_ref[...].astype(BF16), preferred_element_type=F32)


def _ffn_kernel(x_ref, g_ref, wg_ref, wu_ref, wd_ref, o_ref, h_scr, acc_scr):
    f = pl.program_id(1)

    @pl.when(f == 0)
    def _():
        h_scr[...] = _rms(x_ref[...], g_ref[...]).astype(BF16)
        acc_scr[...] = jnp.zeros(acc_scr.shape, F32)

    _swiglu_step(h_scr[...], wg_ref, wu_ref, wd_ref, acc_scr)

    @pl.when(f == pl.num_programs(1) - 1)
    def _():
        o_ref[...] = x_ref[...] + acc_scr[...]


def swiglu_ffn(x2, gain, w_gate_up, w_down, *, tm, tf):
    t, d = x2.shape
    ff = w_down.shape[0]
    nf = ff // tf
    return pl.pallas_call(
        _ffn_kernel,
        grid=(t // tm, nf),
        in_specs=[
            pl.BlockSpec((tm, d), lambda i, f: (i, 0)),
            pl.BlockSpec((1, d), lambda i, f: (0, 0)),
            pl.BlockSpec((d, tf), lambda i, f: (0, f)),
            pl.BlockSpec((d, tf), lambda i, f: (0, nf + f)),
            pl.BlockSpec((tf, d), lambda i, f: (f, 0)),
        ],
        out_specs=pl.BlockSpec((tm, d), lambda i, f: (i, 0)),
        out_shape=jax.ShapeDtypeStruct((t, d), F32),
        scratch_shapes=[pltpu.VMEM((tm, d), BF16), pltpu.VMEM((tm, d), F32)],
        compiler_params=_params(("parallel", "arbitrary")),
        name="dense_ffn",
    )(x2, gain, w_gate_up, w_gate_up, w_down)


SUBLANES = 8


def _copy_token(src_hbm, dst, src_tok, dst_tok, sem):
    return pltpu.make_async_copy(src_hbm.at[pl.ds(src_tok * SUBLANES, SUBLANES)],
                                 dst.at[pl.ds(dst_tok * SUBLANES, SUBLANES)], sem)


def _moe_scatter_kernel(zero_ref, pos_ref, x_ref, o_hbm, stage, zeros_scr, sem, *, tm):
    p = pl.program_id(0)
    n = pl.num_programs(0)
    slot = p % 2
    chunks = x_ref.shape[1] // LANES
    block_rows = tm * SUBLANES

    def zero_copy(j):
        dst = o_hbm.at[pl.ds(zero_ref[j] * SUBLANES, zeros_scr.shape[0])]
        return pltpu.make_async_copy(zeros_scr, dst, sem.at[2])

    @pl.when(p == 0)
    def _():
        zeros_scr[...] = jnp.zeros(zeros_scr.shape, F32)
        for j in range(2 * N_EXPERTS):
            zero_copy(j).start()
            zero_copy(j).wait()

    def slot_wait(s):
        rows = pl.ds(0, 2 * block_rows)
        pltpu.make_async_copy(o_hbm.at[rows], o_hbm.at[rows], sem.at[s]).wait()

    @pl.when(p >= 2)
    def _():
        slot_wait(slot)

    for c in range(chunks):
        stage[slot, pl.ds(c, tm, stride=SUBLANES), :] = x_ref[:, c * LANES:(c + 1) * LANES]

    def issue(r, carry):
        src = stage.at[slot, pl.ds(r * SUBLANES, SUBLANES)]
        for k in range(TOP_K):
            dst = o_hbm.at[pl.ds(pos_ref[0, k * tm + r] * SUBLANES, SUBLANES)]
            pltpu.make_async_copy(src, dst, sem.at[slot]).start()
        return carry
    lax.fori_loop(0, tm, issue, 0, unroll=8)

    @pl.when(p == n - 1)
    def _():
        slot_wait(slot)

        @pl.when(n >= 2)
        def _():
            slot_wait(1 - slot)


def moe_scatter(x2, pos, zero_starts, n_tiles, *, tm, te):
    t, d = x2.shape
    grid_spec = pltpu.PrefetchScalarGridSpec(
        num_scalar_prefetch=1,
        grid=(t // tm,),
        in_specs=[
            pl.BlockSpec((None, 1, TOP_K * tm), lambda p, z: (p, 0, 0), memory_space=pltpu.SMEM),
            pl.BlockSpec((tm, d), lambda p, z: (p, 0)),
        ],
        out_specs=pl.BlockSpec(memory_space=pl.ANY),
        scratch_shapes=[pltpu.VMEM((2, tm * SUBLANES, LANES), F32), pltpu.VMEM((te * SUBLANES, LANES), F32),
                        pltpu.SemaphoreType.DMA((3,))],
    )
    return pl.pallas_call(
        functools.partial(_moe_scatter_kernel, tm=tm),
        grid_spec=grid_spec,
        out_shape=jax.ShapeDtypeStruct(((n_tiles + 1) * te * SUBLANES, LANES), F32),
        compiler_params=_params(("arbitrary",)),
        name="moe_scatter",
    )(zero_starts, pos, x2)


EXPERT_ROW_GROUPS = 8


def _moe_expert_kernel(te_ref, meta_ref, x0_ref, xn_ref, g_ref, wg_ref, wu_ref, wd_ref, y_ref, h_scr, acc_scr,
                       *, nf):
    p = pl.program_id(0)
    f = pl.program_id(1)
    tm = acc_scr.shape[0]
    chunks = acc_scr.shape[1] // LANES
    group = tm // EXPERT_ROW_GROUPS
    used = p < meta_ref[0]
    cur = p % 2

    def normalise_group(src_ref, g, slot):
        start = pl.multiple_of(g * group, group)
        x = jnp.concatenate([src_ref[pl.ds(start * SUBLANES + c, group, stride=SUBLANES), :]
                             for c in range(chunks)], axis=1)
        h_scr[slot, pl.ds(start, group), :] = _rms(x, g_ref[...]).astype(BF16)

    @pl.when((p == 0) & (f == 0))
    def _():
        for g in range(EXPERT_ROW_GROUPS):
            normalise_group(x0_ref, g, 0)

    @pl.when(used & (f == 0))
    def _():
        acc_scr[...] = jnp.zeros(acc_scr.shape, F32)

    @pl.when(used)
    def _():
        _swiglu_step(h_scr[cur], wg_ref, wu_ref, wd_ref, acc_scr)
        normalise_group(xn_ref, jnp.minimum(f, EXPERT_ROW_GROUPS - 1), 1 - cur)

    @pl.when(used & (f == nf - 1))
    def _():
        for g in range(min(nf, EXPERT_ROW_GROUPS), EXPERT_ROW_GROUPS):
            normalise_group(xn_ref, g, 1 - cur)

    @pl.when(f == nf - 1)
    def _():
        y = jnp.where(used, acc_scr[...], 0.0)
        for c in range(chunks):
            y_ref[pl.ds(c, tm, stride=SUBLANES), :] = y[:, c * LANES:(c + 1) * LANES]


def moe_experts(xs, gain, w_gate_up, w_down, tile_expert, meta, n_tiles, *, tm, tf):
    n_e, ff, d = w_down.shape
    nf = ff // tf

    def f_eff(p, f, meta_ref):
        return jnp.where(p < meta_ref[0], f, nf - 1)

    grid_spec = pltpu.PrefetchScalarGridSpec(
        num_scalar_prefetch=2,
        grid=(n_tiles, nf),
        in_specs=[
            pl.BlockSpec((tm * SUBLANES, LANES), lambda p, f, te, m: (0, 0)),
            pl.BlockSpec((tm * SUBLANES, LANES), lambda p, f, te, m: (jnp.minimum(p + 1, n_tiles - 1), 0)),
            pl.BlockSpec((1, d), lambda p, f, te, m: (0, 0)),
            pl.BlockSpec((None, d, tf), lambda p, f, te, m: (te[p], 0, f_eff(p, f, m))),
            pl.BlockSpec((None, d, tf), lambda p, f, te, m: (te[p], 0, nf + f_eff(p, f, m))),
            pl.BlockSpec((None, tf, d), lambda p, f, te, m: (te[p], f_eff(p, f, m), 0)),
        ],
        out_specs=pl.BlockSpec((tm * SUBLANES, LANES), lambda p, f, te, m: (p, 0)),
        scratch_shapes=[pltpu.VMEM((2, tm, d), BF16), pltpu.VMEM((tm, d), F32)],
    )
    return pl.pallas_call(
        functools.partial(_moe_expert_kernel, nf=nf),
        grid_spec=grid_spec,
        out_shape=jax.ShapeDtypeStruct((n_tiles * tm * SUBLANES, LANES), F32),
        compiler_params=_params(("arbitrary", "arbitrary")),
        name="moe_experts",
    )(tile_expert, meta, xs, xs, gain, w_gate_up, w_gate_up, w_down)


def _moe_combine_kernel(cur_ref, nxt_ref, x_ref, route_ref, y_hbm, o_ref, buf, sem, *, tm):
    p = pl.program_id(0)
    n = pl.num_programs(0)
    chunks = x_ref.shape[1] // LANES

    def issue(idx_ref, slot):
        def body(r, carry):
            _copy_token(y_hbm, buf.at[slot], idx_ref[0, r], r, sem.at[slot]).start()
            return carry
        lax.fori_loop(0, 2 * tm, body, 0, unroll=8)

    @pl.when(p == 0)
    def _():
        issue(cur_ref, 0)

    @pl.when(p + 1 < n)
    def _():
        issue(nxt_ref, (p + 1) % 2)

    slot = p % 2
    pltpu.make_async_copy(y_hbm.at[pl.ds(0, 2 * tm * SUBLANES)], buf.at[slot], sem.at[slot]).wait()
    route = route_ref[...]
    g1 = route[:, 2:3]
    g2 = route[:, 3:4]
    for c in range(chunks):
        first = buf[slot, pl.ds(c, tm, stride=SUBLANES), :]
        second = buf[slot, pl.ds(tm * SUBLANES + c, tm, stride=SUBLANES), :]
        sl = slice(c * LANES, (c + 1) * LANES)
        o_ref[:, sl] = x_ref[:, sl] + (g1 * first + g2 * second)


def moe_combine(x2, y_tiles, pos, route, *, tm):
    t, d = x2.shape
    n = t // tm
    idx_spec = lambda shift: pl.BlockSpec((None, 1, 2 * tm), lambda p: (jnp.minimum(p + shift, n - 1), 0, 0),
                                          memory_space=pltpu.SMEM)
    return pl.pallas_call(
        functools.partial(_moe_combine_kernel, tm=tm),
        grid=(n,),
        in_specs=[idx_spec(0), idx_spec(1), pl.BlockSpec((tm, d), lambda p: (p, 0)),
                  pl.BlockSpec((tm, LANES), lambda p: (p, 0)), pl.BlockSpec(memory_space=pl.ANY)],
        out_specs=pl.BlockSpec((tm, d), lambda p: (p, 0)),
        out_shape=jax.ShapeDtypeStruct((t, d), F32),
        scratch_shapes=[pltpu.VMEM((2, 2 * tm * SUBLANES, LANES), F32), pltpu.SemaphoreType.DMA((2,))],
        compiler_params=_params(("arbitrary",)),
        name="moe_combine",
    )(pos, pos, x2, route, y_tiles)


def _moe_plan(route, te, tm):
    t = route.shape[0]
    experts = route[:, :TOP_K].astype(jnp.int32)
    onehot = (experts[:, :, None] == jnp.arange(N_EXPERTS)[None, None, :]).astype(jnp.int32).sum(axis=1)
    incl = jnp.cumsum(onehot, axis=0)
    rank = incl - onehot
    counts = incl[-1]
    tiles_per = (counts + te - 1) // te
    tile_end = jnp.cumsum(tiles_per)
    row_start = (tile_end - tiles_per) * te
    dest = row_start[experts] + jnp.take_along_axis(rank, experts, axis=1)
    n_tiles = -(-(TOP_K * t) // te) + N_EXPERTS
    tile_expert = jnp.minimum(jnp.searchsorted(tile_end, jnp.arange(n_tiles), side="right"),
                              N_EXPERTS - 1).astype(jnp.int32)
    n_used = tile_end[-1]
    tail = jnp.minimum((n_used + jnp.arange(N_EXPERTS)) * te, n_tiles * te)
    zero_starts = jnp.concatenate([row_start + counts, tail]).astype(jnp.int32)
    pos = jnp.concatenate([dest[:, k].reshape(t // tm, tm) for k in range(TOP_K)], axis=1)
    return (pos.reshape(t // tm, 1, TOP_K * tm).astype(jnp.int32), zero_starts, tile_expert,
            n_used.reshape(1).astype(jnp.int32), n_tiles)


def _router_kernel(x_ref, g_ref, r_ref, b_ref, c_ref):
    h = _rms(x_ref[...], g_ref[...])
    r = r_ref[...]
    h_hi = h.astype(BF16)
    h_lo = (h - h_hi.astype(F32)).astype(BF16)
    r_hi = r.astype(BF16)
    r_lo = (r - r_hi.astype(F32)).astype(BF16)
    both = jnp.dot(h_hi, jnp.concatenate([r_hi, r_lo], axis=1), preferred_element_type=F32)
    logits = both[:, :LANES] + both[:, LANES:] + jnp.dot(h_lo, r_hi, preferred_element_type=F32)
    logits = logits + b_ref[...]
    lane = lax.broadcasted_iota(jnp.int32, logits.shape, 1)
    logits = jnp.where(lane < N_EXPERTS, logits, NEG_INF)
    v1 = jnp.max(logits, axis=-1, keepdims=True)
    i1 = jnp.min(jnp.where(logits == v1, lane, LANES), axis=-1, keepdims=True)
    rest = jnp.where(lane == i1, NEG_INF, logits)
    v2 = jnp.max(rest, axis=-1, keepdims=True)
    i2 = jnp.min(jnp.where(rest == v2, lane, LANES), axis=-1, keepdims=True)
    e2 = jnp.exp(v2 - v1)
    g1 = 1.0 / (1.0 + e2)
    g2 = e2 / (1.0 + e2)
    c_ref[...] = jnp.where(lane == 0, i1.astype(F32), jnp.where(lane == 1, i2.astype(F32),
                           jnp.where(lane == 2, g1, jnp.where(lane == 3, g2, 0.0))))


def moe_router(x2, gain, router, router_b, *, tm):
    t, d = x2.shape
    r_pad = jnp.zeros((d, LANES), F32).at[:, :N_EXPERTS].set(router)
    b_pad = jnp.zeros((1, LANES), F32).at[0, :N_EXPERTS].set(router_b)
    return pl.pallas_call(
        _router_kernel,
        grid=(t // tm,),
        in_specs=[
            pl.BlockSpec((tm, d), lambda i: (i, 0)),
            pl.BlockSpec((1, d), lambda i: (0, 0)),
            pl.BlockSpec((d, LANES), lambda i: (0, 0)),
            pl.BlockSpec((1, LANES), lambda i: (0, 0)),
        ],
        out_specs=pl.BlockSpec((tm, LANES), lambda i: (i, 0)),
        out_shape=jax.ShapeDtypeStruct((t, LANES), F32),
        compiler_params=_params(("parallel",)),
        name="moe_router",
    )(x2, gain, r_pad, b_pad)


def _inv_freq(dim, theta):
    return theta ** (-jnp.arange(0, dim, 2, dtype=F32) / dim)


_Q = HEAD_DIM // 4
AXIAL_PERM = np.concatenate([np.arange(0, _Q), np.arange(2 * _Q, 3 * _Q), np.arange(_Q, 2 * _Q),
                             np.arange(3 * _Q, 4 * _Q)])
_R = ROT_DIMS // 2
_REST = (HEAD_DIM - ROT_DIMS) // 2
PARTIAL_PERM = np.concatenate([np.arange(0, _R), np.arange(ROT_DIMS, ROT_DIMS + _REST), np.arange(_R, ROT_DIMS),
                               np.arange(ROT_DIMS + _REST, HEAD_DIM)])


def _axial_tables(seq):
    t = jnp.arange(seq)
    rows = (t // GRID_W).astype(F32)
    cols = (t % GRID_W).astype(F32)
    freq = _inv_freq(HEAD_DIM // 2, AXIAL_THETA)
    ang_r = rows[:, None] * freq[None, :]
    ang_c = cols[:, None] * freq[None, :]
    ang = jnp.concatenate([ang_r, ang_c, ang_r, ang_c], axis=-1)
    sign = jnp.concatenate([-jnp.ones(ROPE_SHIFT, F32), jnp.ones(ROPE_SHIFT, F32)])
    return jnp.cos(ang), jnp.sin(ang) * sign[None, :]


def _partial_tables(seq):
    pos = jnp.arange(seq, dtype=F32)
    freq = _inv_freq(ROT_DIMS, ROPE_THETA)
    ang = pos[:, None] * freq[None, :]
    ones = jnp.ones((seq, _REST), F32)
    zeros = jnp.zeros((seq, _REST), F32)
    cos = jnp.concatenate([jnp.cos(ang), ones, jnp.cos(ang), ones], axis=-1)
    sin = jnp.concatenate([-jnp.sin(ang), zeros, jnp.sin(ang), zeros], axis=-1)
    return cos, sin


def _permute_qk_columns(w, head_is_qk, perm):
    idx = np.concatenate([h * HEAD_DIM + (perm if qk else np.arange(HEAD_DIM))
                          for h, qk in enumerate(head_is_qk)])
    return jnp.take(w, jnp.asarray(idx, jnp.int32), axis=1)


def _row(v):
    return v.reshape(1, -1).astype(F32)


def _rglru_layer(x2, batch, seq, norm, w_in, conv_w, conv_b, w_a, b_a, w_x, b_x, lam, w_out):
    c = w_out.shape[0]
    yx = norm_matmul_time_major(x2, _row(norm), w_in.astype(BF16), batch, seq,
                                tm=512, tn=2048, out_dtype=F32)
    yx3 = yx.reshape(seq, batch, 2 * c)
    hs = []
    for direction in range(2):
        hs.append(rglru_scan(
            yx3, conv_w, _row(conv_b), w_a[direction].astype(BF16), _row(b_a[direction]),
            w_x[direction].astype(BF16), _row(b_x[direction]), _row(lam[direction]),
            reverse=direction == 1, ts=256, tc=512).reshape(seq, batch * c))
    return rglru_out_project(yx, hs[0], hs[1], w_out.astype(BF16), x2, batch, seq, tm=512)


def _gqa_layer(x2, batch, seq, norm, w_qkv, q_gain, k_gain, w_out):
    nq = GQA_HEADS * HEAD_DIM
    nkv = GQA_KV_HEADS * HEAD_DIM
    modes = [MODE_QUERY] * (nq // PAIR) + [MODE_KEY] * (nkv // PAIR) + [MODE_PLAIN] * (nkv // PAIR)
    cos, sin = _axial_tables(seq)
    is_qk = [True] * (GQA_HEADS + GQA_KV_HEADS) + [False] * GQA_KV_HEADS
    w16 = _permute_qk_columns(w_qkv, is_qk, AXIAL_PERM).astype(BF16)
    qkv = qkv_project(x2, _row(norm), w16, modes, cos, sin, _row(q_gain[AXIAL_PERM]), _row(k_gain[AXIAL_PERM]),
                      batch, seq, tm=512, scale=HEAD_DIM ** -0.5 * LOG2_E, dil=1)
    o = gqa_attention(qkv.reshape(batch, seq, nq + 2 * nkv), tq=512, tk=256, streams=2)
    return matmul_residual(o.reshape(batch * seq, nq), w_out.astype(BF16), x2, tm=512)


def _dilated_layer(x2, batch, seq, norm, w_qkv, q_gain, k_gain, w_out):
    hw = DIL_HEADS * HEAD_DIM
    modes = [MODE_QUERY] * (hw // PAIR) + [MODE_KEY] * (hw // PAIR) + [MODE_PLAIN] * (hw // PAIR)
    cos, sin = _partial_tables(seq)
    is_qk = ([True] * (2 * DIL_HEADS) + [False] * DIL_HEADS) * len(DIL_PATTERNS)
    w16 = _permute_qk_columns(w_qkv, is_qk, PARTIAL_PERM).astype(BF16)
    outs, lses = [], []
    for g, (window, dil) in enumerate(DIL_PATTERNS):
        span = window // (2 * dil)
        assert seq % (dil * span) == 0
        qkv = qkv_project(x2, _row(norm), w16[:, g * 3 * hw:(g + 1) * 3 * hw], modes, cos, sin,
                          _row(q_gain[PARTIAL_PERM]), _row(k_gain[PARTIAL_PERM]), batch, seq,
                          tm=512, scale=HEAD_DIM ** -0.5 * LOG2_E, dil=dil)
        o, lse = dilated_attention(qkv, span, tq=min(4 * span, seq // dil))
        outs.append(o)
        lses.append(lse)
    return dilated_merge_project(outs, lses, w_out.astype(BF16), x2, batch, seq, tm=512)


def _dense_ffn_layer(x2, norm, w_gate_up, w_down):
    return swiglu_ffn(x2, _row(norm), w_gate_up.astype(BF16), w_down.astype(BF16), tm=1024, tf=512)


def _moe_layer(x2, norm, router, router_b, w_gate_up, w_down):
    tm, te = 512, 1024
    route = moe_router(x2, _row(norm), router, router_b, tm=tm)
    pos, zero_starts, tile_expert, meta, n_tiles = _moe_plan(route, te, tm)
    xs = moe_scatter(x2, pos, zero_starts, n_tiles, tm=tm, te=te)
    ys = moe_experts(xs, _row(norm), w_gate_up, w_down, tile_expert, meta, n_tiles,
                     tm=te, tf=512)
    return moe_combine(x2, ys, pos, route, tm=tm)


def kernel(x, l0_attn_norm, l0_rg_w_in, l0_rg_conv_w, l0_rg_conv_b, l0_rg_w_a, l0_rg_b_a, l0_rg_w_x, l0_rg_b_x, l0_rg_lambda, l0_rg_w_out, l0_ffn_norm, l0_ffn_w_gate_up, l0_ffn_w_down, l1_attn_norm, l1_gqa_w_qkv, l1_gqa_q_norm, l1_gqa_k_norm, l1_gqa_w_out, l1_ffn_norm, l1_moe_router, l1_moe_router_b, l1_moe_w_gate_up, l1_moe_w_down, l2_attn_norm, l2_dil_w_qkv, l2_dil_q_norm, l2_dil_k_norm, l2_dil_w_out, l2_ffn_norm, l2_ffn_w_gate_up, l2_ffn_w_down, l3_attn_norm, l3_rg_w_in, l3_rg_conv_w, l3_rg_conv_b, l3_rg_w_a, l3_rg_b_a, l3_rg_w_x, l3_rg_b_x, l3_rg_lambda, l3_rg_w_out, l3_ffn_norm, l3_moe_router, l3_moe_router_b, l3_moe_w_gate_up, l3_moe_w_down):
    batch, seq, d = x.shape
    x2 = x.reshape(batch * seq, d)
    x2 = _rglru_layer(x2, batch, seq, l0_attn_norm, l0_rg_w_in, l0_rg_conv_w, l0_rg_conv_b, l0_rg_w_a,
                      l0_rg_b_a, l0_rg_w_x, l0_rg_b_x, l0_rg_lambda, l0_rg_w_out)
    x2 = _dense_ffn_layer(x2, l0_ffn_norm, l0_ffn_w_gate_up, l0_ffn_w_down)
    x2 = _gqa_layer(x2, batch, seq, l1_attn_norm, l1_gqa_w_qkv, l1_gqa_q_norm, l1_gqa_k_norm, l1_gqa_w_out)
    x2 = _moe_layer(x2, l1_ffn_norm, l1_moe_router, l1_moe_router_b, l1_moe_w_gate_up, l1_moe_w_down)
    x2 = _dilated_layer(x2, batch, seq, l2_attn_norm, l2_dil_w_qkv, l2_dil_q_norm, l2_dil_k_norm, l2_dil_w_out)
    x2 = _dense_ffn_layer(x2, l2_ffn_norm, l2_ffn_w_gate_up, l2_ffn_w_down)
    x2 = _rglru_layer(x2, batch, seq, l3_attn_norm, l3_rg_w_in, l3_rg_conv_w, l3_rg_conv_b, l3_rg_w_a,
                      l3_rg_b_a, l3_rg_w_x, l3_rg_b_x, l3_rg_lambda, l3_rg_w_out)
    x2 = _moe_layer(x2, l3_ffn_norm, l3_moe_router, l3_moe_router_b, l3_moe_w_gate_up, l3_moe_w_down)
    return x2.reshape(batch, seq, d)
```

```python
import functools

import jax
import jax.numpy as jnp
import numpy as np
from jax import lax
from jax.experimental import pallas as pl
from jax.experimental.pallas import tpu as pltpu

F32 = jnp.float32
BF16 = jnp.bfloat16

EPS = 1e-6
NEG_INF = -1e30
GRID_W = 64
HEAD_DIM = 128
LANES = 128
RG_BLOCK_W = 128
CONV_W = 4
CONV_LEFT = 2
RG_C = 8.0
GQA_HEADS = 8
GQA_KV_HEADS = 2
GQA_GROUP = GQA_HEADS // GQA_KV_HEADS
AXIAL_THETA = 10000.0
DIL_PATTERNS = ((128, 1), (512, 4), (2048, 16))
DIL_HEADS = 8
ROPE_THETA = 500000.0
ROT_DIMS = HEAD_DIM // 4
N_EXPERTS = 8
TOP_K = 2
VMEM_LIMIT = 56 * 1024 * 1024

NT_DIMS = (((1,), (1,)), ((), ()))
LOG2_E = 1.4426950408889634


def _params(sem, vmem=VMEM_LIMIT):
    return pltpu.CompilerParams(dimension_semantics=sem, vmem_limit_bytes=vmem)


def _rms(x, g):
    return x * lax.rsqrt(jnp.mean(x * x, axis=-1, keepdims=True) + EPS) * g


def _norm_matmul_kernel(x_ref, g_ref, w_ref, o_ref, h_scr):
    @pl.when(pl.program_id(2) == 0)
    def _():
        h_scr[...] = _rms(x_ref[...], g_ref[...]).astype(BF16)

    o_ref[...] = jnp.dot(h_scr[...], w_ref[...], preferred_element_type=F32).astype(o_ref.dtype)


def norm_matmul_time_major(x2, gain, w, batch, seq, *, tm, tn, out_dtype):
    d = x2.shape[1]
    n = w.shape[1]
    ns, nj = seq // tm, n // tn
    return pl.pallas_call(
        _norm_matmul_kernel,
        grid=(batch, ns, nj),
        in_specs=[
            pl.BlockSpec((tm, d), lambda b, s, j: (b * ns + s, 0)),
            pl.BlockSpec((1, d), lambda b, s, j: (0, 0)),
            pl.BlockSpec((d, tn), lambda b, s, j: (0, j)),
        ],
        out_specs=pl.BlockSpec((tm, tn), lambda b, s, j: (s, b * nj + j)),
        out_shape=jax.ShapeDtypeStruct((seq, batch * n), out_dtype),
        scratch_shapes=[pltpu.VMEM((tm, d), BF16)],
        compiler_params=_params(("parallel", "parallel", "arbitrary")),
        name="norm_matmul_tm",
    )(x2, gain, w)


ROPE_SHIFT = LANES // 2


PAIR = 2 * LANES
MODE_PLAIN, MODE_QUERY, MODE_KEY = 0, 1, 2


def _qkv_kernel(x_ref, g_ref, w_ref, cos_ref, sin_ref, qg_ref, kg_ref, o_ref, h_scr, *stage,
                scale, dil, pair_modes):
    tm = x_ref.shape[0]
    per = tm // dil
    hn = _rms(x_ref[...], g_ref[...])
    if dil == 1:
        h_scr[...] = hn.astype(BF16)
    else:
        stage_scr, = stage
        for c in range(hn.shape[1] // LANES):
            sl = slice(c * LANES, (c + 1) * LANES)
            stage_scr[c] = hn[:, sl]
            for r in range(dil):
                h_scr[r * per:(r + 1) * per, sl] = stage_scr[c, pl.ds(r, per, stride=dil), :].astype(BF16)
    h = h_scr[...]

    row = lax.broadcasted_iota(jnp.int32, (PAIR, PAIR), 0)
    col = lax.broadcasted_iota(jnp.int32, (PAIR, PAIR), 1)
    head_ones = (row // LANES == col // LANES).astype(BF16)
    cos = cos_ref[...]
    sin = sin_ref[...]
    tables = {}
    for mode, gain, sc in ((MODE_QUERY, qg_ref[...], scale), (MODE_KEY, kg_ref[...], 1.0)):
        if mode in pair_modes:
            tables[mode] = (cos * (gain * sc), sin * (pltpu.roll(gain, ROPE_SHIFT, 1) * sc))

    accs = [jnp.dot(h, w_ref[:, i * PAIR:(i + 1) * PAIR], preferred_element_type=F32)
            for i in range(len(pair_modes))]
    sumsqs = [None if mode == MODE_PLAIN else
              jnp.dot((a * a).astype(BF16), head_ones, preferred_element_type=F32)
              for a, mode in zip(accs, pair_modes)]
    for i, (a, sumsq, mode) in enumerate(zip(accs, sumsqs, pair_modes)):
        for c in range(PAIR // LANES):
            sl = slice(c * LANES, (c + 1) * LANES)
            head = a[:, sl]
            if mode != MODE_PLAIN:
                c1, s1 = tables[mode]
                inv = lax.rsqrt(sumsq[:, sl] * (1.0 / HEAD_DIM) + EPS)
                head = (head * c1 + pltpu.roll(head, ROPE_SHIFT, 1) * s1) * inv
            out_sl = slice(i * PAIR + c * LANES, i * PAIR + (c + 1) * LANES)
            o_ref[:, :, out_sl] = head.reshape(dil, per, LANES).astype(o_ref.dtype)


def _tile_residue_major(table, dil, tm):
    s, w = table.shape
    return table.reshape(s // tm, tm // dil, dil, w).transpose(0, 2, 1, 3).reshape(s, w)


def qkv_project(x2, gain, w, pair_modes, cos, sin, q_gain, k_gain, batch, seq, *, tm, scale, dil):
    t, d = x2.shape
    n = w.shape[1]
    assert n == len(pair_modes) * PAIR
    ns = seq // tm
    per = tm // dil
    cos_r = _tile_residue_major(cos, dil, tm)
    sin_r = _tile_residue_major(sin, dil, tm)
    table_spec = pl.BlockSpec((tm, LANES), lambda i: (i % ns, 0))
    vec_spec = pl.BlockSpec((1, LANES), lambda i: (0, 0))
    out = pl.pallas_call(
        functools.partial(_qkv_kernel, scale=scale, dil=dil, pair_modes=tuple(pair_modes)),
        grid=(t // tm,),
        in_specs=[
            pl.BlockSpec((tm, d), lambda i: (i, 0)),
            pl.BlockSpec((1, d), lambda i: (0, 0)),
            pl.BlockSpec((d, n), lambda i: (0, 0)),
            table_spec, table_spec, vec_spec, vec_spec,
        ],
        out_specs=pl.BlockSpec((None, dil, None, per, n), lambda i: (i // ns, 0, i % ns, 0, 0)),
        out_shape=jax.ShapeDtypeStruct((batch, dil, ns, per, n), BF16),
        scratch_shapes=[pltpu.VMEM((tm, d), BF16)]
        + ([pltpu.VMEM((d // LANES, tm, LANES), F32)] if dil > 1 else []),
        compiler_params=_params(("parallel",)),
        name=f"qkv_project_d{dil}",
    )(x2, gain, w, cos_r, sin_r, q_gain, k_gain)
    return out.reshape(batch, dil, seq // dil, n)


def _matmul_residual_kernel(a_ref, w_ref, x_ref, o_ref):
    o_ref[...] = x_ref[...] + jnp.dot(a_ref[...], w_ref[...], preferred_element_type=F32)


def matmul_residual(a, w, x2, *, tm):
    t, k = a.shape
    n = w.shape[1]
    return pl.pallas_call(
        _matmul_residual_kernel,
        grid=(t // tm,),
        in_specs=[
            pl.BlockSpec((tm, k), lambda i: (i, 0)),
            pl.BlockSpec((k, n), lambda i: (0, 0)),
            pl.BlockSpec((tm, n), lambda i: (i, 0)),
        ],
        out_specs=pl.BlockSpec((tm, n), lambda i: (i, 0)),
        out_shape=jax.ShapeDtypeStruct((t, n), F32),
        compiler_params=_params(("parallel",)),
        name="matmul_residual",
    )(a, w, x2)


def _gqa_attn_kernel(q_ref, k_ref, v_ref, o_ref, *, tk, streams):
    tq = q_ref.shape[1]
    seq = k_ref.shape[1]
    per = GQA_GROUP // streams
    qs = [jnp.concatenate([q_ref[0, :, g * LANES:(g + 1) * LANES] for g in range(st * per, (st + 1) * per)],
                          axis=0) for st in range(streams)]
    rows = per * tq
    ones = jnp.ones((tk, LANES), BF16)

    def body(c, carry):
        start = pl.multiple_of(c * tk, tk)
        k = k_ref[0, pl.ds(start, tk), :]
        v1 = jnp.concatenate([v_ref[0, pl.ds(start, tk), :], ones], axis=1)
        new = []
        for st in range(streams):
            m_prev, acc = carry[st]
            s = lax.dot_general(qs[st], k, NT_DIMS, preferred_element_type=F32)
            m_new = jnp.maximum(m_prev, jnp.max(s, axis=-1, keepdims=True))
            alpha = jnp.exp2(m_prev - m_new)
            p = jnp.exp2(s - m_new).astype(BF16)
            acc = alpha * acc + jnp.dot(p, v1, preferred_element_type=F32)
            new.append((m_new, acc))
        return tuple(new)

    init = tuple((jnp.full((rows, 1), NEG_INF, F32), jnp.zeros((rows, 2 * LANES), F32)) for _ in range(streams))
    final = lax.fori_loop(0, seq // tk, body, init, unroll=16)
    for st in range(streams):
        acc = final[st][1]
        o = acc[:, :LANES] / acc[:, LANES:]
        for gi in range(per):
            g = st * per + gi
            o_ref[0, :, g * LANES:(g + 1) * LANES] = o[gi * tq:(gi + 1) * tq].astype(o_ref.dtype)


def gqa_attention(qkv, *, tq, tk, streams):
    b, s, _ = qkv.shape
    gw = GQA_GROUP * HEAD_DIM
    k_off = GQA_HEADS
    v_off = GQA_HEADS + GQA_KV_HEADS
    return pl.pallas_call(
        functools.partial(_gqa_attn_kernel, tk=tk, streams=streams),
        grid=(b, GQA_KV_HEADS, s // tq),
        in_specs=[
            pl.BlockSpec((1, tq, gw), lambda bb, n, i: (bb, i, n)),
            pl.BlockSpec((1, s, HEAD_DIM), lambda bb, n, i: (bb, 0, k_off + n)),
            pl.BlockSpec((1, s, HEAD_DIM), lambda bb, n, i: (bb, 0, v_off + n)),
        ],
        out_specs=pl.BlockSpec((1, tq, gw), lambda bb, n, i: (bb, i, n)),
        out_shape=jax.ShapeDtypeStruct((b, s, GQA_HEADS * HEAD_DIM), BF16),
        compiler_params=_params(("parallel", "parallel", "arbitrary")),
        name="gqa_attention",
    )(qkv, qkv, qkv)


def _dil_attn_kernel(q_ref, kp_ref, kc_ref, kn_ref, vp_ref, vc_ref, vn_ref, o_ref, lse_ref,
                     *, span, m_total):
    i = pl.program_id(2)
    tq = q_ref.shape[0]
    qb = 2 * span
    nk = qb + 2 * span
    row = lax.broadcasted_iota(jnp.int32, (qb, nk), 0)
    col = lax.broadcasted_iota(jnp.int32, (qb, nk), 1)
    in_band = jnp.abs(col - span - row) <= span
    lane = lax.broadcasted_iota(jnp.int32, (qb, LANES), 1)
    ones = jnp.ones((nk, LANES), BF16)
    for sub in range(tq // qb):
        key_idx = i * tq + sub * qb - span + col
        ok = in_band & (key_idx >= 0) & (key_idx < m_total)
        lse_tile = jnp.zeros((qb, LANES), F32)
        rows = slice(sub * qb, (sub + 1) * qb)
        win = slice(sub * qb, sub * qb + nk)
        heads = [slice(h * LANES, (h + 1) * LANES) for h in range(DIL_HEADS)]
        scores = [lax.dot_general(q_ref[rows, sl],
                                  jnp.concatenate([kp_ref[:, sl], kc_ref[:, sl], kn_ref[:, sl]], axis=0)[win],
                                  NT_DIMS, preferred_element_type=F32) for sl in heads]
        maxes, probs = [], []
        for s in scores:
            s = jnp.where(ok, s, NEG_INF)
            m = jnp.max(s, axis=-1, keepdims=True)
            maxes.append(m)
            probs.append(jnp.exp2(s - m).astype(BF16))
        accs = [jnp.dot(p, jnp.concatenate(
                    [jnp.concatenate([vp_ref[:, sl], vc_ref[:, sl], vn_ref[:, sl]], axis=0)[win], ones], axis=1),
                    preferred_element_type=F32) for p, sl in zip(probs, heads)]
        for h, (acc, m, sl) in enumerate(zip(accs, maxes, heads)):
            o_ref[rows, sl] = (acc[:, :LANES] / acc[:, LANES:]).astype(o_ref.dtype)
            lse_tile = jnp.where(lane == h, m + jnp.log2(acc[:, LANES:LANES + 1]), lse_tile)
        lse_ref[rows, :] = lse_tile


def dilated_attention(qkv, span, *, tq):
    b, dil, m_total, n = qkv.shape
    hw = DIL_HEADS * HEAD_DIM
    nb = m_total // tq
    halo_per_tile = tq // span
    n_halo = m_total // span

    def tile(col):
        return pl.BlockSpec((None, None, tq, hw), lambda bb, r, i: (bb, r, i, col))

    def before(col):
        return pl.BlockSpec((None, None, span, hw),
                            lambda bb, r, i: (bb, r, jnp.maximum(i * halo_per_tile - 1, 0), col))

    def after(col):
        return pl.BlockSpec((None, None, span, hw),
                            lambda bb, r, i: (bb, r, jnp.minimum((i + 1) * halo_per_tile, n_halo - 1), col))

    return pl.pallas_call(
        functools.partial(_dil_attn_kernel, span=span, m_total=m_total),
        grid=(b, dil, nb),
        in_specs=[tile(0), before(1), tile(1), after(1), before(2), tile(2), after(2)],
        out_specs=[
            pl.BlockSpec((None, None, tq, hw), lambda bb, r, i: (bb, r, i, 0)),
            pl.BlockSpec((None, None, tq, LANES), lambda bb, r, i: (bb, r, i, 0)),
        ],
        out_shape=[
            jax.ShapeDtypeStruct((b, dil, m_total, hw), BF16),
            jax.ShapeDtypeStruct((b, dil, m_total, LANES), F32),
        ],
        compiler_params=_params(("parallel", "parallel", "arbitrary")),
        name=f"dilated_attention_d{dil}",
    )(qkv, qkv, qkv, qkv, qkv, qkv, qkv)


def _dil_merge_kernel(o0_ref, o1_ref, o2_ref, l0_ref, l1_ref, l2_ref, w_ref, x_ref, out_ref, o_scr, l_scr):
    def to_tokens(ref, scr, g):
        dil, per, width = ref.shape
        for c in range(width // LANES):
            sl = slice(c * LANES, (c + 1) * LANES)
            if dil == 1:
                scr[g, c] = ref[0, :, sl].astype(F32)
            else:
                for r in range(dil):
                    scr[g, c, pl.ds(r, per, stride=dil), :] = ref[r, :, sl].astype(F32)

    for g, (o_ref, l_ref) in enumerate(((o0_ref, l0_ref), (o1_ref, l1_ref), (o2_ref, l2_ref))):
        to_tokens(o_ref, o_scr, g)
        to_tokens(l_ref, l_scr, g)
    lses = [l_scr[g, 0] for g in range(3)]
    mx = jnp.maximum(jnp.maximum(lses[0], lses[1]), lses[2])
    es = [jnp.exp2(l - mx) for l in lses]
    den = es[0] + es[1] + es[2]
    ws = [e / den for e in es]
    parts = []
    for h in range(DIL_HEADS):
        acc = None
        for g in range(3):
            term = o_scr[g, h] * ws[g][:, h:h + 1]
            acc = term if acc is None else acc + term
        parts.append(acc.astype(BF16))
    merged = jnp.concatenate(parts, axis=1)
    out_ref[...] = x_ref[...] + jnp.dot(merged, w_ref[...], preferred_element_type=F32)


def dilated_merge_project(outs, lses, w_out, x2, batch, seq, *, tm):
    t, d = x2.shape
    hw = w_out.shape[0]
    ns = seq // tm

    def grouped(arr):
        _, dil, m_total, w = arr.shape
        per = tm // dil
        view = arr.reshape(batch, dil, ns, per, w)
        return view, pl.BlockSpec((None, dil, None, per, w), lambda i: (i // ns, 0, i % ns, 0, 0))

    views, specs = zip(*[grouped(a) for a in (*outs, *lses)])
    row = pl.BlockSpec((tm, d), lambda i: (i, 0))
    return pl.pallas_call(
        _dil_merge_kernel,
        grid=(t // tm,),
        in_specs=list(specs) + [pl.BlockSpec((hw, d), lambda i: (0, 0)), row],
        out_specs=row,
        out_shape=jax.ShapeDtypeStruct((t, d), F32),
        scratch_shapes=[pltpu.VMEM((3, hw // LANES, tm, LANES), F32), pltpu.VMEM((3, 1, tm, LANES), F32)],
        compiler_params=_params(("parallel",)),
        name="dilated_merge_project",
    )(*views, w_out, x2)


def _rglru_scan_kernel(xc_ref, xp_ref, xn_ref, cw_ref, cb_ref, wa_ref, ba_ref, wx_ref, bx_ref, lam_ref,
                       o_ref, a_scr, u_scr, h_scr, *, reverse, n_steps):
    s = pl.program_id(1)
    ts, nb, tc = xc_ref.shape
    chunk = (n_steps - 1 - s) if reverse else s

    @pl.when(s == 0)
    def _():
        h_scr[...] = jnp.zeros(h_scr.shape, F32)

    prev = jnp.where(chunk == 0, 0.0, xp_ref[...])
    nxt = jnp.where(chunk == n_steps - 1, 0.0, xn_ref[...])
    xe = jnp.concatenate([prev, xc_ref[...], nxt], axis=0)
    cw = cw_ref[...]
    xc = cb_ref[...].reshape(1, 1, tc) + xe[0:ts] * cw[0:1].reshape(1, 1, tc)
    for j in range(1, CONV_W):
        xc = xc + xe[j:j + ts] * cw[j:j + 1].reshape(1, 1, tc)
    xc2 = xc.reshape(ts * nb, tc)
    lam = lam_ref[...]
    softplus_neg_lam = jnp.maximum(-lam, 0.0) + jnp.log1p(jnp.exp(-jnp.abs(lam)))
    decay_log2 = (-RG_C * LOG2_E) * softplus_neg_lam

    def sigmoid(x16, w_ref, bias):
        half_z = jnp.dot(x16, w_ref[kb] * 0.5, preferred_element_type=F32) + 0.5 * bias
        return 0.5 * jnp.tanh(half_z) + 0.5

    for kb in range(tc // RG_BLOCK_W):
        sl = slice(kb * RG_BLOCK_W, (kb + 1) * RG_BLOCK_W)
        xb = xc2[:, sl]
        xb16 = xb.astype(BF16)
        r = sigmoid(xb16, wa_ref, ba_ref[:, sl])
        gate = sigmoid(xb16, wx_ref, bx_ref[:, sl])
        a = jnp.exp2(r * decay_log2[:, sl])
        z = 1.0 - a * a
        u = (z * lax.rsqrt(jnp.maximum(z, 1e-30))) * (gate * xb)
        a_scr[:, :, sl] = a.reshape(ts, nb, RG_BLOCK_W)
        u_scr[:, :, sl] = u.reshape(ts, nb, RG_BLOCK_W)

    def step(t, h):
        tt = (ts - 1 - t) if reverse else t
        h = a_scr[tt] * h + u_scr[tt]
        o_ref[tt] = h
        return h

    h_scr[...] = lax.fori_loop(0, ts, step, h_scr[...], unroll=8)


def rglru_scan(yx3, conv_w, conv_b, w_a, b_a, w_x, b_x, lam, *, reverse, ts, tc):
    seq, nb, two_c = yx3.shape
    c = two_c // 2
    nc, n_steps = c // tc, seq // ts
    kb = tc // RG_BLOCK_W

    def chunk_of(s):
        return (n_steps - 1 - s) if reverse else s

    vec = lambda rows: pl.BlockSpec((rows, tc), lambda ci, s: (0, ci))
    mat = pl.BlockSpec((kb, RG_BLOCK_W, RG_BLOCK_W), lambda ci, s: (ci, 0, 0))
    return pl.pallas_call(
        functools.partial(_rglru_scan_kernel, reverse=reverse, n_steps=n_steps),
        grid=(nc, n_steps),
        in_specs=[
            pl.BlockSpec((ts, nb, tc), lambda ci, s: (chunk_of(s), 0, nc + ci)),
            pl.BlockSpec((CONV_LEFT, nb, tc),
                         lambda ci, s: (jnp.maximum(chunk_of(s) * (ts // CONV_LEFT) - 1, 0), 0, nc + ci)),
            pl.BlockSpec((1, nb, tc),
                         lambda ci, s: (jnp.minimum((chunk_of(s) + 1) * ts, seq - 1), 0, nc + ci)),
            vec(CONV_W), vec(1), mat, vec(1), mat, vec(1), vec(1),
        ],
        out_specs=pl.BlockSpec((ts, nb, tc), lambda ci, s: (chunk_of(s), 0, ci)),
        out_shape=jax.ShapeDtypeStruct((seq, nb, c), F32),
        scratch_shapes=[
            pltpu.VMEM((ts, nb, tc), F32),
            pltpu.VMEM((ts, nb, tc), F32),
            pltpu.VMEM((nb, tc), F32),
        ],
        compiler_params=_params(("parallel", "arbitrary")),
        name="rglru_scan_bwd" if reverse else "rglru_scan_fwd",
    )(yx3, yx3, yx3, conv_w, conv_b, w_a, b_a, w_x, b_x, lam)


def _gelu_tanh(y):
    return y * (0.5 * (1.0 + jnp.tanh(0.7978845608028654 * (y + 0.044715 * (y * y * y)))))


def _rglru_out_kernel(y_ref, hf_ref, hb_ref, w_ref, x_ref, o_ref):
    gated = _gelu_tanh(y_ref[...]) * (hf_ref[...] + hb_ref[...])
    o_ref[...] = x_ref[...] + jnp.dot(gated.astype(BF16), w_ref[...], preferred_element_type=F32)


def rglru_out_project(yx, hf, hb, w_out, x2, batch, seq, *, tm):
    c, d = w_out.shape
    ns = seq // tm
    return pl.pallas_call(
        _rglru_out_kernel,
        grid=(batch, ns),
        in_specs=[
            pl.BlockSpec((tm, c), lambda b, s: (s, 2 * b)),
            pl.BlockSpec((tm, c), lambda b, s: (s, b)),
            pl.BlockSpec((tm, c), lambda b, s: (s, b)),
            pl.BlockSpec((c, d), lambda b, s: (0, 0)),
            pl.BlockSpec((tm, d), lambda b, s: (b * ns + s, 0)),
        ],
        out_specs=pl.BlockSpec((tm, d), lambda b, s: (b * ns + s, 0)),
        out_shape=jax.ShapeDtypeStruct(x2.shape, F32),
        compiler_params=_params(("parallel", "parallel")),
        name="rglru_out_project",
    )(yx, hf, hb, w_out, x2)


def _swiglu_step(h, wg_ref, wu_ref, wd_ref, acc_scr):
    gate = jnp.dot(h, wg_ref[...].astype(BF16), preferred_element_type=F32)
    up = jnp.dot(h, wu_ref[...].astype(BF16), preferred_element_type=F32)
    act = gate * jax.nn.sigmoid(gate) * up
    acc_scr[...] += jnp.dot(act.astype(BF16), wd_ref[...].astype(BF16), preferred_element_type=F32)


def _ffn_kernel(x_ref, g_ref, wg_ref, wu_ref, wd_ref, o_ref, h_scr, acc_scr):
    f = pl.program_id(1)

    @pl.when(f == 0)
    def _():
        h_scr[...] = _rms(x_ref[...], g_ref[...]).astype(BF16)
        acc_scr[...] = jnp.zeros(acc_scr.shape, F32)

    _swiglu_step(h_scr[...], wg_ref, wu_ref, wd_ref, acc_scr)

    @pl.when(f == pl.num_programs(1) - 1)
    def _():
        o_ref[...] = x_ref[...] + acc_scr[...]


def swiglu_ffn(x2, gain, w_gate_up, w_down, *, tm, tf):
    t, d = x2.shape
    ff = w_down.shape[0]
    nf = ff // tf
    return pl.pallas_call(
        _ffn_kernel,
        grid=(t // tm, nf),
        in_specs=[
            pl.BlockSpec((tm, d), lambda i, f: (i, 0)),
            pl.BlockSpec((1, d), lambda i, f: (0, 0)),
            pl.BlockSpec((d, tf), lambda i, f: (0, f)),
            pl.BlockSpec((d, tf), lambda i, f: (0, nf + f)),
            pl.BlockSpec((tf, d), lambda i, f: (f, 0)),
        ],
        out_specs=pl.BlockSpec((tm, d), lambda i, f: (i, 0)),
        out_shape=jax.ShapeDtypeStruct((t, d), F32),
        scratch_shapes=[pltpu.VMEM((tm, d), BF16), pltpu.VMEM((tm, d), F32)],
        compiler_params=_params(("parallel", "arbitrary")),
        name="dense_ffn",
    )(x2, gain, w_gate_up, w_gate_up, w_down)


SUBLANES = 8


def _copy_token(src_hbm, dst, src_tok, dst_tok, sem):
    return pltpu.make_async_copy(src_hbm.at[pl.ds(src_tok * SUBLANES, SUBLANES)],
                                 dst.at[pl.ds(dst_tok * SUBLANES, SUBLANES)], sem)


def _moe_scatter_kernel(zero_ref, pos_ref, x_ref, o_hbm, stage, zeros_scr, sem, *, tm):
    p = pl.program_id(0)
    n = pl.num_programs(0)
    slot = p % 2
    chunks = x_ref.shape[1] // LANES
    block_rows = tm * SUBLANES

    def zero_copy(j):
        dst = o_hbm.at[pl.ds(zero_ref[j] * SUBLANES, zeros_scr.shape[0])]
        return pltpu.make_async_copy(zeros_scr, dst, sem.at[2])

    @pl.when(p == 0)
    def _():
        zeros_scr[...] = jnp.zeros(zeros_scr.shape, F32)
        for j in range(2 * N_EXPERTS):
            zero_copy(j).start()
            zero_copy(j).wait()

    def slot_wait(s):
        rows = pl.ds(0, 2 * block_rows)
        pltpu.make_async_copy(o_hbm.at[rows], o_hbm.at[rows], sem.at[s]).wait()

    @pl.when(p >= 2)
    def _():
        slot_wait(slot)

    for c in range(chunks):
        stage[slot, pl.ds(c, tm, stride=SUBLANES), :] = x_ref[:, c * LANES:(c + 1) * LANES]

    def issue(r, carry):
        src = stage.at[slot, pl.ds(r * SUBLANES, SUBLANES)]
        for k in range(TOP_K):
            dst = o_hbm.at[pl.ds(pos_ref[0, k * tm + r] * SUBLANES, SUBLANES)]
            pltpu.make_async_copy(src, dst, sem.at[slot]).start()
        return carry
    lax.fori_loop(0, tm, issue, 0, unroll=8)

    @pl.when(p == n - 1)
    def _():
        slot_wait(slot)

        @pl.when(n >= 2)
        def _():
            slot_wait(1 - slot)


def moe_scatter(x2, pos, zero_starts, n_tiles, *, tm, te):
    t, d = x2.shape
    grid_spec = pltpu.PrefetchScalarGridSpec(
        num_scalar_prefetch=1,
        grid=(t // tm,),
        in_specs=[
            pl.BlockSpec((None, 1, TOP_K * tm), lambda p, z: (p, 0, 0), memory_space=pltpu.SMEM),
            pl.BlockSpec((tm, d), lambda p, z: (p, 0)),
        ],
        out_specs=pl.BlockSpec(memory_space=pl.ANY),
        scratch_shapes=[pltpu.VMEM((2, tm * SUBLANES, LANES), F32), pltpu.VMEM((te * SUBLANES, LANES), F32),
                        pltpu.SemaphoreType.DMA((3,))],
    )
    return pl.pallas_call(
        functools.partial(_moe_scatter_kernel, tm=tm),
        grid_spec=grid_spec,
        out_shape=jax.ShapeDtypeStruct(((n_tiles + 1) * te * SUBLANES, LANES), F32),
        compiler_params=_params(("arbitrary",)),
        name="moe_scatter",
    )(zero_starts, pos, x2)


def _moe_expert_kernel(te_ref, meta_ref, xs_ref, g_ref, wg_ref, wu_ref, wd_ref, y_ref, h_scr, acc_scr):
    p = pl.program_id(0)
    f = pl.program_id(1)
    tm = acc_scr.shape[0]
    chunks = acc_scr.shape[1] // LANES
    used = p < meta_ref[0]

    @pl.when(used & (f == 0))
    def _():
        x = jnp.concatenate([xs_ref[pl.ds(c, tm, stride=SUBLANES), :] for c in range(chunks)], axis=1)
        h_scr[...] = _rms(x, g_ref[...]).astype(BF16)
        acc_scr[...] = jnp.zeros(acc_scr.shape, F32)

    @pl.when(used)
    def _():
        _swiglu_step(h_scr[...], wg_ref, wu_ref, wd_ref, acc_scr)

    @pl.when(f == pl.num_programs(1) - 1)
    def _():
        y = jnp.where(used, acc_scr[...], 0.0)
        for c in range(chunks):
            y_ref[pl.ds(c, tm, stride=SUBLANES), :] = y[:, c * LANES:(c + 1) * LANES]


def moe_experts(xs, gain, w_gate_up, w_down, tile_expert, meta, n_tiles, *, tm, tf):
    n_e, ff, d = w_down.shape
    nf = ff // tf

    def f_eff(p, f, meta_ref):
        return jnp.where(p < meta_ref[0], f, nf - 1)

    grid_spec = pltpu.PrefetchScalarGridSpec(
        num_scalar_prefetch=2,
        grid=(n_tiles, nf),
        in_specs=[
            pl.BlockSpec((tm * SUBLANES, LANES), lambda p, f, te, m: (p, 0)),
            pl.BlockSpec((1, d), lambda p, f, te, m: (0, 0)),
            pl.BlockSpec((None, d, tf), lambda p, f, te, m: (te[p], 0, f_eff(p, f, m))),
            pl.BlockSpec((None, d, tf), lambda p, f, te, m: (te[p], 0, nf + f_eff(p, f, m))),
            pl.BlockSpec((None, tf, d), lambda p, f, te, m: (te[p], f_eff(p, f, m), 0)),
        ],
        out_specs=pl.BlockSpec((tm * SUBLANES, LANES), lambda p, f, te, m: (p, 0)),
        scratch_shapes=[pltpu.VMEM((tm, d), BF16), pltpu.VMEM((tm, d), F32)],
    )
    return pl.pallas_call(
        _moe_expert_kernel,
        grid_spec=grid_spec,
        out_shape=jax.ShapeDtypeStruct((n_tiles * tm * SUBLANES, LANES), F32),
        compiler_params=_params(("parallel", "arbitrary")),
        name="moe_experts",
    )(tile_expert, meta, xs, gain, w_gate_up, w_gate_up, w_down)


def _moe_combine_kernel(cur_ref, nxt_ref, x_ref, route_ref, y_hbm, o_ref, buf, sem, *, tm):
    p = pl.program_id(0)
    n = pl.num_programs(0)
    chunks = x_ref.shape[1] // LANES

    def issue(idx_ref, slot):
        def body(r, carry):
            _copy_token(y_hbm, buf.at[slot], idx_ref[0, r], r, sem.at[slot]).start()
            return carry
        lax.fori_loop(0, 2 * tm, body, 0, unroll=8)

    @pl.when(p == 0)
    def _():
        issue(cur_ref, 0)

    @pl.when(p + 1 < n)
    def _():
        issue(nxt_ref, (p + 1) % 2)

    slot = p % 2
    pltpu.make_async_copy(y_hbm.at[pl.ds(0, 2 * tm * SUBLANES)], buf.at[slot], sem.at[slot]).wait()
    route = route_ref[...]
    g1 = route[:, 2:3]
    g2 = route[:, 3:4]
    for c in range(chunks):
        first = buf[slot, pl.ds(c, tm, stride=SUBLANES), :]
        second = buf[slot, pl.ds(tm * SUBLANES + c, tm, stride=SUBLANES), :]
        sl = slice(c * LANES, (c + 1) * LANES)
        o_ref[:, sl] = x_ref[:, sl] + (g1 * first + g2 * second)


def moe_combine(x2, y_tiles, pos, route, *, tm):
    t, d = x2.shape
    n = t // tm
    idx_spec = lambda shift: pl.BlockSpec((None, 1, 2 * tm), lambda p: (jnp.minimum(p + shift, n - 1), 0, 0),
                                          memory_space=pltpu.SMEM)
    return pl.pallas_call(
        functools.partial(_moe_combine_kernel, tm=tm),
        grid=(n,),
        in_specs=[idx_spec(0), idx_spec(1), pl.BlockSpec((tm, d), lambda p: (p, 0)),
                  pl.BlockSpec((tm, LANES), lambda p: (p, 0)), pl.BlockSpec(memory_space=pl.ANY)],
        out_specs=pl.BlockSpec((tm, d), lambda p: (p, 0)),
        out_shape=jax.ShapeDtypeStruct((t, d), F32),
        scratch_shapes=[pltpu.VMEM((2, 2 * tm * SUBLANES, LANES), F32), pltpu.SemaphoreType.DMA((2,))],
        compiler_params=_params(("arbitrary",)),
        name="moe_combine",
    )(pos, pos, x2, route, y_tiles)


def _moe_plan(route, te, tm):
    t = route.shape[0]
    experts = route[:, :TOP_K].astype(jnp.int32)
    onehot = (experts[:, :, None] == jnp.arange(N_EXPERTS)[None, None, :]).astype(jnp.int32).sum(axis=1)
    incl = jnp.cumsum(onehot, axis=0)
    rank = incl - onehot
    counts = incl[-1]
    tiles_per = (counts + te - 1) // te
    tile_end = jnp.cumsum(tiles_per)
    row_start = (tile_end - tiles_per) * te
    dest = row_start[experts] + jnp.take_along_axis(rank, experts, axis=1)
    n_tiles = -(-(TOP_K * t) // te) + N_EXPERTS
    tile_expert = jnp.minimum(jnp.searchsorted(tile_end, jnp.arange(n_tiles), side="right"),
                              N_EXPERTS - 1).astype(jnp.int32)
    n_used = tile_end[-1]
    tail = jnp.minimum((n_used + jnp.arange(N_EXPERTS)) * te, n_tiles * te)
    zero_starts = jnp.concatenate([row_start + counts, tail]).astype(jnp.int32)
    pos = jnp.concatenate([dest[:, k].reshape(t // tm, tm) for k in range(TOP_K)], axis=1)
    return (pos.reshape(t // tm, 1, TOP_K * tm).astype(jnp.int32), zero_starts, tile_expert,
            n_used.reshape(1).astype(jnp.int32), n_tiles)


def _router_kernel(x_ref, g_ref, r_ref, b_ref, c_ref):
    h = _rms(x_ref[...], g_ref[...])
    r = r_ref[...]
    h_hi = h.astype(BF16)
    h_lo = (h - h_hi.astype(F32)).astype(BF16)
    r_hi = r.astype(BF16)
    r_lo = (r - r_hi.astype(F32)).astype(BF16)
    both = jnp.dot(h_hi, jnp.concatenate([r_hi, r_lo], axis=1), preferred_element_type=F32)
    logits = both[:, :LANES] + both[:, LANES:] + jnp.dot(h_lo, r_hi, preferred_element_type=F32)
    logits = logits + b_ref[...]
    lane = lax.broadcasted_iota(jnp.int32, logits.shape, 1)
    logits = jnp.where(lane < N_EXPERTS, logits, NEG_INF)
    v1 = jnp.max(logits, axis=-1, keepdims=True)
    i1 = jnp.min(jnp.where(logits == v1, lane, LANES), axis=-1, keepdims=True)
    rest = jnp.where(lane == i1, NEG_INF, logits)
    v2 = jnp.max(rest, axis=-1, keepdims=True)
    i2 = jnp.min(jnp.where(rest == v2, lane, LANES), axis=-1, keepdims=True)
    e2 = jnp.exp(v2 - v1)
    g1 = 1.0 / (1.0 + e2)
    g2 = e2 / (1.0 + e2)
    c_ref[...] = jnp.where(lane == 0, i1.astype(F32), jnp.where(lane == 1, i2.astype(F32),
                           jnp.where(lane == 2, g1, jnp.where(lane == 3, g2, 0.0))))


def moe_router(x2, gain, router, router_b, *, tm):
    t, d = x2.shape
    r_pad = jnp.zeros((d, LANES), F32).at[:, :N_EXPERTS].set(router)
    b_pad = jnp.zeros((1, LANES), F32).at[0, :N_EXPERTS].set(router_b)
    return pl.pallas_call(
        _router_kernel,
        grid=(t // tm,),
        in_specs=[
            pl.BlockSpec((tm, d), lambda i: (i, 0)),
            pl.BlockSpec((1, d), lambda i: (0, 0)),
            pl.BlockSpec((d, LANES), lambda i: (0, 0)),
            pl.BlockSpec((1, LANES), lambda i: (0, 0)),
        ],
        out_specs=pl.BlockSpec((tm, LANES), lambda i: (i, 0)),
        out_shape=jax.ShapeDtypeStruct((t, LANES), F32),
        compiler_params=_params(("parallel",)),
        name="moe_router",
    )(x2, gain, r_pad, b_pad)


def _inv_freq(dim, theta):
    return theta ** (-jnp.arange(0, dim, 2, dtype=F32) / dim)


_Q = HEAD_DIM // 4
AXIAL_PERM = np.concatenate([np.arange(0, _Q), np.arange(2 * _Q, 3 * _Q), np.arange(_Q, 2 * _Q),
                             np.arange(3 * _Q, 4 * _Q)])
_R = ROT_DIMS // 2
_REST = (HEAD_DIM - ROT_DIMS) // 2
PARTIAL_PERM = np.concatenate([np.arange(0, _R), np.arange(ROT_DIMS, ROT_DIMS + _REST), np.arange(_R, ROT_DIMS),
                               np.arange(ROT_DIMS + _REST, HEAD_DIM)])


def _axial_tables(seq):
    t = jnp.arange(seq)
    rows = (t // GRID_W).astype(F32)
    cols = (t % GRID_W).astype(F32)
    freq = _inv_freq(HEAD_DIM // 2, AXIAL_THETA)
    ang_r = rows[:, None] * freq[None, :]
    ang_c = cols[:, None] * freq[None, :]
    ang = jnp.concatenate([ang_r, ang_c, ang_r, ang_c], axis=-1)
    sign = jnp.concatenate([-jnp.ones(ROPE_SHIFT, F32), jnp.ones(ROPE_SHIFT, F32)])
    return jnp.cos(ang), jnp.sin(ang) * sign[None, :]


def _partial_tables(seq):
    pos = jnp.arange(seq, dtype=F32)
    freq = _inv_freq(ROT_DIMS, ROPE_THETA)
    ang = pos[:, None] * freq[None, :]
    ones = jnp.ones((seq, _REST), F32)
    zeros = jnp.zeros((seq, _REST), F32)
    cos = jnp.concatenate([jnp.cos(ang), ones, jnp.cos(ang), ones], axis=-1)
    sin = jnp.concatenate([-jnp.sin(ang), zeros, jnp.sin(ang), zeros], axis=-1)
    return cos, sin


def _permute_qk_columns(w, head_is_qk, perm):
    idx = np.concatenate([h * HEAD_DIM + (perm if qk else np.arange(HEAD_DIM))
                          for h, qk in enumerate(head_is_qk)])
    return jnp.take(w, jnp.asarray(idx, jnp.int32), axis=1)


def _row(v):
    return v.reshape(1, -1).astype(F32)


def _rglru_layer(x2, batch, seq, norm, w_in, conv_w, conv_b, w_a, b_a, w_x, b_x, lam, w_out):
    c = w_out.shape[0]
    yx = norm_matmul_time_major(x2, _row(norm), w_in.astype(BF16), batch, seq,
                                tm=512, tn=2048, out_dtype=F32)
    yx3 = yx.reshape(seq, batch, 2 * c)
    hs = []
    for direction in range(2):
        hs.append(rglru_scan(
            yx3, conv_w, _row(conv_b), w_a[direction].astype(BF16), _row(b_a[direction]),
            w_x[direction].astype(BF16), _row(b_x[direction]), _row(lam[direction]),
            reverse=direction == 1, ts=128, tc=1024).reshape(seq, batch * c))
    return rglru_out_project(yx, hs[0], hs[1], w_out.astype(BF16), x2, batch, seq, tm=512)


def _gqa_layer(x2, batch, seq, norm, w_qkv, q_gain, k_gain, w_out):
    nq = GQA_HEADS * HEAD_DIM
    nkv = GQA_KV_HEADS * HEAD_DIM
    modes = [MODE_QUERY] * (nq // PAIR) + [MODE_KEY] * (nkv // PAIR) + [MODE_PLAIN] * (nkv // PAIR)
    cos, sin = _axial_tables(seq)
    is_qk = [True] * (GQA_HEADS + GQA_KV_HEADS) + [False] * GQA_KV_HEADS
    w16 = _permute_qk_columns(w_qkv, is_qk, AXIAL_PERM).astype(BF16)
    qkv = qkv_project(x2, _row(norm), w16, modes, cos, sin, _row(q_gain[AXIAL_PERM]), _row(k_gain[AXIAL_PERM]),
                      batch, seq, tm=512, scale=HEAD_DIM ** -0.5 * LOG2_E, dil=1)
    o = gqa_attention(qkv.reshape(batch, seq, nq + 2 * nkv), tq=256, tk=512, streams=2)
    return matmul_residual(o.reshape(batch * seq, nq), w_out.astype(BF16), x2, tm=512)


def _dilated_layer(x2, batch, seq, norm, w_qkv, q_gain, k_gain, w_out):
    hw = DIL_HEADS * HEAD_DIM
    modes = [MODE_QUERY] * (hw // PAIR) + [MODE_KEY] * (hw // PAIR) + [MODE_PLAIN] * (hw // PAIR)
    cos, sin = _partial_tables(seq)
    is_qk = ([True] * (2 * DIL_HEADS) + [False] * DIL_HEADS) * len(DIL_PATTERNS)
    w16 = _permute_qk_columns(w_qkv, is_qk, PARTIAL_PERM).astype(BF16)
    outs, lses = [], []
    for g, (window, dil) in enumerate(DIL_PATTERNS):
        span = window // (2 * dil)
        assert seq % (dil * span) == 0
        qkv = qkv_project(x2, _row(norm), w16[:, g * 3 * hw:(g + 1) * 3 * hw], modes, cos, sin,
                          _row(q_gain[PARTIAL_PERM]), _row(k_gain[PARTIAL_PERM]), batch, seq,
                          tm=512, scale=HEAD_DIM ** -0.5 * LOG2_E, dil=dil)
        o, lse = dilated_attention(qkv, span, tq=min(4 * span, seq // dil))
        outs.append(o)
        lses.append(lse)
    return dilated_merge_project(outs, lses, w_out.astype(BF16), x2, batch, seq, tm=512)


def _dense_ffn_layer(x2, norm, w_gate_up, w_down):
    return swiglu_ffn(x2, _row(norm), w_gate_up.astype(BF16), w_down.astype(BF16), tm=1024, tf=512)


def _moe_layer(x2, norm, router, router_b, w_gate_up, w_down):
    tm, te = 512, 1024
    route = moe_router(x2, _row(norm), router, router_b, tm=tm)
    pos, zero_starts, tile_expert, meta, n_tiles = _moe_plan(route, te, tm)
    xs = moe_scatter(x2, pos, zero_starts, n_tiles, tm=tm, te=te)
    ys = moe_experts(xs, _row(norm), w_gate_up, w_down, tile_expert, meta, n_tiles,
                     tm=te, tf=512)
    return moe_combine(x2, ys, pos, route, tm=tm)


def kernel(x, l0_attn_norm, l0_rg_w_in, l0_rg_conv_w, l0_rg_conv_b, l0_rg_w_a, l0_rg_b_a, l0_rg_w_x, l0_rg_b_x, l0_rg_lambda, l0_rg_w_out, l0_ffn_norm, l0_ffn_w_gate_up, l0_ffn_w_down, l1_attn_norm, l1_gqa_w_qkv, l1_gqa_q_norm, l1_gqa_k_norm, l1_gqa_w_out, l1_ffn_norm, l1_moe_router, l1_moe_router_b, l1_moe_w_gate_up, l1_moe_w_down, l2_attn_norm, l2_dil_w_qkv, l2_dil_q_norm, l2_dil_k_norm, l2_dil_w_out, l2_ffn_norm, l2_ffn_w_gate_up, l2_ffn_w_down, l3_attn_norm, l3_rg_w_in, l3_rg_conv_w, l3_rg_conv_b, l3_rg_w_a, l3_rg_b_a, l3_rg_w_x, l3_rg_b_x, l3_rg_lambda, l3_rg_w_out, l3_ffn_norm, l3_moe_router, l3_moe_router_b, l3_moe_w_gate_up, l3_moe_w_down):
    batch, seq, d = x.shape
    x2 = x.reshape(batch * seq, d)
    x2 = _rglru_layer(x2, batch, seq, l0_attn_norm, l0_rg_w_in, l0_rg_conv_w, l0_rg_conv_b, l0_rg_w_a,
                      l0_rg_b_a, l0_rg_w_x, l0_rg_b_x, l0_rg_lambda, l0_rg_w_out)
    x2 = _dense_ffn_layer(x2, l0_ffn_norm, l0_ffn_w_gate_up, l0_ffn_w_down)
    x2 = _gqa_layer(x2, batch, seq, l1_attn_norm, l1_gqa_w_qkv, l1_gqa_q_norm, l1_gqa_k_norm, l1_gqa_w_out)
    x2 = _moe_layer(x2, l1_ffn_norm, l1_moe_router, l1_moe_router_b, l1_moe_w_gate_up, l1_moe_w_down)
    x2 = _dilated_layer(x2, batch, seq, l2_attn_norm, l2_dil_w_qkv, l2_dil_q_norm, l2_dil_k_norm, l2_dil_w_out)
    x2 = _dense_ffn_layer(x2, l2_ffn_norm, l2_ffn_w_gate_up, l2_ffn_w_down)
    x2 = _rglru_layer(x2, batch, seq, l3_attn_norm, l3_rg_w_in, l3_rg_conv_w, l3_rg_conv_b, l3_rg_w_a,
                      l3_rg_b_a, l3_rg_w_x, l3_rg_b_x, l3_rg_lambda, l3_rg_w_out)
    x2 = _moe_layer(x2, l3_ffn_norm, l3_moe_router, l3_moe_router_b, l3_moe_w_gate_up, l3_moe_w_down)
    return x2.reshape(batch, seq, d)
```

```python
import functools

import jax
import jax.numpy as jnp
import numpy as np
from jax import lax
from jax.experimental import pallas as pl
from jax.experimental.pallas import tpu as pltpu

F32 = jnp.float32
BF16 = jnp.bfloat16

EPS = 1e-6
NEG_INF = -1e30
GRID_W = 64
HEAD_DIM = 128
LANES = 128
RG_BLOCK_W = 128
CONV_W = 4
CONV_LEFT = 2
RG_C = 8.0
GQA_HEADS = 8
GQA_KV_HEADS = 2
GQA_GROUP = GQA_HEADS // GQA_KV_HEADS
AXIAL_THETA = 10000.0
DIL_PATTERNS = ((128, 1), (512, 4), (2048, 16))
DIL_HEADS = 8
ROPE_THETA = 500000.0
ROT_DIMS = HEAD_DIM // 4
N_EXPERTS = 8
TOP_K = 2
VMEM_LIMIT = 56 * 1024 * 1024

NT_DIMS = (((1,), (1,)), ((), ()))
LOG2_E = 1.4426950408889634


def _params(sem, vmem=VMEM_LIMIT):
    return pltpu.CompilerParams(dimension_semantics=sem, vmem_limit_bytes=vmem)


def _rms(x, g):
    return x * lax.rsqrt(jnp.mean(x * x, axis=-1, keepdims=True) + EPS) * g


def _norm_matmul_kernel(x_ref, g_ref, w_ref, o_ref, h_scr):
    @pl.when(pl.program_id(2) == 0)
    def _():
        h_scr[...] = _rms(x_ref[...], g_ref[...]).astype(BF16)

    o_ref[...] = jnp.dot(h_scr[...], w_ref[...], preferred_element_type=F32).astype(o_ref.dtype)


def norm_matmul_time_major(x2, gain, w, batch, seq, *, tm, tn, out_dtype):
    d = x2.shape[1]
    n = w.shape[1]
    ns, nj = seq // tm, n // tn
    return pl.pallas_call(
        _norm_matmul_kernel,
        grid=(batch, ns, nj),
        in_specs=[
            pl.BlockSpec((tm, d), lambda b, s, j: (b * ns + s, 0)),
            pl.BlockSpec((1, d), lambda b, s, j: (0, 0)),
            pl.BlockSpec((d, tn), lambda b, s, j: (0, j)),
        ],
        out_specs=pl.BlockSpec((tm, tn), lambda b, s, j: (s, b * nj + j)),
        out_shape=jax.ShapeDtypeStruct((seq, batch * n), out_dtype),
        scratch_shapes=[pltpu.VMEM((tm, d), BF16)],
        compiler_params=_params(("parallel", "parallel", "arbitrary")),
        name="norm_matmul_tm",
    )(x2, gain, w)


ROPE_SHIFT = LANES // 2


PAIR = 2 * LANES
MODE_PLAIN, MODE_QUERY, MODE_KEY = 0, 1, 2


def _qkv_kernel(x_ref, g_ref, w_ref, cos_ref, sin_ref, qg_ref, kg_ref, o_ref, h_scr, *stage,
                scale, dil, pair_modes):
    tm = x_ref.shape[0]
    per = tm // dil
    hn = _rms(x_ref[...], g_ref[...])
    if dil == 1:
        h_scr[...] = hn.astype(BF16)
    else:
        stage_scr, = stage
        for c in range(hn.shape[1] // LANES):
            sl = slice(c * LANES, (c + 1) * LANES)
            stage_scr[c] = hn[:, sl]
            for r in range(dil):
                h_scr[r * per:(r + 1) * per, sl] = stage_scr[c, pl.ds(r, per, stride=dil), :].astype(BF16)
    h = h_scr[...]

    row = lax.broadcasted_iota(jnp.int32, (PAIR, PAIR), 0)
    col = lax.broadcasted_iota(jnp.int32, (PAIR, PAIR), 1)
    head_ones = (row // LANES == col // LANES).astype(BF16)
    cos = cos_ref[...]
    sin = sin_ref[...]
    tables = {}
    for mode, gain, sc in ((MODE_QUERY, qg_ref[...], scale), (MODE_KEY, kg_ref[...], 1.0)):
        if mode in pair_modes:
            tables[mode] = (cos * (gain * sc), sin * (pltpu.roll(gain, ROPE_SHIFT, 1) * sc))

    accs = [jnp.dot(h, w_ref[:, i * PAIR:(i + 1) * PAIR], preferred_element_type=F32)
            for i in range(len(pair_modes))]
    sumsqs = [None if mode == MODE_PLAIN else
              jnp.dot((a * a).astype(BF16), head_ones, preferred_element_type=F32)
              for a, mode in zip(accs, pair_modes)]
    for i, (a, sumsq, mode) in enumerate(zip(accs, sumsqs, pair_modes)):
        for c in range(PAIR // LANES):
            sl = slice(c * LANES, (c + 1) * LANES)
            head = a[:, sl]
            if mode != MODE_PLAIN:
                c1, s1 = tables[mode]
                inv = lax.rsqrt(sumsq[:, sl] * (1.0 / HEAD_DIM) + EPS)
                head = (head * c1 + pltpu.roll(head, ROPE_SHIFT, 1) * s1) * inv
            out_sl = slice(i * PAIR + c * LANES, i * PAIR + (c + 1) * LANES)
            o_ref[:, :, out_sl] = head.reshape(dil, per, LANES).astype(o_ref.dtype)


def _tile_residue_major(table, dil, tm):
    s, w = table.shape
    return table.reshape(s // tm, tm // dil, dil, w).transpose(0, 2, 1, 3).reshape(s, w)


def qkv_project(x2, gain, w, pair_modes, cos, sin, q_gain, k_gain, batch, seq, *, tm, scale, dil):
    t, d = x2.shape
    n = w.shape[1]
    assert n == len(pair_modes) * PAIR
    ns = seq // tm
    per = tm // dil
    cos_r = _tile_residue_major(cos, dil, tm)
    sin_r = _tile_residue_major(sin, dil, tm)
    table_spec = pl.BlockSpec((tm, LANES), lambda i: (i % ns, 0))
    vec_spec = pl.BlockSpec((1, LANES), lambda i: (0, 0))
    out = pl.pallas_call(
        functools.partial(_qkv_kernel, scale=scale, dil=dil, pair_modes=tuple(pair_modes)),
        grid=(t // tm,),
        in_specs=[
            pl.BlockSpec((tm, d), lambda i: (i, 0)),
            pl.BlockSpec((1, d), lambda i: (0, 0)),
            pl.BlockSpec((d, n), lambda i: (0, 0)),
            table_spec, table_spec, vec_spec, vec_spec,
        ],
        out_specs=pl.BlockSpec((None, dil, None, per, n), lambda i: (i // ns, 0, i % ns, 0, 0)),
        out_shape=jax.ShapeDtypeStruct((batch, dil, ns, per, n), BF16),
        scratch_shapes=[pltpu.VMEM((tm, d), BF16)]
        + ([pltpu.VMEM((d // LANES, tm, LANES), F32)] if dil > 1 else []),
        compiler_params=_params(("parallel",)),
        name=f"qkv_project_d{dil}",
    )(x2, gain, w, cos_r, sin_r, q_gain, k_gain)
    return out.reshape(batch, dil, seq // dil, n)


def _matmul_residual_kernel(a_ref, w_ref, x_ref, o_ref):
    o_ref[...] = x_ref[...] + jnp.dot(a_ref[...], w_ref[...], preferred_element_type=F32)


def matmul_residual(a, w, x2, *, tm):
    t, k = a.shape
    n = w.shape[1]
    return pl.pallas_call(
        _matmul_residual_kernel,
        grid=(t // tm,),
        in_specs=[
            pl.BlockSpec((tm, k), lambda i: (i, 0)),
            pl.BlockSpec((k, n), lambda i: (0, 0)),
            pl.BlockSpec((tm, n), lambda i: (i, 0)),
        ],
        out_specs=pl.BlockSpec((tm, n), lambda i: (i, 0)),
        out_shape=jax.ShapeDtypeStruct((t, n), F32),
        compiler_params=_params(("parallel",)),
        name="matmul_residual",
    )(a, w, x2)


def _gqa_attn_kernel(q_ref, k_ref, v_ref, o_ref, *, tk, streams):
    tq = q_ref.shape[1]
    seq = k_ref.shape[1]
    per = GQA_GROUP // streams
    qs = [jnp.concatenate([q_ref[0, :, g * LANES:(g + 1) * LANES] for g in range(st * per, (st + 1) * per)],
                          axis=0) for st in range(streams)]
    rows = per * tq
    ones = jnp.ones((tk, LANES), BF16)

    def body(c, carry):
        start = pl.multiple_of(c * tk, tk)
        k = k_ref[0, pl.ds(start, tk), :]
        v1 = jnp.concatenate([v_ref[0, pl.ds(start, tk), :], ones], axis=1)
        new = []
        for st in range(streams):
            m_prev, acc = carry[st]
            s = lax.dot_general(qs[st], k, NT_DIMS, preferred_element_type=F32)
            m_new = jnp.maximum(m_prev, jnp.max(s, axis=-1, keepdims=True))
            alpha = jnp.exp2(m_prev - m_new)
            p = jnp.exp2(s - m_new).astype(BF16)
            acc = alpha * acc + jnp.dot(p, v1, preferred_element_type=F32)
            new.append((m_new, acc))
        return tuple(new)

    init = tuple((jnp.full((rows, 1), NEG_INF, F32), jnp.zeros((rows, 2 * LANES), F32)) for _ in range(streams))
    final = lax.fori_loop(0, seq // tk, body, init, unroll=16)
    for st in range(streams):
        acc = final[st][1]
        o = acc[:, :LANES] / acc[:, LANES:]
        for gi in range(per):
            g = st * per + gi
            o_ref[0, :, g * LANES:(g + 1) * LANES] = o[gi * tq:(gi + 1) * tq].astype(o_ref.dtype)


def gqa_attention(qkv, *, tq, tk, streams):
    b, s, _ = qkv.shape
    gw = GQA_GROUP * HEAD_DIM
    k_off = GQA_HEADS
    v_off = GQA_HEADS + GQA_KV_HEADS
    return pl.pallas_call(
        functools.partial(_gqa_attn_kernel, tk=tk, streams=streams),
        grid=(b, GQA_KV_HEADS, s // tq),
        in_specs=[
            pl.BlockSpec((1, tq, gw), lambda bb, n, i: (bb, i, n)),
            pl.BlockSpec((1, s, HEAD_DIM), lambda bb, n, i: (bb, 0, k_off + n)),
            pl.BlockSpec((1, s, HEAD_DIM), lambda bb, n, i: (bb, 0, v_off + n)),
        ],
        out_specs=pl.BlockSpec((1, tq, gw), lambda bb, n, i: (bb, i, n)),
        out_shape=jax.ShapeDtypeStruct((b, s, GQA_HEADS * HEAD_DIM), BF16),
        compiler_params=_params(("parallel", "parallel", "arbitrary")),
        name="gqa_attention",
    )(qkv, qkv, qkv)


def _dil_attn_kernel(q_ref, kp_ref, kc_ref, kn_ref, vp_ref, vc_ref, vn_ref, o_ref, lse_ref,
                     *, span, m_total):
    i = pl.program_id(2)
    tq = q_ref.shape[0]
    qb = 2 * span
    nk = qb + 2 * span
    row = lax.broadcasted_iota(jnp.int32, (qb, nk), 0)
    col = lax.broadcasted_iota(jnp.int32, (qb, nk), 1)
    in_band = jnp.abs(col - span - row) <= span
    lane = lax.broadcasted_iota(jnp.int32, (qb, LANES), 1)
    ones = jnp.ones((nk, LANES), BF16)
    for sub in range(tq // qb):
        key_idx = i * tq + sub * qb - span + col
        ok = in_band & (key_idx >= 0) & (key_idx < m_total)
        lse_tile = jnp.zeros((qb, LANES), F32)
        rows = slice(sub * qb, (sub + 1) * qb)
        win = slice(sub * qb, sub * qb + nk)
        heads = [slice(h * LANES, (h + 1) * LANES) for h in range(DIL_HEADS)]
        scores = [lax.dot_general(q_ref[rows, sl],
                                  jnp.concatenate([kp_ref[:, sl], kc_ref[:, sl], kn_ref[:, sl]], axis=0)[win],
                                  NT_DIMS, preferred_element_type=F32) for sl in heads]
        maxes, probs = [], []
        for s in scores:
            s = jnp.where(ok, s, NEG_INF)
            m = jnp.max(s, axis=-1, keepdims=True)
            maxes.append(m)
            probs.append(jnp.exp2(s - m).astype(BF16))
        accs = [jnp.dot(p, jnp.concatenate(
                    [jnp.concatenate([vp_ref[:, sl], vc_ref[:, sl], vn_ref[:, sl]], axis=0)[win], ones], axis=1),
                    preferred_element_type=F32) for p, sl in zip(probs, heads)]
        for h, (acc, m, sl) in enumerate(zip(accs, maxes, heads)):
            o_ref[rows, sl] = (acc[:, :LANES] / acc[:, LANES:]).astype(o_ref.dtype)
            lse_tile = jnp.where(lane == h, m + jnp.log2(acc[:, LANES:LANES + 1]), lse_tile)
        lse_ref[rows, :] = lse_tile


def dilated_attention(qkv, span, *, tq):
    b, dil, m_total, n = qkv.shape
    hw = DIL_HEADS * HEAD_DIM
    nb = m_total // tq
    halo_per_tile = tq // span
    n_halo = m_total // span

    def tile(col):
        return pl.BlockSpec((None, None, tq, hw), lambda bb, r, i: (bb, r, i, col))

    def before(col):
        return pl.BlockSpec((None, None, span, hw),
                            lambda bb, r, i: (bb, r, jnp.maximum(i * halo_per_tile - 1, 0), col))

    def after(col):
        return pl.BlockSpec((None, None, span, hw),
                            lambda bb, r, i: (bb, r, jnp.minimum((i + 1) * halo_per_tile, n_halo - 1), col))

    return pl.pallas_call(
        functools.partial(_dil_attn_kernel, span=span, m_total=m_total),
        grid=(b, dil, nb),
        in_specs=[tile(0), before(1), tile(1), after(1), before(2), tile(2), after(2)],
        out_specs=[
            pl.BlockSpec((None, None, tq, hw), lambda bb, r, i: (bb, r, i, 0)),
            pl.BlockSpec((None, None, tq, LANES), lambda bb, r, i: (bb, r, i, 0)),
        ],
        out_shape=[
            jax.ShapeDtypeStruct((b, dil, m_total, hw), BF16),
            jax.ShapeDtypeStruct((b, dil, m_total, LANES), F32),
        ],
        compiler_params=_params(("parallel", "parallel", "arbitrary")),
        name=f"dilated_attention_d{dil}",
    )(qkv, qkv, qkv, qkv, qkv, qkv, qkv)


def _dil_merge_kernel(o0_ref, o1_ref, o2_ref, l0_ref, l1_ref, l2_ref, w_ref, x_ref, out_ref, o_scr, l_scr):
    def to_tokens(ref, scr, g):
        dil, per, width = ref.shape
        for c in range(width // LANES):
            sl = slice(c * LANES, (c + 1) * LANES)
            if dil == 1:
                scr[g, c] = ref[0, :, sl].astype(F32)
            else:
                for r in range(dil):
                    scr[g, c, pl.ds(r, per, stride=dil), :] = ref[r, :, sl].astype(F32)

    for g, (o_ref, l_ref) in enumerate(((o0_ref, l0_ref), (o1_ref, l1_ref), (o2_ref, l2_ref))):
        to_tokens(o_ref, o_scr, g)
        to_tokens(l_ref, l_scr, g)
    lses = [l_scr[g, 0] for g in range(3)]
    mx = jnp.maximum(jnp.maximum(lses[0], lses[1]), lses[2])
    es = [jnp.exp2(l - mx) for l in lses]
    den = es[0] + es[1] + es[2]
    ws = [e / den for e in es]
    parts = []
    for h in range(DIL_HEADS):
        acc = None
        for g in range(3):
            term = o_scr[g, h] * ws[g][:, h:h + 1]
            acc = term if acc is None else acc + term
        parts.append(acc.astype(BF16))
    merged = jnp.concatenate(parts, axis=1)
    out_ref[...] = x_ref[...] + jnp.dot(merged, w_ref[...], preferred_element_type=F32)


def dilated_merge_project(outs, lses, w_out, x2, batch, seq, *, tm):
    t, d = x2.shape
    hw = w_out.shape[0]
    ns = seq // tm

    def grouped(arr):
        _, dil, m_total, w = arr.shape
        per = tm // dil
        view = arr.reshape(batch, dil, ns, per, w)
        return view, pl.BlockSpec((None, dil, None, per, w), lambda i: (i // ns, 0, i % ns, 0, 0))

    views, specs = zip(*[grouped(a) for a in (*outs, *lses)])
    row = pl.BlockSpec((tm, d), lambda i: (i, 0))
    return pl.pallas_call(
        _dil_merge_kernel,
        grid=(t // tm,),
        in_specs=list(specs) + [pl.BlockSpec((hw, d), lambda i: (0, 0)), row],
        out_specs=row,
        out_shape=jax.ShapeDtypeStruct((t, d), F32),
        scratch_shapes=[pltpu.VMEM((3, hw // LANES, tm, LANES), F32), pltpu.VMEM((3, 1, tm, LANES), F32)],
        compiler_params=_params(("parallel",)),
        name="dilated_merge_project",
    )(*views, w_out, x2)


def _rglru_scan_kernel(xc_ref, xp_ref, xn_ref, cw_ref, cb_ref, wa_ref, ba_ref, wx_ref, bx_ref, lam_ref,
                       o_ref, a_scr, u_scr, h_scr, o_scr, *, reverse, n_steps):
    s = pl.program_id(1)
    ts, nb, tc = xc_ref.shape
    chunk = (n_steps - 1 - s) if reverse else s

    @pl.when(s == 0)
    def _():
        h_scr[...] = jnp.zeros(h_scr.shape, F32)

    prev = jnp.where(chunk == 0, 0.0, xp_ref[...])
    nxt = jnp.where(chunk == n_steps - 1, 0.0, xn_ref[...])
    xe = jnp.concatenate([prev, xc_ref[...], nxt], axis=0)
    cw = cw_ref[...]
    xc = cb_ref[...].reshape(1, 1, tc) + xe[0:ts] * cw[0:1].reshape(1, 1, tc)
    for j in range(1, CONV_W):
        xc = xc + xe[j:j + ts] * cw[j:j + 1].reshape(1, 1, tc)
    xc2 = xc.reshape(ts * nb, tc)
    lam = lam_ref[...]
    softplus_neg_lam = jnp.maximum(-lam, 0.0) + jnp.log1p(jnp.exp(-jnp.abs(lam)))
    decay_log2 = (-RG_C * LOG2_E) * softplus_neg_lam

    def sigmoid(x16, w_ref, bias):
        half_z = jnp.dot(x16, w_ref[kb] * 0.5, preferred_element_type=F32) + 0.5 * bias
        return 0.5 * jnp.tanh(half_z) + 0.5

    for kb in range(tc // RG_BLOCK_W):
        sl = slice(kb * RG_BLOCK_W, (kb + 1) * RG_BLOCK_W)
        xb = xc2[:, sl]
        xb16 = xb.astype(BF16)
        r = sigmoid(xb16, wa_ref, ba_ref[:, sl])
        gate = sigmoid(xb16, wx_ref, bx_ref[:, sl])
        a = jnp.exp2(r * decay_log2[:, sl])
        z = 1.0 - a * a
        u = (z * lax.rsqrt(jnp.maximum(z, 1e-30))) * (gate * xb)
        a_scr[:, :, sl] = a.reshape(ts, nb, RG_BLOCK_W)
        u_scr[:, :, sl] = u.reshape(ts, nb, RG_BLOCK_W)

    def step(t, h):
        tt = (ts - 1 - t) if reverse else t
        h = a_scr[tt] * h + u_scr[tt]
        row = pl.multiple_of(tt * nb, nb)
        for kb in range(tc // LANES):
            o_scr[kb, pl.ds(row, nb), :] = h[:, kb * LANES:(kb + 1) * LANES]
        return h

    h_scr[...] = lax.fori_loop(0, ts, step, h_scr[...], unroll=8)
    for b in range(nb):
        for kb in range(tc // LANES):
            o_ref[:, b * tc + kb * LANES:b * tc + (kb + 1) * LANES] = o_scr[kb, pl.ds(b, ts, stride=nb), :]


def rglru_scan(yx3, conv_w, conv_b, w_a, b_a, w_x, b_x, lam, *, reverse, ts, tc):
    seq, nb, two_c = yx3.shape
    c = two_c // 2
    assert tc == c
    nc, n_steps = c // tc, seq // ts
    kb = tc // RG_BLOCK_W

    def chunk_of(s):
        return (n_steps - 1 - s) if reverse else s

    vec = lambda rows: pl.BlockSpec((rows, tc), lambda ci, s: (0, ci))
    mat = pl.BlockSpec((kb, RG_BLOCK_W, RG_BLOCK_W), lambda ci, s: (ci, 0, 0))
    return pl.pallas_call(
        functools.partial(_rglru_scan_kernel, reverse=reverse, n_steps=n_steps),
        grid=(nc, n_steps),
        in_specs=[
            pl.BlockSpec((ts, nb, tc), lambda ci, s: (chunk_of(s), 0, nc + ci)),
            pl.BlockSpec((CONV_LEFT, nb, tc),
                         lambda ci, s: (jnp.maximum(chunk_of(s) * (ts // CONV_LEFT) - 1, 0), 0, nc + ci)),
            pl.BlockSpec((1, nb, tc),
                         lambda ci, s: (jnp.minimum((chunk_of(s) + 1) * ts, seq - 1), 0, nc + ci)),
            vec(CONV_W), vec(1), mat, vec(1), mat, vec(1), vec(1),
        ],
        out_specs=pl.BlockSpec((ts, nb * tc), lambda ci, s: (chunk_of(s), 0)),
        out_shape=jax.ShapeDtypeStruct((seq, nb * c), F32),
        scratch_shapes=[
            pltpu.VMEM((ts, nb, tc), F32),
            pltpu.VMEM((ts, nb, tc), F32),
            pltpu.VMEM((nb, tc), F32),
            pltpu.VMEM((tc // LANES, ts * nb, LANES), F32),
        ],
        compiler_params=_params(("parallel", "arbitrary")),
        name="rglru_scan_bwd" if reverse else "rglru_scan_fwd",
    )(yx3, yx3, yx3, conv_w, conv_b, w_a, b_a, w_x, b_x, lam)


def _gelu_tanh(y):
    return y * (0.5 * (1.0 + jnp.tanh(0.7978845608028654 * (y + 0.044715 * (y * y * y)))))


def _rglru_out_kernel(y_ref, hf_ref, hb_ref, w_ref, x_ref, o_ref):
    gated = _gelu_tanh(y_ref[...]) * (hf_ref[...] + hb_ref[...])
    o_ref[...] = x_ref[...] + jnp.dot(gated.astype(BF16), w_ref[...], preferred_element_type=F32)


def rglru_out_project(yx, hf, hb, w_out, x2, batch, seq, *, tm):
    c, d = w_out.shape
    ns = seq // tm
    return pl.pallas_call(
        _rglru_out_kernel,
        grid=(batch, ns),
        in_specs=[
            pl.BlockSpec((tm, c), lambda b, s: (s, 2 * b)),
            pl.BlockSpec((tm, c), lambda b, s: (s, b)),
            pl.BlockSpec((tm, c), lambda b, s: (s, b)),
            pl.BlockSpec((c, d), lambda b, s: (0, 0)),
            pl.BlockSpec((tm, d), lambda b, s: (b * ns + s, 0)),
        ],
        out_specs=pl.BlockSpec((tm, d), lambda b, s: (b * ns + s, 0)),
        out_shape=jax.ShapeDtypeStruct(x2.shape, F32),
        compiler_params=_params(("parallel", "parallel")),
        name="rglru_out_project",
    )(yx, hf, hb, w_out, x2)


def _swiglu_step(h, wg_ref, wu_ref, wd_ref, acc_scr):
    gate = jnp.dot(h, wg_ref[...].astype(BF16), preferred_element_type=F32)
    up = jnp.dot(h, wu_ref[...].astype(BF16), preferred_element_type=F32)
    act = gate * jax.nn.sigmoid(gate) * up
    acc_scr[...] += jnp.dot(act.astype(BF16), wd_ref[...].astype(BF16), preferred_element_type=F32)


def _ffn_kernel(x_ref, g_ref, wg_ref, wu_ref, wd_ref, o_ref, h_scr, acc_scr):
    f = pl.program_id(1)

    @pl.when(f == 0)
    def _():
        h_scr[...] = _rms(x_ref[...], g_ref[...]).astype(BF16)
        acc_scr[...] = jnp.zeros(acc_scr.shape, F32)

    _swiglu_step(h_scr[...], wg_ref, wu_ref, wd_ref, acc_scr)

    @pl.when(f == pl.num_programs(1) - 1)
    def _():
        o_ref[...] = x_ref[...] + acc_scr[...]


def swiglu_ffn(x2, gain, w_gate_up, w_down, *, tm, tf):
    t, d = x2.shape
    ff = w_down.shape[0]
    nf = ff // tf
    return pl.pallas_call(
        _ffn_kernel,
        grid=(t // tm, nf),
        in_specs=[
            pl.BlockSpec((tm, d), lambda i, f: (i, 0)),
            pl.BlockSpec((1, d), lambda i, f: (0, 0)),
            pl.BlockSpec((d, tf), lambda i, f: (0, f)),
            pl.BlockSpec((d, tf), lambda i, f: (0, nf + f)),
            pl.BlockSpec((tf, d), lambda i, f: (f, 0)),
        ],
        out_specs=pl.BlockSpec((tm, d), lambda i, f: (i, 0)),
        out_shape=jax.ShapeDtypeStruct((t, d), F32),
        scratch_shapes=[pltpu.VMEM((tm, d), BF16), pltpu.VMEM((tm, d), F32)],
        compiler_params=_params(("parallel", "arbitrary")),
        name="dense_ffn",
    )(x2, gain, w_gate_up, w_gate_up, w_down)


SUBLANES = 8


def _copy_token(src_hbm, dst, src_tok, dst_tok, sem):
    return pltpu.make_async_copy(src_hbm.at[pl.ds(src_tok * SUBLANES, SUBLANES)],
                                 dst.at[pl.ds(dst_tok * SUBLANES, SUBLANES)], sem)


def _moe_scatter_kernel(zero_ref, pos_ref, x_ref, o_hbm, stage, zeros_scr, sem, *, tm):
    p = pl.program_id(0)
    n = pl.num_programs(0)
    slot = p % 2
    chunks = x_ref.shape[1] // LANES
    block_rows = tm * SUBLANES

    def zero_copy(j):
        dst = o_hbm.at[pl.ds(zero_ref[j] * SUBLANES, zeros_scr.shape[0])]
        return pltpu.make_async_copy(zeros_scr, dst, sem.at[2])

    @pl.when(p == 0)
    def _():
        zeros_scr[...] = jnp.zeros(zeros_scr.shape, F32)
        for j in range(2 * N_EXPERTS):
            zero_copy(j).start()
            zero_copy(j).wait()

    def slot_wait(s):
        rows = pl.ds(0, 2 * block_rows)
        pltpu.make_async_copy(o_hbm.at[rows], o_hbm.at[rows], sem.at[s]).wait()

    @pl.when(p >= 2)
    def _():
        slot_wait(slot)

    for c in range(chunks):
        stage[slot, pl.ds(c, tm, stride=SUBLANES), :] = x_ref[:, c * LANES:(c + 1) * LANES]

    def issue(r, carry):
        src = stage.at[slot, pl.ds(r * SUBLANES, SUBLANES)]
        for k in range(TOP_K):
            dst = o_hbm.at[pl.ds(pos_ref[0, k * tm + r] * SUBLANES, SUBLANES)]
            pltpu.make_async_copy(src, dst, sem.at[slot]).start()
        return carry
    lax.fori_loop(0, tm, issue, 0, unroll=8)

    @pl.when(p == n - 1)
    def _():
        slot_wait(slot)

        @pl.when(n >= 2)
        def _():
            slot_wait(1 - slot)


def moe_scatter(x2, pos, zero_starts, n_tiles, *, tm, te):
    t, d = x2.shape
    grid_spec = pltpu.PrefetchScalarGridSpec(
        num_scalar_prefetch=1,
        grid=(t // tm,),
        in_specs=[
            pl.BlockSpec((None, 1, TOP_K * tm), lambda p, z: (p, 0, 0), memory_space=pltpu.SMEM),
            pl.BlockSpec((tm, d), lambda p, z: (p, 0)),
        ],
        out_specs=pl.BlockSpec(memory_space=pl.ANY),
        scratch_shapes=[pltpu.VMEM((2, tm * SUBLANES, LANES), F32), pltpu.VMEM((te * SUBLANES, LANES), F32),
                        pltpu.SemaphoreType.DMA((3,))],
    )
    return pl.pallas_call(
        functools.partial(_moe_scatter_kernel, tm=tm),
        grid_spec=grid_spec,
        out_shape=jax.ShapeDtypeStruct(((n_tiles + 1) * te * SUBLANES, LANES), F32),
        compiler_params=_params(("arbitrary",)),
        name="moe_scatter",
    )(zero_starts, pos, x2)


def _moe_expert_kernel(te_ref, meta_ref, xs_ref, g_ref, wg_ref, wu_ref, wd_ref, y_ref, h_scr, acc_scr):
    p = pl.program_id(0)
    f = pl.program_id(1)
    tm = acc_scr.shape[0]
    chunks = acc_scr.shape[1] // LANES
    used = p < meta_ref[0]

    @pl.when(used & (f == 0))
    def _():
        x = jnp.concatenate([xs_ref[pl.ds(c, tm, stride=SUBLANES), :] for c in range(chunks)], axis=1)
        h_scr[...] = _rms(x, g_ref[...]).astype(BF16)
        acc_scr[...] = jnp.zeros(acc_scr.shape, F32)

    @pl.when(used)
    def _():
        _swiglu_step(h_scr[...], wg_ref, wu_ref, wd_ref, acc_scr)

    @pl.when(f == pl.num_programs(1) - 1)
    def _():
        y = jnp.where(used, acc_scr[...], 0.0)
        for c in range(chunks):
            y_ref[pl.ds(c, tm, stride=SUBLANES), :] = y[:, c * LANES:(c + 1) * LANES]


def moe_experts(xs, gain, w_gate_up, w_down, tile_expert, meta, n_tiles, *, tm, tf):
    n_e, ff, d = w_down.shape
    nf = ff // tf

    def f_eff(p, f, meta_ref):
        return jnp.where(p < meta_ref[0], f, nf - 1)

    grid_spec = pltpu.PrefetchScalarGridSpec(
        num_scalar_prefetch=2,
        grid=(n_tiles, nf),
        in_specs=[
            pl.BlockSpec((tm * SUBLANES, LANES), lambda p, f, te, m: (p, 0)),
            pl.BlockSpec((1, d), lambda p, f, te, m: (0, 0)),
            pl.BlockSpec((None, d, tf), lambda p, f, te, m: (te[p], 0, f_eff(p, f, m))),
            pl.BlockSpec((None, d, tf), lambda p, f, te, m: (te[p], 0, nf + f_eff(p, f, m))),
            pl.BlockSpec((None, tf, d), lambda p, f, te, m: (te[p], f_eff(p, f, m), 0)),
        ],
        out_specs=pl.BlockSpec((tm * SUBLANES, LANES), lambda p, f, te, m: (p, 0)),
        scratch_shapes=[pltpu.VMEM((tm, d), BF16), pltpu.VMEM((tm, d), F32)],
    )
    return pl.pallas_call(
        _moe_expert_kernel,
        grid_spec=grid_spec,
        out_shape=jax.ShapeDtypeStruct((n_tiles * tm * SUBLANES, LANES), F32),
        compiler_params=_params(("parallel", "arbitrary")),
        name="moe_experts",
    )(tile_expert, meta, xs, gain, w_gate_up, w_gate_up, w_down)


def _moe_combine_kernel(cur_ref, nxt_ref, x_ref, route_ref, y_hbm, o_ref, buf, sem, *, tm):
    p = pl.program_id(0)
    n = pl.num_programs(0)
    chunks = x_ref.shape[1] // LANES

    def issue(idx_ref, slot):
        def body(r, carry):
            _copy_token(y_hbm, buf.at[slot], idx_ref[0, r], r, sem.at[slot]).start()
            return carry
        lax.fori_loop(0, 2 * tm, body, 0, unroll=8)

    @pl.when(p == 0)
    def _():
        issue(cur_ref, 0)

    @pl.when(p + 1 < n)
    def _():
        issue(nxt_ref, (p + 1) % 2)

    slot = p % 2
    pltpu.make_async_copy(y_hbm.at[pl.ds(0, 2 * tm * SUBLANES)], buf.at[slot], sem.at[slot]).wait()
    route = route_ref[...]
    g1 = route[:, 2:3]
    g2 = route[:, 3:4]
    for c in range(chunks):
        first = buf[slot, pl.ds(c, tm, stride=SUBLANES), :]
        second = buf[slot, pl.ds(tm * SUBLANES + c, tm, stride=SUBLANES), :]
        sl = slice(c * LANES, (c + 1) * LANES)
        o_ref[:, sl] = x_ref[:, sl] + (g1 * first + g2 * second)


def moe_combine(x2, y_tiles, pos, route, *, tm):
    t, d = x2.shape
    n = t // tm
    idx_spec = lambda shift: pl.BlockSpec((None, 1, 2 * tm), lambda p: (jnp.minimum(p + shift, n - 1), 0, 0),
                                          memory_space=pltpu.SMEM)
    return pl.pallas_call(
        functools.partial(_moe_combine_kernel, tm=tm),
        grid=(n,),
        in_specs=[idx_spec(0), idx_spec(1), pl.BlockSpec((tm, d), lambda p: (p, 0)),
                  pl.BlockSpec((tm, LANES), lambda p: (p, 0)), pl.BlockSpec(memory_space=pl.ANY)],
        out_specs=pl.BlockSpec((tm, d), lambda p: (p, 0)),
        out_shape=jax.ShapeDtypeStruct((t, d), F32),
        scratch_shapes=[pltpu.VMEM((2, 2 * tm * SUBLANES, LANES), F32), pltpu.SemaphoreType.DMA((2,))],
        compiler_params=_params(("arbitrary",)),
        name="moe_combine",
    )(pos, pos, x2, route, y_tiles)


def _moe_plan(route, te, tm):
    t = route.shape[0]
    experts = route[:, :TOP_K].astype(jnp.int32)
    onehot = (experts[:, :, None] == jnp.arange(N_EXPERTS)[None, None, :]).astype(jnp.int32).sum(axis=1)
    incl = jnp.cumsum(onehot, axis=0)
    rank = incl - onehot
    counts = incl[-1]
    tiles_per = (counts + te - 1) // te
    tile_end = jnp.cumsum(tiles_per)
    row_start = (tile_end - tiles_per) * te
    dest = row_start[experts] + jnp.take_along_axis(rank, experts, axis=1)
    n_tiles = -(-(TOP_K * t) // te) + N_EXPERTS
    tile_expert = jnp.minimum(jnp.searchsorted(tile_end, jnp.arange(n_tiles), side="right"),
                              N_EXPERTS - 1).astype(jnp.int32)
    n_used = tile_end[-1]
    tail = jnp.minimum((n_used + jnp.arange(N_EXPERTS)) * te, n_tiles * te)
    zero_starts = jnp.concatenate([row_start + counts, tail]).astype(jnp.int32)
    pos = jnp.concatenate([dest[:, k].reshape(t // tm, tm) for k in range(TOP_K)], axis=1)
    return (pos.reshape(t // tm, 1, TOP_K * tm).astype(jnp.int32), zero_starts, tile_expert,
            n_used.reshape(1).astype(jnp.int32), n_tiles)


def _router_kernel(x_ref, g_ref, r_ref, b_ref, c_ref):
    h = _rms(x_ref[...], g_ref[...])
    r = r_ref[...]
    h_hi = h.astype(BF16)
    h_lo = (h - h_hi.astype(F32)).astype(BF16)
    r_hi = r.astype(BF16)
    r_lo = (r - r_hi.astype(F32)).astype(BF16)
    both = jnp.dot(h_hi, jnp.concatenate([r_hi, r_lo], axis=1), preferred_element_type=F32)
    logits = both[:, :LANES] + both[:, LANES:] + jnp.dot(h_lo, r_hi, preferred_element_type=F32)
    logits = logits + b_ref[...]
    lane = lax.broadcasted_iota(jnp.int32, logits.shape, 1)
    logits = jnp.where(lane < N_EXPERTS, logits, NEG_INF)
    v1 = jnp.max(logits, axis=-1, keepdims=True)
    i1 = jnp.min(jnp.where(logits == v1, lane, LANES), axis=-1, keepdims=True)
    rest = jnp.where(lane == i1, NEG_INF, logits)
    v2 = jnp.max(rest, axis=-1, keepdims=True)
    i2 = jnp.min(jnp.where(rest == v2, lane, LANES), axis=-1, keepdims=True)
    e2 = jnp.exp(v2 - v1)
    g1 = 1.0 / (1.0 + e2)
    g2 = e2 / (1.0 + e2)
    c_ref[...] = jnp.where(lane == 0, i1.astype(F32), jnp.where(lane == 1, i2.astype(F32),
                           jnp.where(lane == 2, g1, jnp.where(lane == 3, g2, 0.0))))


def moe_router(x2, gain, router, router_b, *, tm):
    t, d = x2.shape
    r_pad = jnp.zeros((d, LANES), F32).at[:, :N_EXPERTS].set(router)
    b_pad = jnp.zeros((1, LANES), F32).at[0, :N_EXPERTS].set(router_b)
    return pl.pallas_call(
        _router_kernel,
        grid=(t // tm,),
        in_specs=[
            pl.BlockSpec((tm, d), lambda i: (i, 0)),
            pl.BlockSpec((1, d), lambda i: (0, 0)),
            pl.BlockSpec((d, LANES), lambda i: (0, 0)),
            pl.BlockSpec((1, LANES), lambda i: (0, 0)),
        ],
        out_specs=pl.BlockSpec((tm, LANES), lambda i: (i, 0)),
        out_shape=jax.ShapeDtypeStruct((t, LANES), F32),
        compiler_params=_params(("parallel",)),
        name="moe_router",
    )(x2, gain, r_pad, b_pad)


def _inv_freq(dim, theta):
    return theta ** (-jnp.arange(0, dim, 2, dtype=F32) / dim)


_Q = HEAD_DIM // 4
AXIAL_PERM = np.concatenate([np.arange(0, _Q), np.arange(2 * _Q, 3 * _Q), np.arange(_Q, 2 * _Q),
                             np.arange(3 * _Q, 4 * _Q)])
_R = ROT_DIMS // 2
_REST = (HEAD_DIM - ROT_DIMS) // 2
PARTIAL_PERM = np.concatenate([np.arange(0, _R), np.arange(ROT_DIMS, ROT_DIMS + _REST), np.arange(_R, ROT_DIMS),
                               np.arange(ROT_DIMS + _REST, HEAD_DIM)])


def _axial_tables(seq):
    t = jnp.arange(seq)
    rows = (t // GRID_W).astype(F32)
    cols = (t % GRID_W).astype(F32)
    freq = _inv_freq(HEAD_DIM // 2, AXIAL_THETA)
    ang_r = rows[:, None] * freq[None, :]
    ang_c = cols[:, None] * freq[None, :]
    ang = jnp.concatenate([ang_r, ang_c, ang_r, ang_c], axis=-1)
    sign = jnp.concatenate([-jnp.ones(ROPE_SHIFT, F32), jnp.ones(ROPE_SHIFT, F32)])
    return jnp.cos(ang), jnp.sin(ang) * sign[None, :]


def _partial_tables(seq):
    pos = jnp.arange(seq, dtype=F32)
    freq = _inv_freq(ROT_DIMS, ROPE_THETA)
    ang = pos[:, None] * freq[None, :]
    ones = jnp.ones((seq, _REST), F32)
    zeros = jnp.zeros((seq, _REST), F32)
    cos = jnp.concatenate([jnp.cos(ang), ones, jnp.cos(ang), ones], axis=-1)
    sin = jnp.concatenate([-jnp.sin(ang), zeros, jnp.sin(ang), zeros], axis=-1)
    return cos, sin


def _permute_qk_columns(w, head_is_qk, perm):
    idx = np.concatenate([h * HEAD_DIM + (perm if qk else np.arange(HEAD_DIM))
                          for h, qk in enumerate(head_is_qk)])
    return jnp.take(w, jnp.asarray(idx, jnp.int32), axis=1)


def _row(v):
    return v.reshape(1, -1).astype(F32)


def _rglru_layer(x2, batch, seq, norm, w_in, conv_w, conv_b, w_a, b_a, w_x, b_x, lam, w_out):
    c = w_out.shape[0]
    yx = norm_matmul_time_major(x2, _row(norm), w_in.astype(BF16), batch, seq,
                                tm=512, tn=2048, out_dtype=F32)
    yx3 = yx.reshape(seq, batch, 2 * c)
    hs = []
    for direction in range(2):
        hs.append(rglru_scan(
            yx3, conv_w, _row(conv_b), w_a[direction].astype(BF16), _row(b_a[direction]),
            w_x[direction].astype(BF16), _row(b_x[direction]), _row(lam[direction]),
            reverse=direction == 1, ts=128, tc=c))
    return rglru_out_project(yx, hs[0], hs[1], w_out.astype(BF16), x2, batch, seq, tm=512)


def _gqa_layer(x2, batch, seq, norm, w_qkv, q_gain, k_gain, w_out):
    nq = GQA_HEADS * HEAD_DIM
    nkv = GQA_KV_HEADS * HEAD_DIM
    modes = [MODE_QUERY] * (nq // PAIR) + [MODE_KEY] * (nkv // PAIR) + [MODE_PLAIN] * (nkv // PAIR)
    cos, sin = _axial_tables(seq)
    is_qk = [True] * (GQA_HEADS + GQA_KV_HEADS) + [False] * GQA_KV_HEADS
    w16 = _permute_qk_columns(w_qkv, is_qk, AXIAL_PERM).astype(BF16)
    qkv = qkv_project(x2, _row(norm), w16, modes, cos, sin, _row(q_gain[AXIAL_PERM]), _row(k_gain[AXIAL_PERM]),
                      batch, seq, tm=512, scale=HEAD_DIM ** -0.5 * LOG2_E, dil=1)
    o = gqa_attention(qkv.reshape(batch, seq, nq + 2 * nkv), tq=256, tk=512, streams=2)
    return matmul_residual(o.reshape(batch * seq, nq), w_out.astype(BF16), x2, tm=512)


def _dilated_layer(x2, batch, seq, norm, w_qkv, q_gain, k_gain, w_out):
    hw = DIL_HEADS * HEAD_DIM
    modes = [MODE_QUERY] * (hw // PAIR) + [MODE_KEY] * (hw // PAIR) + [MODE_PLAIN] * (hw // PAIR)
    cos, sin = _partial_tables(seq)
    is_qk = ([True] * (2 * DIL_HEADS) + [False] * DIL_HEADS) * len(DIL_PATTERNS)
    w16 = _permute_qk_columns(w_qkv, is_qk, PARTIAL_PERM).astype(BF16)
    outs, lses = [], []
    for g, (window, dil) in enumerate(DIL_PATTERNS):
        span = window // (2 * dil)
        assert seq % (dil * span) == 0
        qkv = qkv_project(x2, _row(norm), w16[:, g * 3 * hw:(g + 1) * 3 * hw], modes, cos, sin,
                          _row(q_gain[PARTIAL_PERM]), _row(k_gain[PARTIAL_PERM]), batch, seq,
                          tm=512, scale=HEAD_DIM ** -0.5 * LOG2_E, dil=dil)
        o, lse = dilated_attention(qkv, span, tq=min(4 * span, seq // dil))
        outs.append(o)
        lses.append(lse)
    return dilated_merge_project(outs, lses, w_out.astype(BF16), x2, batch, seq, tm=512)


def _dense_ffn_layer(x2, norm, w_gate_up, w_down):
    return swiglu_ffn(x2, _row(norm), w_gate_up.astype(BF16), w_down.astype(BF16), tm=1024, tf=512)


def _moe_layer(x2, norm, router, router_b, w_gate_up, w_down):
    tm, te = 512, 1024
    route = moe_router(x2, _row(norm), router, router_b, tm=tm)
    pos, zero_starts, tile_expert, meta, n_tiles = _moe_plan(route, te, tm)
    xs = moe_scatter(x2, pos, zero_starts, n_tiles, tm=tm, te=te)
    ys = moe_experts(xs, _row(norm), w_gate_up, w_down, tile_expert, meta, n_tiles,
                     tm=te, tf=512)
    return moe_combine(x2, ys, pos, route, tm=tm)


def kernel(x, l0_attn_norm, l0_rg_w_in, l0_rg_conv_w, l0_rg_conv_b, l0_rg_w_a, l0_rg_b_a, l0_rg_w_x, l0_rg_b_x, l0_rg_lambda, l0_rg_w_out, l0_ffn_norm, l0_ffn_w_gate_up, l0_ffn_w_down, l1_attn_norm, l1_gqa_w_qkv, l1_gqa_q_norm, l1_gqa_k_norm, l1_gqa_w_out, l1_ffn_norm, l1_moe_router, l1_moe_router_b, l1_moe_w_gate_up, l1_moe_w_down, l2_attn_norm, l2_dil_w_qkv, l2_dil_q_norm, l2_dil_k_norm, l2_dil_w_out, l2_ffn_norm, l2_ffn_w_gate_up, l2_ffn_w_down, l3_attn_norm, l3_rg_w_in, l3_rg_conv_w, l3_rg_conv_b, l3_rg_w_a, l3_rg_b_a, l3_rg_w_x, l3_rg_b_x, l3_rg_lambda, l3_rg_w_out, l3_ffn_norm, l3_moe_router, l3_moe_router_b, l3_moe_w_gate_up, l3_moe_w_down):
    batch, seq, d = x.shape
    x2 = x.reshape(batch * seq, d)
    x2 = _rglru_layer(x2, batch, seq, l0_attn_norm, l0_rg_w_in, l0_rg_conv_w, l0_rg_conv_b, l0_rg_w_a,
                      l0_rg_b_a, l0_rg_w_x, l0_rg_b_x, l0_rg_lambda, l0_rg_w_out)
    x2 = _dense_ffn_layer(x2, l0_ffn_norm, l0_ffn_w_gate_up, l0_ffn_w_down)
    x2 = _gqa_layer(x2, batch, seq, l1_attn_norm, l1_gqa_w_qkv, l1_gqa_q_norm, l1_gqa_k_norm, l1_gqa_w_out)
    x2 = _moe_layer(x2, l1_ffn_norm, l1_moe_router, l1_moe_router_b, l1_moe_w_gate_up, l1_moe_w_down)
    x2 = _dilated_layer(x2, batch, seq, l2_attn_norm, l2_dil_w_qkv, l2_dil_q_norm, l2_dil_k_norm, l2_dil_w_out)
    x2 = _dense_ffn_layer(x2, l2_ffn_norm, l2_ffn_w_gate_up, l2_ffn_w_down)
    x2 = _rglru_layer(x2, batch, seq, l3_attn_norm, l3_rg_w_in, l3_rg_conv_w, l3_rg_conv_b, l3_rg_w_a,
                      l3_rg_b_a, l3_rg_w_x, l3_rg_b_x, l3_rg_lambda, l3_rg_w_out)
    x2 = _moe_layer(x2, l3_ffn_norm, l3_moe_router, l3_moe_router_b, l3_moe_w_gate_up, l3_moe_w_down)
    return x2.reshape(batch, seq, d)
```

```python
import functools

import jax
import jax.numpy as jnp
import numpy as np
from jax import lax
from jax.experimental import pallas as pl
from jax.experimental.pallas import tpu as pltpu

F32 = jnp.float32
BF16 = jnp.bfloat16

EPS = 1e-6
NEG_INF = -1e30
GRID_W = 64
HEAD_DIM = 128
LANES = 128
RG_BLOCK_W = 128
CONV_W = 4
CONV_LEFT = 2
RG_C = 8.0
GQA_HEADS = 8
GQA_KV_HEADS = 2
GQA_GROUP = GQA_HEADS // GQA_KV_HEADS
AXIAL_THETA = 10000.0
DIL_PATTERNS = ((128, 1), (512, 4), (2048, 16))
DIL_HEADS = 8
ROPE_THETA = 500000.0
ROT_DIMS = HEAD_DIM // 4
N_EXPERTS = 8
TOP_K = 2
VMEM_LIMIT = 56 * 1024 * 1024

NT_DIMS = (((1,), (1,)), ((), ()))
LOG2_E = 1.4426950408889634


def _params(sem, vmem=VMEM_LIMIT):
    return pltpu.CompilerParams(dimension_semantics=sem, vmem_limit_bytes=vmem)


def _rms(x, g):
    return x * lax.rsqrt(jnp.mean(x * x, axis=-1, keepdims=True) + EPS) * g


def _norm_matmul_kernel(x_ref, g_ref, w_ref, o_ref, h_scr):
    @pl.when(pl.program_id(2) == 0)
    def _():
        h_scr[...] = _rms(x_ref[...], g_ref[...]).astype(BF16)

    o_ref[...] = jnp.dot(h_scr[...], w_ref[...], preferred_element_type=F32).astype(o_ref.dtype)


def norm_matmul_time_major(x2, gain, w, batch, seq, *, tm, tn, out_dtype):
    d = x2.shape[1]
    n = w.shape[1]
    ns, nj = seq // tm, n // tn
    return pl.pallas_call(
        _norm_matmul_kernel,
        grid=(batch, ns, nj),
        in_specs=[
            pl.BlockSpec((tm, d), lambda b, s, j: (b * ns + s, 0)),
            pl.BlockSpec((1, d), lambda b, s, j: (0, 0)),
            pl.BlockSpec((d, tn), lambda b, s, j: (0, j)),
        ],
        out_specs=pl.BlockSpec((tm, tn), lambda b, s, j: (s, b * nj + j)),
        out_shape=jax.ShapeDtypeStruct((seq, batch * n), out_dtype),
        scratch_shapes=[pltpu.VMEM((tm, d), BF16)],
        compiler_params=_params(("parallel", "parallel", "arbitrary")),
        name="norm_matmul_tm",
    )(x2, gain, w)


ROPE_SHIFT = LANES // 2


PAIR = 2 * LANES
MODE_PLAIN, MODE_QUERY, MODE_KEY = 0, 1, 2


def _qkv_kernel(x_ref, g_ref, w_ref, cos_ref, sin_ref, qg_ref, kg_ref, o_ref, h_scr, *stage,
                scale, dil, pair_modes):
    tm = x_ref.shape[0]
    per = tm // dil
    hn = _rms(x_ref[...], g_ref[...])
    if dil == 1:
        h_scr[...] = hn.astype(BF16)
    else:
        stage_scr, = stage
        for c in range(hn.shape[1] // LANES):
            sl = slice(c * LANES, (c + 1) * LANES)
            stage_scr[c] = hn[:, sl]
            for r in range(dil):
                h_scr[r * per:(r + 1) * per, sl] = stage_scr[c, pl.ds(r, per, stride=dil), :].astype(BF16)
    h = h_scr[...]

    row = lax.broadcasted_iota(jnp.int32, (PAIR, PAIR), 0)
    col = lax.broadcasted_iota(jnp.int32, (PAIR, PAIR), 1)
    head_ones = (row // LANES == col // LANES).astype(BF16)
    cos = cos_ref[...]
    sin = sin_ref[...]
    tables = {}
    for mode, gain, sc in ((MODE_QUERY, qg_ref[...], scale), (MODE_KEY, kg_ref[...], 1.0)):
        if mode in pair_modes:
            tables[mode] = (cos * (gain * sc), sin * (pltpu.roll(gain, ROPE_SHIFT, 1) * sc))

    accs = [jnp.dot(h, w_ref[:, i * PAIR:(i + 1) * PAIR], preferred_element_type=F32)
            for i in range(len(pair_modes))]
    sumsqs = [None if mode == MODE_PLAIN else
              jnp.dot((a * a).astype(BF16), head_ones, preferred_element_type=F32)
              for a, mode in zip(accs, pair_modes)]
    for i, (a, sumsq, mode) in enumerate(zip(accs, sumsqs, pair_modes)):
        for c in range(PAIR // LANES):
            sl = slice(c * LANES, (c + 1) * LANES)
            head = a[:, sl]
            if mode != MODE_PLAIN:
                c1, s1 = tables[mode]
                inv = lax.rsqrt(sumsq[:, sl] * (1.0 / HEAD_DIM) + EPS)
                head = (head * c1 + pltpu.roll(head, ROPE_SHIFT, 1) * s1) * inv
            out_sl = slice(i * PAIR + c * LANES, i * PAIR + (c + 1) * LANES)
            o_ref[:, :, out_sl] = head.reshape(dil, per, LANES).astype(o_ref.dtype)


def _tile_residue_major(table, dil, tm):
    s, w = table.shape
    return table.reshape(s // tm, tm // dil, dil, w).transpose(0, 2, 1, 3).reshape(s, w)


def qkv_project(x2, gain, w, pair_modes, cos, sin, q_gain, k_gain, batch, seq, *, tm, scale, dil):
    t, d = x2.shape
    n = w.shape[1]
    assert n == len(pair_modes) * PAIR
    ns = seq // tm
    per = tm // dil
    cos_r = _tile_residue_major(cos, dil, tm)
    sin_r = _tile_residue_major(sin, dil, tm)
    table_spec = pl.BlockSpec((tm, LANES), lambda i: (i % ns, 0))
    vec_spec = pl.BlockSpec((1, LANES), lambda i: (0, 0))
    out = pl.pallas_call(
        functools.partial(_qkv_kernel, scale=scale, dil=dil, pair_modes=tuple(pair_modes)),
        grid=(t // tm,),
        in_specs=[
            pl.BlockSpec((tm, d), lambda i: (i, 0)),
            pl.BlockSpec((1, d), lambda i: (0, 0)),
            pl.BlockSpec((d, n), lambda i: (0, 0)),
            table_spec, table_spec, vec_spec, vec_spec,
        ],
        out_specs=pl.BlockSpec((None, dil, None, per, n), lambda i: (i // ns, 0, i % ns, 0, 0)),
        out_shape=jax.ShapeDtypeStruct((batch, dil, ns, per, n), BF16),
        scratch_shapes=[pltpu.VMEM((tm, d), BF16)]
        + ([pltpu.VMEM((d // LANES, tm, LANES), F32)] if dil > 1 else []),
        compiler_params=_params(("parallel",)),
        name=f"qkv_project_d{dil}",
    )(x2, gain, w, cos_r, sin_r, q_gain, k_gain)
    return out.reshape(batch, dil, seq // dil, n)


def _matmul_residual_kernel(a_ref, w_ref, x_ref, o_ref):
    o_ref[...] = x_ref[...] + jnp.dot(a_ref[...], w_ref[...], preferred_element_type=F32)


def matmul_residual(a, w, x2, *, tm):
    t, k = a.shape
    n = w.shape[1]
    return pl.pallas_call(
        _matmul_residual_kernel,
        grid=(t // tm,),
        in_specs=[
            pl.BlockSpec((tm, k), lambda i: (i, 0)),
            pl.BlockSpec((k, n), lambda i: (0, 0)),
            pl.BlockSpec((tm, n), lambda i: (i, 0)),
        ],
        out_specs=pl.BlockSpec((tm, n), lambda i: (i, 0)),
        out_shape=jax.ShapeDtypeStruct((t, n), F32),
        compiler_params=_params(("parallel",)),
        name="matmul_residual",
    )(a, w, x2)


def _gqa_attn_kernel(q_ref, k_ref, v_ref, o_ref, *, tk, streams):
    tq = q_ref.shape[1]
    seq = k_ref.shape[1]
    per = GQA_GROUP // streams
    qs = [jnp.concatenate([q_ref[0, :, g * LANES:(g + 1) * LANES] for g in range(st * per, (st + 1) * per)],
                          axis=0) for st in range(streams)]
    rows = per * tq
    ones = jnp.ones((tk, LANES), BF16)

    def body(c, carry):
        start = pl.multiple_of(c * tk, tk)
        k = k_ref[0, pl.ds(start, tk), :]
        v1 = jnp.concatenate([v_ref[0, pl.ds(start, tk), :], ones], axis=1)
        new = []
        for st in range(streams):
            m_prev, acc = carry[st]
            s = lax.dot_general(qs[st], k, NT_DIMS, preferred_element_type=F32)
            m_new = jnp.maximum(m_prev, jnp.max(s, axis=-1, keepdims=True))
            alpha = jnp.exp2(m_prev - m_new)
            p = jnp.exp2(s - m_new).astype(BF16)
            acc = alpha * acc + jnp.dot(p, v1, preferred_element_type=F32)
            new.append((m_new, acc))
        return tuple(new)

    init = tuple((jnp.full((rows, 1), NEG_INF, F32), jnp.zeros((rows, 2 * LANES), F32)) for _ in range(streams))
    final = lax.fori_loop(0, seq // tk, body, init, unroll=16)
    for st in range(streams):
        acc = final[st][1]
        o = acc[:, :LANES] / acc[:, LANES:]
        for gi in range(per):
            g = st * per + gi
            o_ref[0, :, g * LANES:(g + 1) * LANES] = o[gi * tq:(gi + 1) * tq].astype(o_ref.dtype)


def gqa_attention(qkv, *, tq, tk, streams):
    b, s, _ = qkv.shape
    gw = GQA_GROUP * HEAD_DIM
    k_off = GQA_HEADS
    v_off = GQA_HEADS + GQA_KV_HEADS
    return pl.pallas_call(
        functools.partial(_gqa_attn_kernel, tk=tk, streams=streams),
        grid=(b, GQA_KV_HEADS, s // tq),
        in_specs=[
            pl.BlockSpec((1, tq, gw), lambda bb, n, i: (bb, i, n)),
            pl.BlockSpec((1, s, HEAD_DIM), lambda bb, n, i: (bb, 0, k_off + n)),
            pl.BlockSpec((1, s, HEAD_DIM), lambda bb, n, i: (bb, 0, v_off + n)),
        ],
        out_specs=pl.BlockSpec((1, tq, gw), lambda bb, n, i: (bb, i, n)),
        out_shape=jax.ShapeDtypeStruct((b, s, GQA_HEADS * HEAD_DIM), BF16),
        compiler_params=_params(("parallel", "parallel", "arbitrary")),
        name="gqa_attention",
    )(qkv, qkv, qkv)


def _dil_attn_kernel(q_ref, kp_ref, kc_ref, kn_ref, vp_ref, vc_ref, vn_ref, o_ref, lse_ref,
                     *, span, m_total):
    i = pl.program_id(2)
    tq = q_ref.shape[0]
    qb = 2 * span
    nk = qb + 2 * span
    row = lax.broadcasted_iota(jnp.int32, (qb, nk), 0)
    col = lax.broadcasted_iota(jnp.int32, (qb, nk), 1)
    in_band = jnp.abs(col - span - row) <= span
    lane = lax.broadcasted_iota(jnp.int32, (qb, LANES), 1)
    ones = jnp.ones((nk, LANES), BF16)
    for sub in range(tq // qb):
        key_idx = i * tq + sub * qb - span + col
        ok = in_band & (key_idx >= 0) & (key_idx < m_total)
        lse_tile = jnp.zeros((qb, LANES), F32)
        rows = slice(sub * qb, (sub + 1) * qb)
        win = slice(sub * qb, sub * qb + nk)
        heads = [slice(h * LANES, (h + 1) * LANES) for h in range(DIL_HEADS)]
        scores = [lax.dot_general(q_ref[rows, sl],
                                  jnp.concatenate([kp_ref[:, sl], kc_ref[:, sl], kn_ref[:, sl]], axis=0)[win],
                                  NT_DIMS, preferred_element_type=F32) for sl in heads]
        maxes, probs = [], []
        for s in scores:
            s = jnp.where(ok, s, NEG_INF)
            m = jnp.max(s, axis=-1, keepdims=True)
            maxes.append(m)
            probs.append(jnp.exp2(s - m).astype(BF16))
        accs = [jnp.dot(p, jnp.concatenate(
                    [jnp.concatenate([vp_ref[:, sl], vc_ref[:, sl], vn_ref[:, sl]], axis=0)[win], ones], axis=1),
                    preferred_element_type=F32) for p, sl in zip(probs, heads)]
        for h, (acc, m, sl) in enumerate(zip(accs, maxes, heads)):
            o_ref[rows, sl] = (acc[:, :LANES] / acc[:, LANES:]).astype(o_ref.dtype)
            lse_tile = jnp.where(lane == h, m + jnp.log2(acc[:, LANES:LANES + 1]), lse_tile)
        lse_ref[rows, :] = lse_tile


def dilated_attention(qkv, span, *, tq):
    b, dil, m_total, n = qkv.shape
    hw = DIL_HEADS * HEAD_DIM
    nb = m_total // tq
    halo_per_tile = tq // span
    n_halo = m_total // span

    def tile(col):
        return pl.BlockSpec((None, None, tq, hw), lambda bb, r, i: (bb, r, i, col))

    def before(col):
        return pl.BlockSpec((None, None, span, hw),
                            lambda bb, r, i: (bb, r, jnp.maximum(i * halo_per_tile - 1, 0), col))

    def after(col):
        return pl.BlockSpec((None, None, span, hw),
                            lambda bb, r, i: (bb, r, jnp.minimum((i + 1) * halo_per_tile, n_halo - 1), col))

    return pl.pallas_call(
        functools.partial(_dil_attn_kernel, span=span, m_total=m_total),
        grid=(b, dil, nb),
        in_specs=[tile(0), before(1), tile(1), after(1), before(2), tile(2), after(2)],
        out_specs=[
            pl.BlockSpec((None, None, tq, hw), lambda bb, r, i: (bb, r, i, 0)),
            pl.BlockSpec((None, None, tq, LANES), lambda bb, r, i: (bb, r, i, 0)),
        ],
        out_shape=[
            jax.ShapeDtypeStruct((b, dil, m_total, hw), BF16),
            jax.ShapeDtypeStruct((b, dil, m_total, LANES), F32),
        ],
        compiler_params=_params(("parallel", "parallel", "arbitrary")),
        name=f"dilated_attention_d{dil}",
    )(qkv, qkv, qkv, qkv, qkv, qkv, qkv)


def _dil_merge_kernel(o0_ref, o1_ref, o2_ref, l0_ref, l1_ref, l2_ref, w_ref, x_ref, out_ref, o_scr, l_scr):
    def to_tokens(ref, scr, g):
        dil, per, width = ref.shape
        for c in range(width // LANES):
            sl = slice(c * LANES, (c + 1) * LANES)
            if dil == 1:
                scr[g, c] = ref[0, :, sl].astype(F32)
            else:
                for r in range(dil):
                    scr[g, c, pl.ds(r, per, stride=dil), :] = ref[r, :, sl].astype(F32)

    for g, (o_ref, l_ref) in enumerate(((o0_ref, l0_ref), (o1_ref, l1_ref), (o2_ref, l2_ref))):
        to_tokens(o_ref, o_scr, g)
        to_tokens(l_ref, l_scr, g)
    lses = [l_scr[g, 0] for g in range(3)]
    mx = jnp.maximum(jnp.maximum(lses[0], lses[1]), lses[2])
    es = [jnp.exp2(l - mx) for l in lses]
    den = es[0] + es[1] + es[2]
    ws = [e / den for e in es]
    parts = []
    for h in range(DIL_HEADS):
        acc = None
        for g in range(3):
            term = o_scr[g, h] * ws[g][:, h:h + 1]
            acc = term if acc is None else acc + term
        parts.append(acc.astype(BF16))
    merged = jnp.concatenate(parts, axis=1)
    out_ref[...] = x_ref[...] + jnp.dot(merged, w_ref[...], preferred_element_type=F32)


def dilated_merge_project(outs, lses, w_out, x2, batch, seq, *, tm):
    t, d = x2.shape
    hw = w_out.shape[0]
    ns = seq // tm

    def grouped(arr):
        _, dil, m_total, w = arr.shape
        per = tm // dil
        view = arr.reshape(batch, dil, ns, per, w)
        return view, pl.BlockSpec((None, dil, None, per, w), lambda i: (i // ns, 0, i % ns, 0, 0))

    views, specs = zip(*[grouped(a) for a in (*outs, *lses)])
    row = pl.BlockSpec((tm, d), lambda i: (i, 0))
    return pl.pallas_call(
        _dil_merge_kernel,
        grid=(t // tm,),
        in_specs=list(specs) + [pl.BlockSpec((hw, d), lambda i: (0, 0)), row],
        out_specs=row,
        out_shape=jax.ShapeDtypeStruct((t, d), F32),
        scratch_shapes=[pltpu.VMEM((3, hw // LANES, tm, LANES), F32), pltpu.VMEM((3, 1, tm, LANES), F32)],
        compiler_params=_params(("parallel",)),
        name="dilated_merge_project",
    )(*views, w_out, x2)


def _rglru_scan_kernel(xc_ref, xp_ref, xn_ref, cw_ref, cb_ref, wa_ref, ba_ref, wx_ref, bx_ref, lam_ref,
                       o_ref, a_scr, u_scr, h_scr, o_scr, *, reverse, n_steps):
    s = pl.program_id(1)
    ts, nb, tc = xc_ref.shape
    chunk = (n_steps - 1 - s) if reverse else s

    @pl.when(s == 0)
    def _():
        h_scr[...] = jnp.zeros(h_scr.shape, F32)

    prev = jnp.where(chunk == 0, 0.0, xp_ref[...])
    nxt = jnp.where(chunk == n_steps - 1, 0.0, xn_ref[...])
    xe = jnp.concatenate([prev, xc_ref[...], nxt], axis=0)
    cw = cw_ref[...]
    xc = cb_ref[...].reshape(1, 1, tc) + xe[0:ts] * cw[0:1].reshape(1, 1, tc)
    for j in range(1, CONV_W):
        xc = xc + xe[j:j + ts] * cw[j:j + 1].reshape(1, 1, tc)
    xc2 = xc.reshape(ts * nb, tc)
    lam = lam_ref[...]
    softplus_neg_lam = jnp.maximum(-lam, 0.0) + jnp.log1p(jnp.exp(-jnp.abs(lam)))
    decay_log2 = (-RG_C * LOG2_E) * softplus_neg_lam

    def sigmoid(x16, w_ref, bias):
        half_z = jnp.dot(x16, w_ref[kb] * 0.5, preferred_element_type=F32) + 0.5 * bias
        return 0.5 * jnp.tanh(half_z) + 0.5

    for kb in range(tc // RG_BLOCK_W):
        sl = slice(kb * RG_BLOCK_W, (kb + 1) * RG_BLOCK_W)
        xb = xc2[:, sl]
        xb16 = xb.astype(BF16)
        r = sigmoid(xb16, wa_ref, ba_ref[:, sl])
        gate = sigmoid(xb16, wx_ref, bx_ref[:, sl])
        a = jnp.exp2(r * decay_log2[:, sl])
        z = 1.0 - a * a
        u = (z * lax.rsqrt(jnp.maximum(z, 1e-30))) * (gate * xb)
        a_scr[:, :, sl] = a.reshape(ts, nb, RG_BLOCK_W)
        u_scr[:, :, sl] = u.reshape(ts, nb, RG_BLOCK_W)

    def step(t, h):
        tt = (ts - 1 - t) if reverse else t
        h = a_scr[tt] * h + u_scr[tt]
        row = pl.multiple_of(tt * nb, nb)
        for kb in range(tc // LANES):
            o_scr[kb, pl.ds(row, nb), :] = h[:, kb * LANES:(kb + 1) * LANES]
        return h

    h_scr[...] = lax.fori_loop(0, ts, step, h_scr[...], unroll=8)
    for b in range(nb):
        for kb in range(tc // LANES):
            o_ref[:, b * tc + kb * LANES:b * tc + (kb + 1) * LANES] = o_scr[kb, pl.ds(b, ts, stride=nb), :]


def rglru_scan(yx3, conv_w, conv_b, w_a, b_a, w_x, b_x, lam, *, reverse, ts, tc):
    seq, nb, two_c = yx3.shape
    c = two_c // 2
    assert tc == c
    nc, n_steps = c // tc, seq // ts
    kb = tc // RG_BLOCK_W

    def chunk_of(s):
        return (n_steps - 1 - s) if reverse else s

    vec = lambda rows: pl.BlockSpec((rows, tc), lambda ci, s: (0, ci))
    mat = pl.BlockSpec((kb, RG_BLOCK_W, RG_BLOCK_W), lambda ci, s: (ci, 0, 0))
    return pl.pallas_call(
        functools.partial(_rglru_scan_kernel, reverse=reverse, n_steps=n_steps),
        grid=(nc, n_steps),
        in_specs=[
            pl.BlockSpec((ts, nb, tc), lambda ci, s: (chunk_of(s), 0, nc + ci)),
            pl.BlockSpec((CONV_LEFT, nb, tc),
                         lambda ci, s: (jnp.maximum(chunk_of(s) * (ts // CONV_LEFT) - 1, 0), 0, nc + ci)),
            pl.BlockSpec((1, nb, tc),
                         lambda ci, s: (jnp.minimum((chunk_of(s) + 1) * ts, seq - 1), 0, nc + ci)),
            vec(CONV_W), vec(1), mat, vec(1), mat, vec(1), vec(1),
        ],
        out_specs=pl.BlockSpec((ts, nb * tc), lambda ci, s: (chunk_of(s), 0)),
        out_shape=jax.ShapeDtypeStruct((seq, nb * c), F32),
        scratch_shapes=[
            pltpu.VMEM((ts, nb, tc), F32),
            pltpu.VMEM((ts, nb, tc), F32),
            pltpu.VMEM((nb, tc), F32),
            pltpu.VMEM((tc // LANES, ts * nb, LANES), F32),
        ],
        compiler_params=_params(("parallel", "arbitrary")),
        name="rglru_scan_bwd" if reverse else "rglru_scan_fwd",
    )(yx3, yx3, yx3, conv_w, conv_b, w_a, b_a, w_x, b_x, lam)


def _gelu_tanh(y):
    return y * (0.5 * (1.0 + jnp.tanh(0.7978845608028654 * (y + 0.044715 * (y * y * y)))))


def _rglru_out_kernel(y_ref, hf_ref, hb_ref, w_ref, x_ref, o_ref):
    gated = _gelu_tanh(y_ref[...]) * (hf_ref[...] + hb_ref[...])
    o_ref[...] = x_ref[...] + jnp.dot(gated.astype(BF16), w_ref[...], preferred_element_type=F32)


def rglru_out_project(yx, hf, hb, w_out, x2, batch, seq, *, tm):
    c, d = w_out.shape
    ns = seq // tm
    return pl.pallas_call(
        _rglru_out_kernel,
        grid=(batch, ns),
        in_specs=[
            pl.BlockSpec((tm, c), lambda b, s: (s, 2 * b)),
            pl.BlockSpec((tm, c), lambda b, s: (s, b)),
            pl.BlockSpec((tm, c), lambda b, s: (s, b)),
            pl.BlockSpec((c, d), lambda b, s: (0, 0)),
            pl.BlockSpec((tm, d), lambda b, s: (b * ns + s, 0)),
        ],
        out_specs=pl.BlockSpec((tm, d), lambda b, s: (b * ns + s, 0)),
        out_shape=jax.ShapeDtypeStruct(x2.shape, F32),
        compiler_params=_params(("parallel", "parallel")),
        name="rglru_out_project",
    )(yx, hf, hb, w_out, x2)


def _swiglu_step(h, wg_ref, wu_ref, wd_ref, acc_scr):
    gate = jnp.dot(h, wg_ref[...].astype(BF16), preferred_element_type=F32)
    up = jnp.dot(h, wu_ref[...].astype(BF16), preferred_element_type=F32)
    act = gate * jax.nn.sigmoid(gate) * up
    acc_scr[...] += jnp.dot(act.astype(BF16), wd_ref[...].astype(BF16), preferred_element_type=F32)


def _ffn_kernel(x_ref, g_ref, wg_ref, wu_ref, wd_ref, o_ref, h_scr, acc_scr):
    f = pl.program_id(1)

    @pl.when(f == 0)
    def _():
        h_scr[...] = _rms(x_ref[...], g_ref[...]).astype(BF16)
        acc_scr[...] = jnp.zeros(acc_scr.shape, F32)

    _swiglu_step(h_scr[...], wg_ref, wu_ref, wd_ref, acc_scr)

    @pl.when(f == pl.num_programs(1) - 1)
    def _():
        o_ref[...] = x_ref[...] + acc_scr[...]


def swiglu_ffn(x2, gain, w_gate_up, w_down, *, tm, tf):
    t, d = x2.shape
    ff = w_down.shape[0]
    nf = ff // tf
    return pl.pallas_call(
        _ffn_kernel,
        grid=(t // tm, nf),
        in_specs=[
            pl.BlockSpec((tm, d), lambda i, f: (i, 0)),
            pl.BlockSpec((1, d), lambda i, f: (0, 0)),
            pl.BlockSpec((d, tf), lambda i, f: (0, f)),
            pl.BlockSpec((d, tf), lambda i, f: (0, nf + f)),
            pl.BlockSpec((tf, d), lambda i, f: (f, 0)),
        ],
        out_specs=pl.BlockSpec((tm, d), lambda i, f: (i, 0)),
        out_shape=jax.ShapeDtypeStruct((t, d), F32),
        scratch_shapes=[pltpu.VMEM((tm, d), BF16), pltpu.VMEM((tm, d), F32)],
        compiler_params=_params(("parallel", "arbitrary")),
        name="dense_ffn",
    )(x2, gain, w_gate_up, w_gate_up, w_down)


SUBLANES = 8


def _copy_token(src_hbm, dst, src_tok, dst_tok, sem):
    return pltpu.make_async_copy(src_hbm.at[pl.ds(src_tok * SUBLANES, SUBLANES)],
                                 dst.at[pl.ds(dst_tok * SUBLANES, SUBLANES)], sem)


def _moe_scatter_kernel(zero_ref, pos_ref, x_ref, o_hbm, stage, zeros_scr, sem, *, tm):
    p = pl.program_id(0)
    n = pl.num_programs(0)
    slot = p % 2
    chunks = x_ref.shape[1] // LANES
    block_rows = tm * SUBLANES

    def zero_copy(j):
        dst = o_hbm.at[pl.ds(zero_ref[j] * SUBLANES, zeros_scr.shape[0])]
        return pltpu.make_async_copy(zeros_scr, dst, sem.at[2])

    @pl.when(p == 0)
    def _():
        zeros_scr[...] = jnp.zeros(zeros_scr.shape, F32)
        for j in range(2 * N_EXPERTS):
            zero_copy(j).start()
            zero_copy(j).wait()

    def slot_wait(s):
        rows = pl.ds(0, 2 * block_rows)
        pltpu.make_async_copy(o_hbm.at[rows], o_hbm.at[rows], sem.at[s]).wait()

    @pl.when(p >= 2)
    def _():
        slot_wait(slot)

    for c in range(chunks):
        stage[slot, pl.ds(c, tm, stride=SUBLANES), :] = x_ref[:, c * LANES:(c + 1) * LANES]

    def issue(r, carry):
        src = stage.at[slot, pl.ds(r * SUBLANES, SUBLANES)]
        for k in range(TOP_K):
            dst = o_hbm.at[pl.ds(pos_ref[0, k * tm + r] * SUBLANES, SUBLANES)]
            pltpu.make_async_copy(src, dst, sem.at[slot]).start(priority=k % 2)
        return carry
    lax.fori_loop(0, tm, issue, 0, unroll=8)

    @pl.when(p == n - 1)
    def _():
        slot_wait(slot)

        @pl.when(n >= 2)
        def _():
            slot_wait(1 - slot)


def moe_scatter(x2, pos, zero_starts, n_tiles, *, tm, te):
    t, d = x2.shape
    grid_spec = pltpu.PrefetchScalarGridSpec(
        num_scalar_prefetch=1,
        grid=(t // tm,),
        in_specs=[
            pl.BlockSpec((None, 1, TOP_K * tm), lambda p, z: (p, 0, 0), memory_space=pltpu.SMEM),
            pl.BlockSpec((tm, d), lambda p, z: (p, 0)),
        ],
        out_specs=pl.BlockSpec(memory_space=pl.ANY),
        scratch_shapes=[pltpu.VMEM((2, tm * SUBLANES, LANES), F32), pltpu.VMEM((te * SUBLANES, LANES), F32),
                        pltpu.SemaphoreType.DMA((3,))],
    )
    return pl.pallas_call(
        functools.partial(_moe_scatter_kernel, tm=tm),
        grid_spec=grid_spec,
        out_shape=jax.ShapeDtypeStruct(((n_tiles + 1) * te * SUBLANES, LANES), F32),
        compiler_params=_params(("arbitrary",)),
        name="moe_scatter",
    )(zero_starts, pos, x2)


def _moe_expert_kernel(te_ref, meta_ref, xs_ref, g_ref, wg_ref, wu_ref, wd_ref, y_ref, h_scr, acc_scr):
    p = pl.program_id(0)
    f = pl.program_id(1)
    tm = acc_scr.shape[0]
    chunks = acc_scr.shape[1] // LANES
    used = p < meta_ref[0]

    @pl.when(used & (f == 0))
    def _():
        x = jnp.concatenate([xs_ref[pl.ds(c, tm, stride=SUBLANES), :] for c in range(chunks)], axis=1)
        h_scr[...] = _rms(x, g_ref[...]).astype(BF16)
        acc_scr[...] = jnp.zeros(acc_scr.shape, F32)

    @pl.when(used)
    def _():
        _swiglu_step(h_scr[...], wg_ref, wu_ref, wd_ref, acc_scr)

    @pl.when(f == pl.num_programs(1) - 1)
    def _():
        y = jnp.where(used, acc_scr[...], 0.0)
        for c in range(chunks):
            y_ref[pl.ds(c, tm, stride=SUBLANES), :] = y[:, c * LANES:(c + 1) * LANES]


def moe_experts(xs, gain, w_gate_up, w_down, tile_expert, meta, n_tiles, *, tm, tf):
    n_e, ff, d = w_down.shape
    nf = ff // tf

    def f_eff(p, f, meta_ref):
        return jnp.where(p < meta_ref[0], f, nf - 1)

    grid_spec = pltpu.PrefetchScalarGridSpec(
        num_scalar_prefetch=2,
        grid=(n_tiles, nf),
        in_specs=[
            pl.BlockSpec((tm * SUBLANES, LANES), lambda p, f, te, m: (p, 0)),
            pl.BlockSpec((1, d), lambda p, f, te, m: (0, 0)),
            pl.BlockSpec((None, d, tf), lambda p, f, te, m: (te[p], 0, f_eff(p, f, m))),
            pl.BlockSpec((None, d, tf), lambda p, f, te, m: (te[p], 0, nf + f_eff(p, f, m))),
            pl.BlockSpec((None, tf, d), lambda p, f, te, m: (te[p], f_eff(p, f, m), 0)),
        ],
        out_specs=pl.BlockSpec((tm * SUBLANES, LANES), lambda p, f, te, m: (p, 0)),
        scratch_shapes=[pltpu.VMEM((tm, d), BF16), pltpu.VMEM((tm, d), F32)],
    )
    return pl.pallas_call(
        _moe_expert_kernel,
        grid_spec=grid_spec,
        out_shape=jax.ShapeDtypeStruct((n_tiles * tm * SUBLANES, LANES), F32),
        compiler_params=_params(("parallel", "arbitrary")),
        name="moe_experts",
    )(tile_expert, meta, xs, gain, w_gate_up, w_gate_up, w_down)


def _moe_combine_kernel(cur_ref, nxt_ref, x_ref, route_ref, y_hbm, o_ref, buf, sem, *, tm):
    p = pl.program_id(0)
    n = pl.num_programs(0)
    chunks = x_ref.shape[1] // LANES

    def issue(idx_ref, slot):
        def body(j, carry):
            for q in range(2):
                r = 2 * j + q
                _copy_token(y_hbm, buf.at[slot], idx_ref[0, r], r, sem.at[slot]).start(priority=q)
            return carry
        lax.fori_loop(0, tm, body, 0, unroll=4)

    @pl.when(p == 0)
    def _():
        issue(cur_ref, 0)

    @pl.when(p + 1 < n)
    def _():
        issue(nxt_ref, (p + 1) % 2)

    slot = p % 2
    pltpu.make_async_copy(y_hbm.at[pl.ds(0, 2 * tm * SUBLANES)], buf.at[slot], sem.at[slot]).wait()
    route = route_ref[...]
    g1 = route[:, 2:3]
    g2 = route[:, 3:4]
    for c in range(chunks):
        first = buf[slot, pl.ds(c, tm, stride=SUBLANES), :]
        second = buf[slot, pl.ds(tm * SUBLANES + c, tm, stride=SUBLANES), :]
        sl = slice(c * LANES, (c + 1) * LANES)
        o_ref[:, sl] = x_ref[:, sl] + (g1 * first + g2 * second)


def moe_combine(x2, y_tiles, pos, route, *, tm):
    t, d = x2.shape
    n = t // tm
    idx_spec = lambda shift: pl.BlockSpec((None, 1, 2 * tm), lambda p: (jnp.minimum(p + shift, n - 1), 0, 0),
                                          memory_space=pltpu.SMEM)
    return pl.pallas_call(
        functools.partial(_moe_combine_kernel, tm=tm),
        grid=(n,),
        in_specs=[idx_spec(0), idx_spec(1), pl.BlockSpec((tm, d), lambda p: (p, 0)),
                  pl.BlockSpec((tm, LANES), lambda p: (p, 0)), pl.BlockSpec(memory_space=pl.ANY)],
        out_specs=pl.BlockSpec((tm, d), lambda p: (p, 0)),
        out_shape=jax.ShapeDtypeStruct((t, d), F32),
        scratch_shapes=[pltpu.VMEM((2, 2 * tm * SUBLANES, LANES), F32), pltpu.SemaphoreType.DMA((2,))],
        compiler_params=_params(("arbitrary",)),
        name="moe_combine",
    )(pos, pos, x2, route, y_tiles)


def _moe_plan(route, te, tm):
    t = route.shape[0]
    experts = route[:, :TOP_K].astype(jnp.int32)
    onehot = (experts[:, :, None] == jnp.arange(N_EXPERTS)[None, None, :]).astype(jnp.int32).sum(axis=1)
    incl = jnp.cumsum(onehot, axis=0)
    rank = incl - onehot
    counts = incl[-1]
    tiles_per = (counts + te - 1) // te
    tile_end = jnp.cumsum(tiles_per)
    row_start = (tile_end - tiles_per) * te
    dest = row_start[experts] + jnp.take_along_axis(rank, experts, axis=1)
    n_tiles = -(-(TOP_K * t) // te) + N_EXPERTS
    tile_expert = jnp.minimum(jnp.searchsorted(tile_end, jnp.arange(n_tiles), side="right"),
                              N_EXPERTS - 1).astype(jnp.int32)
    n_used = tile_end[-1]
    tail = jnp.minimum((n_used + jnp.arange(N_EXPERTS)) * te, n_tiles * te)
    zero_starts = jnp.concatenate([row_start + counts, tail]).astype(jnp.int32)
    pos = jnp.concatenate([dest[:, k].reshape(t // tm, tm) for k in range(TOP_K)], axis=1)
    return (pos.reshape(t // tm, 1, TOP_K * tm).astype(jnp.int32), zero_starts, tile_expert,
            n_used.reshape(1).astype(jnp.int32), n_tiles)


def _router_kernel(x_ref, g_ref, r_ref, b_ref, c_ref):
    h = _rms(x_ref[...], g_ref[...])
    r = r_ref[...]
    h_hi = h.astype(BF16)
    h_lo = (h - h_hi.astype(F32)).astype(BF16)
    r_hi = r.astype(BF16)
    r_lo = (r - r_hi.astype(F32)).astype(BF16)
    both = jnp.dot(h_hi, jnp.concatenate([r_hi, r_lo], axis=1), preferred_element_type=F32)
    logits = both[:, :LANES] + both[:, LANES:] + jnp.dot(h_lo, r_hi, preferred_element_type=F32)
    logits = logits + b_ref[...]
    lane = lax.broadcasted_iota(jnp.int32, logits.shape, 1)
    logits = jnp.where(lane < N_EXPERTS, logits, NEG_INF)
    v1 = jnp.max(logits, axis=-1, keepdims=True)
    i1 = jnp.min(jnp.where(logits == v1, lane, LANES), axis=-1, keepdims=True)
    rest = jnp.where(lane == i1, NEG_INF, logits)
    v2 = jnp.max(rest, axis=-1, keepdims=True)
    i2 = jnp.min(jnp.where(rest == v2, lane, LANES), axis=-1, keepdims=True)
    e2 = jnp.exp(v2 - v1)
    g1 = 1.0 / (1.0 + e2)
    g2 = e2 / (1.0 + e2)
    c_ref[...] = jnp.where(lane == 0, i1.astype(F32), jnp.where(lane == 1, i2.astype(F32),
                           jnp.where(lane == 2, g1, jnp.where(lane == 3, g2, 0.0))))


def moe_router(x2, gain, router, router_b, *, tm):
    t, d = x2.shape
    r_pad = jnp.zeros((d, LANES), F32).at[:, :N_EXPERTS].set(router)
    b_pad = jnp.zeros((1, LANES), F32).at[0, :N_EXPERTS].set(router_b)
    return pl.pallas_call(
        _router_kernel,
        grid=(t // tm,),
        in_specs=[
            pl.BlockSpec((tm, d), lambda i: (i, 0)),
            pl.BlockSpec((1, d), lambda i: (0, 0)),
            pl.BlockSpec((d, LANES), lambda i: (0, 0)),
            pl.BlockSpec((1, LANES), lambda i: (0, 0)),
        ],
        out_specs=pl.BlockSpec((tm, LANES), lambda i: (i, 0)),
        out_shape=jax.ShapeDtypeStruct((t, LANES), F32),
        compiler_params=_params(("parallel",)),
        name="moe_router",
    )(x2, gain, r_pad, b_pad)


def _inv_freq(dim, theta):
    return theta ** (-jnp.arange(0, dim, 2, dtype=F32) / dim)


_Q = HEAD_DIM // 4
AXIAL_PERM = np.concatenate([np.arange(0, _Q), np.arange(2 * _Q, 3 * _Q), np.arange(_Q, 2 * _Q),
                             np.arange(3 * _Q, 4 * _Q)])
_R = ROT_DIMS // 2
_REST = (HEAD_DIM - ROT_DIMS) // 2
PARTIAL_PERM = np.concatenate([np.arange(0, _R), np.arange(ROT_DIMS, ROT_DIMS + _REST), np.arange(_R, ROT_DIMS),
                               np.arange(ROT_DIMS + _REST, HEAD_DIM)])


def _axial_tables(seq):
    t = jnp.arange(seq)
    rows = (t // GRID_W).astype(F32)
    cols = (t % GRID_W).astype(F32)
    freq = _inv_freq(HEAD_DIM // 2, AXIAL_THETA)
    ang_r = rows[:, None] * freq[None, :]
    ang_c = cols[:, None] * freq[None, :]
    ang = jnp.concatenate([ang_r, ang_c, ang_r, ang_c], axis=-1)
    sign = jnp.concatenate([-jnp.ones(ROPE_SHIFT, F32), jnp.ones(ROPE_SHIFT, F32)])
    return jnp.cos(ang), jnp.sin(ang) * sign[None, :]


def _partial_tables(seq):
    pos = jnp.arange(seq, dtype=F32)
    freq = _inv_freq(ROT_DIMS, ROPE_THETA)
    ang = pos[:, None] * freq[None, :]
    ones = jnp.ones((seq, _REST), F32)
    zeros = jnp.zeros((seq, _REST), F32)
    cos = jnp.concatenate([jnp.cos(ang), ones, jnp.cos(ang), ones], axis=-1)
    sin = jnp.concatenate([-jnp.sin(ang), zeros, jnp.sin(ang), zeros], axis=-1)
    return cos, sin


def _permute_qk_columns(w, head_is_qk, perm):
    idx = np.concatenate([h * HEAD_DIM + (perm if qk else np.arange(HEAD_DIM))
                          for h, qk in enumerate(head_is_qk)])
    return jnp.take(w, jnp.asarray(idx, jnp.int32), axis=1)


def _row(v):
    return v.reshape(1, -1).astype(F32)


def _rglru_layer(x2, batch, seq, norm, w_in, conv_w, conv_b, w_a, b_a, w_x, b_x, lam, w_out):
    c = w_out.shape[0]
    yx = norm_matmul_time_major(x2, _row(norm), w_in.astype(BF16), batch, seq,
                                tm=512, tn=2048, out_dtype=F32)
    yx3 = yx.reshape(seq, batch, 2 * c)
    hs = []
    for direction in range(2):
        hs.append(rglru_scan(
            yx3, conv_w, _row(conv_b), w_a[direction].astype(BF16), _row(b_a[direction]),
            w_x[direction].astype(BF16), _row(b_x[direction]), _row(lam[direction]),
            reverse=direction == 1, ts=128, tc=c))
    return rglru_out_project(yx, hs[0], hs[1], w_out.astype(BF16), x2, batch, seq, tm=512)


def _gqa_layer(x2, batch, seq, norm, w_qkv, q_gain, k_gain, w_out):
    nq = GQA_HEADS * HEAD_DIM
    nkv = GQA_KV_HEADS * HEAD_DIM
    modes = [MODE_QUERY] * (nq // PAIR) + [MODE_KEY] * (nkv // PAIR) + [MODE_PLAIN] * (nkv // PAIR)
    cos, sin = _axial_tables(seq)
    is_qk = [True] * (GQA_HEADS + GQA_KV_HEADS) + [False] * GQA_KV_HEADS
    w16 = _permute_qk_columns(w_qkv, is_qk, AXIAL_PERM).astype(BF16)
    qkv = qkv_project(x2, _row(norm), w16, modes, cos, sin, _row(q_gain[AXIAL_PERM]), _row(k_gain[AXIAL_PERM]),
                      batch, seq, tm=512, scale=HEAD_DIM ** -0.5 * LOG2_E, dil=1)
    o = gqa_attention(qkv.reshape(batch, seq, nq + 2 * nkv), tq=256, tk=512, streams=2)
    return matmul_residual(o.reshape(batch * seq, nq), w_out.astype(BF16), x2, tm=512)


def _dilated_layer(x2, batch, seq, norm, w_qkv, q_gain, k_gain, w_out):
    hw = DIL_HEADS * HEAD_DIM
    modes = [MODE_QUERY] * (hw // PAIR) + [MODE_KEY] * (hw // PAIR) + [MODE_PLAIN] * (hw // PAIR)
    cos, sin = _partial_tables(seq)
    is_qk = ([True] * (2 * DIL_HEADS) + [False] * DIL_HEADS) * len(DIL_PATTERNS)
    w16 = _permute_qk_columns(w_qkv, is_qk, PARTIAL_PERM).astype(BF16)
    outs, lses = [], []
    for g, (window, dil) in enumerate(DIL_PATTERNS):
        span = window // (2 * dil)
        assert seq % (dil * span) == 0
        qkv = qkv_project(x2, _row(norm), w16[:, g * 3 * hw:(g + 1) * 3 * hw], modes, cos, sin,
                          _row(q_gain[PARTIAL_PERM]), _row(k_gain[PARTIAL_PERM]), batch, seq,
                          tm=512, scale=HEAD_DIM ** -0.5 * LOG2_E, dil=dil)
        o, lse = dilated_attention(qkv, span, tq=min(4 * span, seq // dil))
        outs.append(o)
        lses.append(lse)
    return dilated_merge_project(outs, lses, w_out.astype(BF16), x2, batch, seq, tm=512)


def _dense_ffn_layer(x2, norm, w_gate_up, w_down):
    return swiglu_ffn(x2, _row(norm), w_gate_up.astype(BF16), w_down.astype(BF16), tm=1024, tf=512)


def _moe_layer(x2, norm, router, router_b, w_gate_up, w_down):
    tm, te = 512, 1024
    route = moe_router(x2, _row(norm), router, router_b, tm=tm)
    pos, zero_starts, tile_expert, meta, n_tiles = _moe_plan(route, te, tm)
    xs = moe_scatter(x2, pos, zero_starts, n_tiles, tm=tm, te=te)
    ys = moe_experts(xs, _row(norm), w_gate_up, w_down, tile_expert, meta, n_tiles,
                     tm=te, tf=512)
    return moe_combine(x2, ys, pos, route, tm=tm)


def kernel(x, l0_attn_norm, l0_rg_w_in, l0_rg_conv_w, l0_rg_conv_b, l0_rg_w_a, l0_rg_b_a, l0_rg_w_x, l0_rg_b_x, l0_rg_lambda, l0_rg_w_out, l0_ffn_norm, l0_ffn_w_gate_up, l0_ffn_w_down, l1_attn_norm, l1_gqa_w_qkv, l1_gqa_q_norm, l1_gqa_k_norm, l1_gqa_w_out, l1_ffn_norm, l1_moe_router, l1_moe_router_b, l1_moe_w_gate_up, l1_moe_w_down, l2_attn_norm, l2_dil_w_qkv, l2_dil_q_norm, l2_dil_k_norm, l2_dil_w_out, l2_ffn_norm, l2_ffn_w_gate_up, l2_ffn_w_down, l3_attn_norm, l3_rg_w_in, l3_rg_conv_w, l3_rg_conv_b, l3_rg_w_a, l3_rg_b_a, l3_rg_w_x, l3_rg_b_x, l3_rg_lambda, l3_rg_w_out, l3_ffn_norm, l3_moe_router, l3_moe_router_b, l3_moe_w_gate_up, l3_moe_w_down):
    batch, seq, d = x.shape
    x2 = x.reshape(batch * seq, d)
    x2 = _rglru_layer(x2, batch, seq, l0_attn_norm, l0_rg_w_in, l0_rg_conv_w, l0_rg_conv_b, l0_rg_w_a,
                      l0_rg_b_a, l0_rg_w_x, l0_rg_b_x, l0_rg_lambda, l0_rg_w_out)
    x2 = _dense_ffn_layer(x2, l0_ffn_norm, l0_ffn_w_gate_up, l0_ffn_w_down)
    x2 = _gqa_layer(x2, batch, seq, l1_attn_norm, l1_gqa_w_qkv, l1_gqa_q_norm, l1_gqa_k_norm, l1_gqa_w_out)
    x2 = _moe_layer(x2, l1_ffn_norm, l1_moe_router, l1_moe_router_b, l1_moe_w_gate_up, l1_moe_w_down)
    x2 = _dilated_layer(x2, batch, seq, l2_attn_norm, l2_dil_w_qkv, l2_dil_q_norm, l2_dil_k_norm, l2_dil_w_out)
    x2 = _dense_ffn_layer(x2, l2_ffn_norm, l2_ffn_w_gate_up, l2_ffn_w_down)
    x2 = _rglru_layer(x2, batch, seq, l3_attn_norm, l3_rg_w_in, l3_rg_conv_w, l3_rg_conv_b, l3_rg_w_a,
                      l3_rg_b_a, l3_rg_w_x, l3_rg_b_x, l3_rg_lambda, l3_rg_w_out)
    x2 = _moe_layer(x2, l3_ffn_norm, l3_moe_router, l3_moe_router_b, l3_moe_w_gate_up, l3_moe_w_down)
    return x2.reshape(batch, seq, d)
```
